```python
import math
import jax, jax.numpy as jnp
from jax import lax
import numpy as np

D_MODEL = 2048
BATCH = 16
SEQ = 256
DEPTH = 4
DEC_BATCH = 2
DEC_SEQ = 4096
PAST_LEN = 256

GRID_W = 64
HEAD_DIM = 128
N_Q_HEADS = 8
N_KV_HEADS = 2
ATTN_W = N_Q_HEADS * HEAD_DIM
KV_W = N_KV_HEADS * HEAD_DIM
HY_W = D_MODEL - ATTN_W
HY_ORDER = 2
POS_FREQS = 16
POS_FEAT = 2 * POS_FREQS + 1
FILTER_HIDDEN = 64
IN_W = ATTN_W + 2 * KV_W + (HY_ORDER + 1) * HY_W
Q_BLOCK = 128
ROPE_THETA = 10000.0
AXIS_ROT = HEAD_DIM // 2
D_FF = 5632
N_EXPERTS = 8
TOP_K = 2
N_DENSE = (DEPTH + 1) // 2
N_MOE = DEPTH // 2
EPS = 1e-6

kernel_name = 'hymba_hyena_diffusion_step'


def rmsnorm(x, g):
    xf = x.astype(jnp.float32)
    y = xf * lax.rsqrt(jnp.mean(xf * xf, axis=-1, keepdims=True) + EPS)
    return (y * g.astype(jnp.float32)).astype(x.dtype)


def grid_rope_tables(L):
    rows = L // GRID_W
    row = jnp.repeat(jnp.arange(rows, dtype=jnp.float32), GRID_W)
    col = jnp.tile(jnp.arange(GRID_W, dtype=jnp.float32), rows)
    inv = ROPE_THETA ** (-jnp.arange(0, AXIS_ROT, 2, dtype=jnp.float32) / AXIS_ROT)
    ang = jnp.concatenate([row[:, None] * inv, col[:, None] * inv], axis=-1)
    return jnp.cos(ang)[:, None, :], jnp.sin(ang)[:, None, :]


def apply_rope(x, cos, sin):
    xf = x.astype(jnp.float32)
    x1, x2 = xf[..., :AXIS_ROT], xf[..., AXIS_ROT:]
    return jnp.concatenate([x1 * cos - x2 * sin, x2 * cos + x1 * sin], axis=-1).astype(x.dtype)


def block_attention(q, k, v):
    B, Lq, H, Dh = q.shape
    KV = k.shape[2]
    G = H // KV
    nb = Lq // Q_BLOCK
    qb = q.reshape(B, nb, Q_BLOCK, KV, G, Dh).transpose(1, 0, 2, 3, 4, 5)
    kf = k.astype(jnp.float32)
    vf = v.astype(jnp.float32)
    scale = 1.0 / math.sqrt(Dh)

    def one_block(qblk):
        s = jnp.einsum('bqkgd,bskd->bkgqs', qblk.astype(jnp.float32), kf) * scale
        p = jax.nn.softmax(s, axis=-1)
        return jnp.einsum('bkgqs,bskd->bqkgd', p, vf)

    o = lax.map(one_block, qb)
    return o.transpose(1, 0, 2, 3, 4, 5).reshape(B, Lq, H * Dh).astype(q.dtype)


def hyena_filter_spectrum(L, l, P):
    t = jnp.arange(L, dtype=jnp.float32) / L
    fr = jnp.arange(1, POS_FREQS + 1, dtype=jnp.float32)
    ang = 2.0 * math.pi * t[:, None] * fr
    feat = jnp.concatenate([t[:, None], jnp.sin(ang), jnp.cos(ang)], axis=-1)
    h = jnp.sin(feat @ P['filt_w1'][l].astype(jnp.float32) + P['filt_b1'][l].astype(jnp.float32))
    h = jnp.sin(h @ P['filt_w2'][l].astype(jnp.float32) + P['filt_b2'][l].astype(jnp.float32))
    h = h @ P['filt_w3'][l].astype(jnp.float32)
    h = h * jnp.exp(-jnp.exp(P['filt_log_decay'][l].astype(jnp.float32)) * t[:, None])
    h = h.reshape(L, HY_ORDER, 2, HY_W)
    hf, hb = h[:, :, 0], h[:, :, 1]
    kc = jnp.concatenate([hf, jnp.zeros((1, HY_ORDER, HY_W), jnp.float32), hb[1:][::-1]], axis=0)
    return jnp.fft.rfft(kc, axis=0)


def hyena_mixer(u, l, P):
    B, L, _ = u.shape
    w = P['hy_conv_w'][l]
    up = jnp.pad(u, ((0, 0), (1, 1), (0, 0)))
    uc = up[:, :-2] * w[0] + up[:, 1:-1] * w[1] + up[:, 2:] * w[2] + P['hy_conv_b'][l]
    v, x1, x2 = jnp.split(uc, 3, axis=-1)
    kf = hyena_filter_spectrum(L, l, P)
    z = v.astype(jnp.float32)
    for o, gate in enumerate((x1, x2)):
        zf = jnp.fft.rfft(z, n=2 * L, axis=1)
        conv = jnp.fft.irfft(zf * kf[None, :, o], n=2 * L, axis=1)[:, :L]
        z = gate.astype(jnp.float32) * (conv + P['hy_skip'][l, o].astype(jnp.float32) * z)
    return z.astype(u.dtype)


def token_mixer(h, l, P, rope, ctx_kv):
    B, L, _ = h.shape
    proj = jnp.einsum('bld,de->ble', h, P['w_in'][l])
    q = proj[..., :ATTN_W].reshape(B, L, N_Q_HEADS, HEAD_DIM)
    k = proj[..., ATTN_W:ATTN_W + KV_W].reshape(B, L, N_KV_HEADS, HEAD_DIM)
    v = proj[..., ATTN_W + KV_W:ATTN_W + 2 * KV_W].reshape(B, L, N_KV_HEADS, HEAD_DIM)
    u = proj[..., ATTN_W + 2 * KV_W:]
    q = rmsnorm(q, P['q_norm_g'][l])
    k = rmsnorm(k, P['k_norm_g'][l])
    if ctx_kv is None:
        k_all, v_all = k, v
    else:
        cos, sin = rope
        q = apply_rope(q, cos, sin)
        k = apply_rope(k, cos, sin)
        k_all = jnp.concatenate([ctx_kv[0].astype(k.dtype), k], axis=1)
        v_all = jnp.concatenate([ctx_kv[1].astype(v.dtype), v], axis=1)
    attn = block_attention(q, k_all, v_all)
    hy = hyena_mixer(u, l, P)
    g = P['out_norm_g'][l]
    merged = jnp.concatenate([rmsnorm(attn, g[:ATTN_W]), rmsnorm(hy, g[ATTN_W:])], axis=-1)
    return jnp.einsum('ble,ed->bld', merged, P['w_out'][l]), (k, v)


def swiglu(t, wg, wu, wd):
    return (jax.nn.silu(t @ wg) * (t @ wu)) @ wd


def moe_ffn(h, i, P):
    B, L, D = h.shape
    t = h.reshape(B * L, D)
    logits = t.astype(jnp.float32) @ P['router_w'][i].astype(jnp.float32)
    top_v, top_i = lax.top_k(logits, TOP_K)
    wts = jax.nn.softmax(top_v, axis=-1)
    gates = jnp.sum(jax.nn.one_hot(top_i, N_EXPERTS, dtype=jnp.float32) * wts[..., None], axis=1)
    y = jnp.zeros_like(t)
    for e in range(N_EXPERTS):
        y = y + gates[:, e:e + 1].astype(t.dtype) * swiglu(
            t, P['moe_w_gate'][i, e], P['moe_w_up'][i, e], P['moe_w_down'][i, e])
    return y.reshape(B, L, D)


def channel_mixer(h, l, P):
    i = l // 2
    if l % 2 == 0:
        return swiglu(h, P['ffn_w_gate'][i], P['ffn_w_up'][i], P['ffn_w_down'][i])
    return moe_ffn(h, i, P)


def trunk_layer(x, cvec, l, P, rope, ctx_kv):
    mod = jnp.einsum('bd,de->be', jax.nn.silu(cvec), P['w_mod'][l]) + P['b_mod'][l]
    sh1, sc1, g1, sh2, sc2, g2 = jnp.split(mod[:, None, :], 6, axis=-1)
    h = rmsnorm(x, P['norm1_g'][l]) * (1 + sc1) + sh1
    mix, kv = token_mixer(h, l, P, rope, ctx_kv)
    x = x + g1 * mix
    h = rmsnorm(x, P['norm2_g'][l]) * (1 + sc2) + sh2
    x = x + g2 * channel_mixer(h, l, P)
    return x, kv


def setup_inputs(seed: int = 0) -> dict:
    key = jax.random.key(seed)
    ks = jax.random.split(key, 32)
    f32 = jnp.float32

    def nrm(k, shape, s):
        return jax.random.normal(k, shape, f32) * s

    def gain(k, shape):
        return 1.0 + 0.05 * jax.random.normal(k, shape, f32)

    kv_shape = (DEC_BATCH, DEPTH, PAST_LEN, N_KV_HEADS, HEAD_DIM)
    return {
        'x_prompt': nrm(ks[0], (BATCH, SEQ, D_MODEL), 1.0),
        'x_sample': nrm(ks[1], (DEC_BATCH, DEC_SEQ, D_MODEL), 1.0),
        'cache_k': nrm(ks[2], kv_shape, 1.0),
        'cache_v': nrm(ks[3], kv_shape, 1.0),
        'c': nrm(ks[4], (DEC_BATCH, D_MODEL), 1.0),
        'c_ctx': nrm(ks[5], (D_MODEL,), 1.0),
        'norm1_g': gain(ks[6], (DEPTH, D_MODEL)),
        'norm2_g': gain(ks[7], (DEPTH, D_MODEL)),
        'w_mod': nrm(ks[8], (DEPTH, D_MODEL, 6 * D_MODEL), D_MODEL ** -0.5),
        'b_mod': nrm(ks[9], (DEPTH, 6 * D_MODEL), 0.02),
        'w_in': nrm(ks[10], (DEPTH, D_MODEL, IN_W), D_MODEL ** -0.5),
        'q_norm_g': gain(ks[11], (DEPTH, HEAD_DIM)),
        'k_norm_g': gain(ks[12], (DEPTH, HEAD_DIM)),
        'hy_conv_w': nrm(ks[13], (DEPTH, 3, (HY_ORDER + 1) * HY_W), 3 ** -0.5),
        'hy_conv_b': nrm(ks[14], (DEPTH, (HY_ORDER + 1) * HY_W), 0.02),
        'filt_w1': nrm(ks[15], (DEPTH, POS_FEAT, FILTER_HIDDEN), POS_FEAT ** -0.5),
        'filt_b1': nrm(ks[16], (DEPTH, FILTER_HIDDEN), 0.1),
        'filt_w2': nrm(ks[17], (DEPTH, FILTER_HIDDEN, FILTER_HIDDEN), FILTER_HIDDEN ** -0.5),
        'filt_b2': nrm(ks[18], (DEPTH, FILTER_HIDDEN), 0.1),
        'filt_w3': nrm(ks[19], (DEPTH, FILTER_HIDDEN, 2 * HY_ORDER * HY_W), FILTER_HIDDEN ** -0.5),
        'filt_log_decay': jnp.log(jax.random.uniform(ks[20], (DEPTH, 2 * HY_ORDER * HY_W), f32, 3.0, 15.0)),
        'hy_skip': nrm(ks[21], (DEPTH, HY_ORDER, HY_W), 0.5),
        'out_norm_g': gain(ks[22], (DEPTH, D_MODEL)),
        'w_out': nrm(ks[23], (DEPTH, D_MODEL, D_MODEL), D_MODEL ** -0.5),
        'ffn_w_gate': nrm(ks[24], (N_DENSE, D_MODEL, D_FF), D_MODEL ** -0.5),
        'ffn_w_up': nrm(ks[25], (N_DENSE, D_MODEL, D_FF), D_MODEL ** -0.5),
        'ffn_w_down': nrm(ks[26], (N_DENSE, D_FF, D_MODEL), D_FF ** -0.5),
        'router_w': nrm(ks[27], (N_MOE, D_MODEL, N_EXPERTS), D_MODEL ** -0.5),
        'moe_w_gate': nrm(ks[28], (N_MOE, N_EXPERTS, D_MODEL, D_FF), D_MODEL ** -0.5),
        'moe_w_up': nrm(ks[29], (N_MOE, N_EXPERTS, D_MODEL, D_FF), D_MODEL ** -0.5),
        'moe_w_down': nrm(ks[30], (N_MOE, N_EXPERTS, D_FF, D_MODEL), D_FF ** -0.5),
        'final_norm_g': gain(ks[31], (D_MODEL,)),
    }


def reference(x_prompt, x_sample, cache_k, cache_v, c, c_ctx, norm1_g, norm2_g, w_mod, b_mod,
              w_in, q_norm_g, k_norm_g, hy_conv_w, hy_conv_b, filt_w1, filt_b1, filt_w2, filt_b2,
              filt_w3, filt_log_decay, hy_skip, out_norm_g, w_out, ffn_w_gate, ffn_w_up, ffn_w_down,
              router_w, moe_w_gate, moe_w_up, moe_w_down, final_norm_g):
    P = dict(norm1_g=norm1_g, norm2_g=norm2_g, w_mod=w_mod, b_mod=b_mod, w_in=w_in,
             q_norm_g=q_norm_g, k_norm_g=k_norm_g, hy_conv_w=hy_conv_w, hy_conv_b=hy_conv_b,
             filt_w1=filt_w1, filt_b1=filt_b1, filt_w2=filt_w2, filt_b2=filt_b2, filt_w3=filt_w3,
             filt_log_decay=filt_log_decay, hy_skip=hy_skip, out_norm_g=out_norm_g, w_out=w_out,
             ffn_w_gate=ffn_w_gate, ffn_w_up=ffn_w_up, ffn_w_down=ffn_w_down, router_w=router_w,
             moe_w_gate=moe_w_gate, moe_w_up=moe_w_up, moe_w_down=moe_w_down)

    xc = x_prompt
    ctx_c = c_ctx[None, :]
    new_k = []
    new_v = []
    for l in range(DEPTH):
        xc, (k_l, v_l) = trunk_layer(xc, ctx_c, l, P, None, None)
        new_k.append(k_l)
        new_v.append(v_l)
    new_cache_k = jnp.stack(new_k, axis=1)
    new_cache_v = jnp.stack(new_v, axis=1)

    rope = grid_rope_tables(x_sample.shape[1])
    xs = x_sample
    for l in range(DEPTH):
        xs, _ = trunk_layer(xs, c, l, P, rope, (cache_k[:, l], cache_v[:, l]))

    y_prompt = rmsnorm(xc, final_norm_g)
    y_sample = rmsnorm(xs, final_norm_g)
    return (y_prompt, y_sample, new_cache_k, new_cache_v)
```

```python
import cmath
import functools
import math

import jax
import jax.numpy as jnp
import numpy as np
from jax import lax
from jax.experimental import pallas as pl
from jax.experimental.pallas import tpu as pltpu

F32 = jnp.float32
BF16 = jnp.bfloat16

HEAD_DIM = 128
GRID_W = 64
ROPE_THETA = 10000.0
EPS = 1e-6
POS_FREQS = 16
HY_ORDER = 2
LANES = 128
MOD_ROWS = 8
VMEM_LIMIT_BYTES = 56 * 1024 * 1024


def _cparams(n_axes):
    return pltpu.CompilerParams(dimension_semantics=("arbitrary",) * n_axes,
                                vmem_limit_bytes=VMEM_LIMIT_BYTES)


def _pick(n, pref, mult):
    best = None
    d = mult
    while d <= min(n, pref):
        if n % d == 0:
            best = d
        d += mult
    assert best is not None, (n, pref, mult)
    return best


def _roundup(x, m):
    return (x + m - 1) // m * m


def _mod_kernel(c_ref, w_ref, b_ref, o_ref):
    c = c_ref[...]
    s = (c * jax.nn.sigmoid(c)).astype(BF16)
    o_ref[...] = jnp.dot(s, w_ref[...].astype(BF16), preferred_element_type=F32) + b_ref[...]


def _mod_all(cvecs, w_mod, b_mod):
    depth, d, n = w_mod.shape
    tn = _pick(n, 1024, LANES)
    return pl.pallas_call(
        _mod_kernel,
        grid=(depth, n // tn),
        in_specs=[pl.BlockSpec((MOD_ROWS, d), lambda l, j: (0, 0)),
                  pl.BlockSpec((None, d, tn), lambda l, j: (l, 0, j)),
                  pl.BlockSpec((None, 1, tn), lambda l, j: (l, 0, j))],
        out_specs=pl.BlockSpec((None, MOD_ROWS, tn), lambda l, j: (l, 0, j)),
        out_shape=jax.ShapeDtypeStruct((depth, MOD_ROWS, n), F32),
        compiler_params=_cparams(2),
        name="mod_all",
    )(cvecs, w_mod, b_mod.reshape(depth, 1, n))


def _rms(x, g):
    return x * lax.rsqrt(jnp.mean(x * x, axis=-1, keepdims=True) + EPS) * g


def _norm_mod_kernel(x_ref, g_ref, sc_ref, sh_ref, o_ref):
    y = _rms(x_ref[...], g_ref[...])
    o_ref[...] = (y * (1.0 + sc_ref[...]) + sh_ref[...]).astype(o_ref.dtype)


def _norm_kernel(x_ref, g_ref, o_ref):
    o_ref[...] = _rms(x_ref[...], g_ref[...]).astype(o_ref.dtype)


class _Stream:
    def __init__(self, dec_batch, dec_seq, batch, seq):
        self.dec_batch, self.dec_seq, self.batch, self.seq = dec_batch, dec_seq, batch, seq
        self.ts = dec_batch * dec_seq
        self.tc = batch * seq
        self.t = self.ts + self.tc

    def group(self, row):
        return jnp.minimum(row // self.dec_seq, self.dec_batch)

    def token_tile(self, pref):
        return _pick(math.gcd(self.dec_seq, self.tc), pref, 16)


def _mod_index(st, layer, which, tm):
    def index(i):
        return ((layer * MOD_ROWS + st.group(i * tm)) * 6 + which, 0, 0)
    return index


def _norm_mod(x, g, modr, st, layer, which_sc, which_sh):
    t, d = x.shape
    tm = st.token_tile(512)
    return pl.pallas_call(
        _norm_mod_kernel,
        grid=(t // tm,),
        in_specs=[pl.BlockSpec((tm, d), lambda i: (i, 0)),
                  pl.BlockSpec((None, 1, d), lambda i: (layer, 0, 0)),
                  pl.BlockSpec((None, 1, d), _mod_index(st, layer, which_sc, tm)),
                  pl.BlockSpec((None, 1, d), _mod_index(st, layer, which_sh, tm))],
        out_specs=pl.BlockSpec((tm, d), lambda i: (i, 0)),
        out_shape=jax.ShapeDtypeStruct((t, d), BF16),
        compiler_params=_cparams(1),
        name="norm_mod",
    )(x, g, modr, modr)


def _final_norm(x, g):
    t, d = x.shape
    tm = _pick(t, 512, 8)
    return pl.pallas_call(
        _norm_kernel,
        grid=(t // tm,),
        in_specs=[pl.BlockSpec((tm, d), lambda i: (i, 0)),
                  pl.BlockSpec((1, d), lambda i: (0, 0))],
        out_specs=pl.BlockSpec((tm, d), lambda i: (i, 0)),
        out_shape=jax.ShapeDtypeStruct((t, d), F32),
        compiler_params=_cparams(1),
        name="final_norm",
    )(x, g.reshape(1, d))


def _gmm_kernel(wid_ref, nv_ref, *refs, mode):
    i = pl.program_id(1)
    if mode == "swiglu":
        a_ref, wg_ref, wu_ref, o_ref, wgb, wub = refs
        wpairs = ((wg_ref, wgb), (wu_ref, wub))
    elif mode == "resid":
        a_ref, w_ref, x_ref, gate_ref, o_ref, wb = refs
        wpairs = ((w_ref, wb),)
    else:
        a_ref, w_ref, o_ref, wb = refs
        wpairs = ((w_ref, wb),)

    prev = wid_ref[jnp.maximum(i - 1, 0)]
    new_w = jnp.logical_or(i == 0, wid_ref[i] != prev)

    @pl.when(jnp.logical_and(new_w, i < nv_ref[0]))
    def _cast():
        for src, dst in wpairs:
            dst[...] = src[...].astype(BF16)

    @pl.when(i >= nv_ref[0])
    def _unused_tile():
        o_ref[...] = jnp.zeros_like(o_ref)

    @pl.when(i < nv_ref[0])
    def _compute():
        a = a_ref[...]
        if mode == "swiglu":
            g = jnp.dot(a, wgb[...], preferred_element_type=F32)
            u = jnp.dot(a, wub[...], preferred_element_type=F32)
            o_ref[...] = (g * jax.nn.sigmoid(g) * u).astype(o_ref.dtype)
        elif mode == "resid":
            acc = jnp.dot(a, wb[...], preferred_element_type=F32)
            o_ref[...] = x_ref[...] + gate_ref[...] * acc
        else:
            o_ref[...] = jnp.dot(a, wb[...], preferred_element_type=F32).astype(o_ref.dtype)


def _gmm(a, ws, wid, nvalid, *, mode, out_dtype, tm, tn, resid=None, name="gmm"):
    m = a.shape[0]
    _, k, n = ws[0].shape
    ka = a.shape[1]
    nt = m // tm
    if isinstance(wid, int):
        wid, nvalid = jnp.full((nt,), wid, jnp.int32), jnp.full((1,), nt, jnp.int32)
    assert m % tm == 0 and n % tn == 0 and wid.shape == (nt,)

    def a_index(j, i, wid_ref, nv_ref):
        return (jnp.minimum(i, nv_ref[0] - 1), 0)

    def w_index(j, i, wid_ref, nv_ref):
        return (wid_ref[i], 0, j)

    def o_index(j, i, wid_ref, nv_ref):
        return (i, j)

    in_specs = [pl.BlockSpec((tm, ka), a_index)]
    in_specs += [pl.BlockSpec((None, k, tn), w_index) for _ in ws]
    args = [a, *ws]
    if mode == "resid":
        x, modr, mod_index = resid
        in_specs += [pl.BlockSpec((tm, tn), o_index),
                     pl.BlockSpec((None, 1, tn),
                                  lambda j, i, wid_ref, nv_ref: mod_index(i)[:2] + (j,))]
        args += [x, modr]
    grid_spec = pltpu.PrefetchScalarGridSpec(
        num_scalar_prefetch=2,
        grid=(n // tn, nt),
        in_specs=in_specs,
        out_specs=pl.BlockSpec((tm, tn), o_index),
        scratch_shapes=[pltpu.VMEM((k, tn), BF16) for _ in ws],
    )
    return pl.pallas_call(
        functools.partial(_gmm_kernel, mode=mode),
        grid_spec=grid_spec,
        out_shape=jax.ShapeDtypeStruct((m, n), out_dtype),
        compiler_params=_cparams(2),
        name=name,
    )(wid, nvalid, *args)


def _qk_kernel(q_ref, k_ref, qg_ref, kg_ref, cos_ref, sin_ref, qo_ref, ko_ref, kc_ref, *, scale):
    cos = cos_ref[...]
    sin = sin_ref[...]

    def head(x, g):
        xn = _rms(x.astype(F32), g)
        return xn, xn * cos + pltpu.roll(xn, HEAD_DIM // 2, 1) * sin

    for h in range(q_ref.shape[1] // HEAD_DIM):
        sl = slice(h * HEAD_DIM, (h + 1) * HEAD_DIM)
        _, roped = head(q_ref[:, sl], qg_ref[...])
        qo_ref[:, sl] = (roped * scale).astype(qo_ref.dtype)
    for h in range(k_ref.shape[1] // HEAD_DIM):
        sl = slice(h * HEAD_DIM, (h + 1) * HEAD_DIM)
        xn, roped = head(k_ref[:, sl], kg_ref[...])
        ko_ref[:, sl] = roped.astype(ko_ref.dtype)
        kc_ref[:, sl] = xn


def _qk_norm_rope(proj, qg, kg, cos2, sin2, layer, a_w, kv_w, st):
    t = proj.shape[0]
    tm = st.token_tile(512)
    assert a_w % kv_w == 0
    return pl.pallas_call(
        functools.partial(_qk_kernel, scale=1.0 / math.sqrt(HEAD_DIM)),
        grid=(t // tm,),
        in_specs=[pl.BlockSpec((tm, a_w), lambda i: (i, 0)),
                  pl.BlockSpec((tm, kv_w), lambda i: (i, a_w // kv_w)),
                  pl.BlockSpec((None, 1, HEAD_DIM), lambda i: (layer, 0, 0)),
                  pl.BlockSpec((None, 1, HEAD_DIM), lambda i: (layer, 0, 0)),
                  pl.BlockSpec((tm, HEAD_DIM), lambda i: (i, 0)),
                  pl.BlockSpec((tm, HEAD_DIM), lambda i: (i, 0))],
        out_specs=[pl.BlockSpec((tm, a_w), lambda i: (i, 0)),
                   pl.BlockSpec((tm, kv_w), lambda i: (i, 0)),
                   pl.BlockSpec((tm, kv_w), lambda i: (i, 0))],
        out_shape=[jax.ShapeDtypeStruct((t, a_w), BF16),
                   jax.ShapeDtypeStruct((t, kv_w), BF16),
                   jax.ShapeDtypeStruct((t, kv_w), F32)],
        compiler_params=_cparams(1),
        name="qk_norm_rope",
    )(proj, proj, qg, kg, cos2, sin2)


def _rope_tables(st):
    pos = np.arange(st.dec_seq)
    inv = ROPE_THETA ** (-np.arange(0, HEAD_DIM // 2, 2, dtype=np.float64) / (HEAD_DIM // 2))
    ang = np.concatenate([(pos // GRID_W)[:, None] * inv, (pos % GRID_W)[:, None] * inv], axis=-1)
    cos2 = np.concatenate([np.cos(ang), np.cos(ang)], axis=-1)
    sin2 = np.concatenate([-np.sin(ang), np.sin(ang)], axis=-1)
    cos_all = np.concatenate([np.tile(cos2, (st.dec_batch, 1)), np.ones((st.tc, HEAD_DIM))])
    sin_all = np.concatenate([np.tile(sin2, (st.dec_batch, 1)), np.zeros((st.tc, HEAD_DIM))])
    return jnp.asarray(cos_all, F32), jnp.asarray(sin_all, F32)


KEY_CHUNK = 512


def _attn_kernel(*refs, g, has_cache):
    if has_cache:
        q_ref, ck_ref, cv_ref, k_ref, v_ref, o_ref = refs
    else:
        q_ref, k_ref, v_ref, o_ref = refs
    tq = q_ref.shape[0]
    q = jnp.concatenate([q_ref[:, h * HEAD_DIM:(h + 1) * HEAD_DIM] for h in range(g)], axis=0)
    lk = k_ref.shape[0]
    parts = []
    if has_cache:
        parts.append((ck_ref, cv_ref, 0, ck_ref.shape[0]))
    c0 = 0
    while c0 < lk:
        n = min(KEY_CHUNK, lk - c0)
        parts.append((k_ref, v_ref, c0, n))
        c0 += n
    m = acc = None
    for kr, vr, c0, n in parts:
        kc = kr[c0:c0 + n, :].astype(BF16)
        va = jnp.concatenate([vr[c0:c0 + n, :].astype(BF16), jnp.ones((n, HEAD_DIM), BF16)], axis=1)
        s = lax.dot_general(q, kc, (((1,), (1,)), ((), ())), preferred_element_type=F32)
        mc = jnp.max(s, axis=-1, keepdims=True)
        if m is None:
            m = mc
            acc = jnp.dot(jnp.exp(s - m).astype(BF16), va, preferred_element_type=F32)
        else:
            m_new = jnp.maximum(m, mc)
            acc = (jnp.exp(m - m_new) * acc
                   + jnp.dot(jnp.exp(s - m_new).astype(BF16), va, preferred_element_type=F32))
            m = m_new
    o = acc[:, :HEAD_DIM] / acc[:, HEAD_DIM:]
    for h in range(g):
        o_ref[:, h * HEAD_DIM:(h + 1) * HEAD_DIM] = o[h * tq:(h + 1) * tq].astype(o_ref.dtype)


def _attention(qn, kn, proj, v_col0, n_kv, nseq, seq_len, row0, cache=None):
    aw = qn.shape[1]
    g = aw // HEAD_DIM // n_kv
    tq = _pick(seq_len, 256, 16)
    nq = seq_len // tq
    assert row0 % seq_len == 0 and v_col0 % HEAD_DIM == 0
    in_specs = [pl.BlockSpec((tq, g * HEAD_DIM), lambda b, kv, i: (row0 // tq + b * nq + i, kv))]
    args = [qn]
    if cache is not None:
        ck, cv, layer = cache
        past = ck.shape[2]
        cspec = pl.BlockSpec((None, None, past, HEAD_DIM), lambda b, kv, i: (b, layer, 0, kv))
        in_specs += [cspec, cspec]
        args += [ck, cv]
    in_specs += [pl.BlockSpec((seq_len, HEAD_DIM), lambda b, kv, i: (row0 // seq_len + b, kv)),
                 pl.BlockSpec((seq_len, HEAD_DIM),
                              lambda b, kv, i: (row0 // seq_len + b, v_col0 // HEAD_DIM + kv))]
    args += [kn, proj]
    return pl.pallas_call(
        functools.partial(_attn_kernel, g=g, has_cache=cache is not None),
        grid=(nseq, n_kv, nq),
        in_specs=in_specs,
        out_specs=pl.BlockSpec((tq, g * HEAD_DIM), lambda b, kv, i: (b * nq + i, kv)),
        out_shape=jax.ShapeDtypeStruct((nseq * seq_len, aw), BF16),
        compiler_params=_cparams(3),
        name="attention",
    )(*args)


def _conv3_kernel(u_ref, w_ref, b_ref, o_ref):
    x = u_ref[...].astype(F32)
    n = x.shape[0]
    row = lax.broadcasted_iota(jnp.int32, x.shape, 0)
    prev = jnp.where(row == 0, 0.0, pltpu.roll(x, 1, 0))
    nxt = jnp.where(row == n - 1, 0.0, pltpu.roll(x, n - 1, 0))
    w = w_ref[...]
    o_ref[...] = (prev * w[0:1] + x * w[1:2] + nxt * w[2:3] + b_ref[...]).astype(o_ref.dtype)


def _conv3(proj, w, b, layer, col0, nseq, seq_len, row0):
    width = w.shape[-1]
    tc = _pick(math.gcd(width, col0), 512, LANES)
    assert row0 % seq_len == 0
    return pl.pallas_call(
        _conv3_kernel,
        grid=(nseq, width // tc),
        in_specs=[pl.BlockSpec((seq_len, tc), lambda s, j: (row0 // seq_len + s, col0 // tc + j)),
                  pl.BlockSpec((None, 3, tc), lambda s, j: (layer, 0, j)),
                  pl.BlockSpec((None, 1, tc), lambda s, j: (layer, 0, j))],
        out_specs=pl.BlockSpec((seq_len, tc), lambda s, j: (s, j)),
        out_shape=jax.ShapeDtypeStruct((nseq * seq_len, width), BF16),
        compiler_params=_cparams(2),
        name="hy_conv3",
    )(proj, w, b)


def _filter_kernel(feat_ref, w1_ref, b1_ref, w2_ref, b2_ref, w3_ref, ld_ref, o_ref, h_ref, *,
                   seq_len, hy_w):
    hp = lax.Precision.HIGHEST

    @pl.when(pl.program_id(1) == 0)
    def _hidden():
        h = jnp.sin(jnp.dot(feat_ref[...], w1_ref[...], precision=hp, preferred_element_type=F32)
                    + b1_ref[...])
        h_ref[...] = jnp.sin(jnp.dot(h, w2_ref[...], precision=hp, preferred_element_type=F32)
                             + b2_ref[...])

    h = jnp.dot(h_ref[...], w3_ref[...], precision=hp, preferred_element_type=F32)
    tl, tn = h.shape
    row = lax.broadcasted_iota(jnp.int32, h.shape, 0) + pl.program_id(0) * tl
    t = row.astype(F32) / seq_len
    h = h * jnp.exp(-jnp.exp(ld_ref[...]) * t)
    col = lax.broadcasted_iota(jnp.int32, h.shape, 1) + pl.program_id(1) * tn
    backward = (col // hy_w) % 2 == 1
    o_ref[...] = jnp.where(jnp.logical_and(backward, row == 0), 0.0, h).astype(o_ref.dtype)


def _filter_features(seq_len):
    t = np.arange(seq_len, dtype=np.float32) / np.float32(seq_len)
    fr = np.arange(1, POS_FREQS + 1, dtype=np.float32)
    ang = float(np.float32(2.0 * math.pi)) * t[:, None].astype(np.float64) * fr
    feat = np.concatenate([t[:, None], np.sin(ang), np.cos(ang)], axis=-1)
    out = np.zeros((seq_len, LANES), np.float32)
    out[:, :feat.shape[1]] = feat
    return jnp.asarray(out)


def _pad2(x, rows, cols):
    return jnp.pad(x, ((0, rows - x.shape[0]), (0, cols - x.shape[1])))


def _hyena_filters(seq_len, layer, p, hy_w):
    n = p["filt_w3"].shape[-1]
    feat = _filter_features(seq_len)
    w1 = _pad2(p["filt_w1"][layer], LANES, LANES)
    b1 = _pad2(p["filt_b1"][layer][None, :], 1, LANES)
    w2 = _pad2(p["filt_w2"][layer], LANES, LANES)
    b2 = _pad2(p["filt_b2"][layer][None, :], 1, LANES)
    w3 = _pad2(p["filt_w3"][layer], LANES, n)
    ld = p["filt_log_decay"][layer][None, :]
    tl = _pick(seq_len, 512, 16)
    tn = _pick(n, 512, LANES)
    return pl.pallas_call(
        functools.partial(_filter_kernel, seq_len=seq_len, hy_w=hy_w),
        grid=(seq_len // tl, n // tn),
        in_specs=[pl.BlockSpec((tl, LANES), lambda i, j: (i, 0)),
                  pl.BlockSpec((LANES, LANES), lambda i, j: (0, 0)),
                  pl.BlockSpec((1, LANES), lambda i, j: (0, 0)),
                  pl.BlockSpec((LANES, LANES), lambda i, j: (0, 0)),
                  pl.BlockSpec((1, LANES), lambda i, j: (0, 0)),
                  pl.BlockSpec((LANES, tn), lambda i, j: (0, j)),
                  pl.BlockSpec((1, tn), lambda i, j: (0, j))],
        out_specs=pl.BlockSpec((tl, tn), lambda i, j: (i, j)),
        out_shape=jax.ShapeDtypeStruct((seq_len, n), BF16),
        scratch_shapes=[pltpu.VMEM((tl, LANES), F32)],
        compiler_params=_cparams(2),
        name="hy_filter",
    )(feat, w1, b1, w2, b2, w3, ld)


class _Dft:
    def __init__(self, seq_len, radix):
        self.seq_len, self.r = seq_len, radix
        n = 2 * seq_len
        self.n1 = n // radix
        self.la = seq_len // radix
        k1 = self.n1 // 2 + 1
        self.k1p = _roundup(k1, 16)
        a = np.arange(self.la)[None, :]
        kk = np.arange(self.k1p)[:, None]
        valid = kk < k1
        th = 2.0 * math.pi * kk * a / self.n1
        fwd = np.concatenate([np.where(valid, np.cos(th), 0.0), np.where(valid, -np.sin(th), 0.0)])
        wgt = np.where((kk == 0) | (kk == self.n1 // 2), 1.0, 2.0) * valid / n
        inv = np.concatenate([wgt * np.cos(th), -wgt * np.sin(th)]).T
        self.fwd = jnp.asarray(fwd, F32)
        self.inv = jnp.asarray(inv, F32)
        ph = 2.0 * math.pi * kk / n
        ones = np.ones((1, LANES))
        self.tw_re = jnp.asarray(np.cos(ph) * ones, F32)
        self.tw_im = jnp.asarray(-np.sin(ph) * ones, F32)


def _cmul_const(x, w):
    wr, wi = w.real, w.imag
    if abs(wi) < 1e-12:
        if abs(wr - 1.0) < 1e-12:
            return x
        if abs(wr + 1.0) < 1e-12:
            return (-x[0], -x[1])
    if abs(wr) < 1e-12:
        if abs(wi - 1.0) < 1e-12:
            return (-x[1], x[0])
        if abs(wi + 1.0) < 1e-12:
            return (x[1], -x[0])
    return (x[0] * wr - x[1] * wi, x[0] * wi + x[1] * wr)


def _fft_list(xs, sign):
    n = len(xs)
    if n == 1:
        return xs
    ev = _fft_list(xs[0::2], sign)
    od = _fft_list(xs[1::2], sign)
    out = [None] * n
    for k in range(n // 2):
        t = _cmul_const(od[k], cmath.exp(sign * 2j * math.pi * k / n))
        out[k] = (ev[k][0] + t[0], ev[k][1] + t[1])
        out[k + n // 2] = (ev[k][0] - t[0], ev[k][1] - t[1])
    return out


ROW_CHUNK = 16


def _lane_tile(x, width):
    return x if width == LANES else jnp.concatenate([x] * (width // LANES), axis=1)


def _twiddles(tw_re_ref, tw_im_ref, rows, r, tc):
    if r == 1:
        return []
    w1 = (tw_re_ref[rows, :], tw_im_ref[rows, :])
    ws = [w1]
    for _ in range(r - 2):
        pr, pi = ws[-1]
        ws.append((pr * w1[0] - pi * w1[1], pr * w1[1] + pi * w1[0]))
    return [(_lane_tile(wr, tc), _lane_tile(wi, tc)) for wr, wi in ws]


def _spectrum_chunk(y_ref, tws, c0, k1p, r):
    rows = pl.ds(c0, ROW_CHUNK)
    rows_im = pl.ds(k1p + c0, ROW_CHUNK)
    ts = []
    for b in range(r):
        yr = y_ref[b, rows, :]
        yi = y_ref[b, rows_im, :]
        if b == 0:
            ts.append((yr, yi))
        else:
            wr, wi = tws[b - 1]
            ts.append((yr * wr - yi * wi, yr * wi + yi * wr))
    return _fft_list(ts, -1)


def _stage1(f_ref, x_ref, xf_ref, y_ref, r, la):
    f = f_ref[...].astype(BF16)
    if r == 1:
        y_ref[0] = jnp.dot(f, x_ref[...], preferred_element_type=F32)
        return
    nq = xf_ref.shape[0]
    for q in range(nq):
        xf_ref[q] = x_ref[:, q * LANES:(q + 1) * LANES].astype(F32)
    for b in range(r):
        xb = jnp.concatenate([xf_ref[q, pl.ds(b, la, stride=r), :] for q in range(nq)], axis=1)
        y_ref[b] = jnp.dot(f, xb.astype(BF16), preferred_element_type=F32)


def _spectrum_kernel(*refs, r, k1p, la):
    if r > 1:
        f_ref, tw_re_ref, tw_im_ref, hf_ref, hb_ref, kr_ref, ki_ref, xf_ref, y_ref, acc_ref = refs
    else:
        f_ref, hf_ref, hb_ref, kr_ref, ki_ref, xf_ref, y_ref, acc_ref = refs
        tw_re_ref = tw_im_ref = None
    tc = y_ref.shape[-1]

    def sweep(src_ref, backward):
        _stage1(f_ref, src_ref, xf_ref, y_ref, r, la)

        def chunk(c, carry):
            c0 = pl.multiple_of(c * ROW_CHUNK, ROW_CHUNK)
            rows = pl.ds(c0, ROW_CHUNK)
            rows_im = pl.ds(k1p + c0, ROW_CHUNK)
            x = _spectrum_chunk(y_ref, _twiddles(tw_re_ref, tw_im_ref, rows, r, tc), c0, k1p, r)
            for k2 in range(r):
                if backward:
                    kr_ref[k2, rows, :] = (acc_ref[k2, rows, :] + x[k2][0]).astype(kr_ref.dtype)
                    ki_ref[k2, rows, :] = (acc_ref[k2, rows_im, :] - x[k2][1]).astype(ki_ref.dtype)
                else:
                    acc_ref[k2, rows, :] = x[k2][0]
                    acc_ref[k2, rows_im, :] = x[k2][1]
            return carry

        lax.fori_loop(0, k1p // ROW_CHUNK, chunk, 0)

    sweep(hf_ref, False)
    sweep(hb_ref, True)


def _filter_spectrum(taps, dft, hy_w):
    r, la, k1p, seq_len = dft.r, dft.la, dft.k1p, dft.seq_len
    tc = _pick(hy_w, 256, LANES)
    nct = hy_w // tc

    def tap_spec(direction):
        return pl.BlockSpec((seq_len, tc), lambda o, j: (0, (o * 2 + direction) * nct + j))

    const = [pl.BlockSpec(dft.fwd.shape, lambda o, j: (0, 0))]
    cargs = [dft.fwd]
    if r > 1:
        const += [pl.BlockSpec(dft.tw_re.shape, lambda o, j: (0, 0))] * 2
        cargs += [dft.tw_re, dft.tw_im]
    out_spec = pl.BlockSpec((None, r, k1p, tc), lambda o, j: (o, 0, 0, j))
    shape = jax.ShapeDtypeStruct((HY_ORDER, r, k1p, hy_w), BF16)
    return pl.pallas_call(
        functools.partial(_spectrum_kernel, r=r, k1p=k1p, la=la),
        grid=(HY_ORDER, nct),
        in_specs=const + [tap_spec(0), tap_spec(1)],
        out_specs=[out_spec, out_spec],
        out_shape=[shape, shape],
        scratch_shapes=[pltpu.VMEM((tc // LANES, seq_len, LANES), F32),
                        pltpu.VMEM((r, 2 * k1p, tc), F32),
                        pltpu.VMEM((r, 2 * k1p, tc), F32)],
        compiler_params=_cparams(2),
        name="hy_filter_spectrum",
    )(*cargs, taps, taps)


def _longconv_kernel(*refs, r, k1p, la):
    if r > 1:
        (f_ref, fi_ref, tw_re_ref, tw_im_ref, z_ref, gate_ref, skip_ref, kr_ref, ki_ref, o_ref,
         xf_ref, y_ref, g_ref) = refs
    else:
        f_ref, fi_ref, z_ref, gate_ref, skip_ref, kr_ref, ki_ref, o_ref, xf_ref, y_ref, g_ref = refs
        tw_re_ref = tw_im_ref = None
    tc = y_ref.shape[-1]
    _stage1(f_ref, z_ref, xf_ref, y_ref, r, la)

    def chunk(c, carry):
        c0 = pl.multiple_of(c * ROW_CHUNK, ROW_CHUNK)
        rows = pl.ds(c0, ROW_CHUNK)
        rows_im = pl.ds(k1p + c0, ROW_CHUNK)
        tws = _twiddles(tw_re_ref, tw_im_ref, rows, r, tc)
        x = _spectrum_chunk(y_ref, tws, c0, k1p, r)
        prod = []
        for k2 in range(r):
            kr = kr_ref[k2, rows, :].astype(F32)
            ki = ki_ref[k2, rows, :].astype(F32)
            prod.append((x[k2][0] * kr - x[k2][1] * ki, x[k2][0] * ki + x[k2][1] * kr))
        s = _fft_list(prod, +1)
        for b in range(r):
            if b == 0:
                gr, gi = s[0]
            else:
                wr, wi = tws[b - 1]
                gr = s[b][0] * wr + s[b][1] * wi
                gi = s[b][1] * wr - s[b][0] * wi
            g_ref[b, rows, :] = gr.astype(g_ref.dtype)
            g_ref[b, rows_im, :] = gi.astype(g_ref.dtype)
        return carry

    lax.fori_loop(0, k1p // ROW_CHUNK, chunk, 0)
    fi = fi_ref[...].astype(BF16)
    for b in range(r):
        yb = jnp.dot(fi, g_ref[b], preferred_element_type=F32)
        for q in range(xf_ref.shape[0]):
            xf_ref[q, pl.ds(b, la, stride=r), :] = yb[:, q * LANES:(q + 1) * LANES]
    y = jnp.concatenate([xf_ref[q] for q in range(xf_ref.shape[0])], axis=1)
    z = z_ref[...].astype(F32)
    o_ref[...] = (gate_ref[...].astype(F32) * (y + skip_ref[...] * z)).astype(o_ref.dtype)


def _longconv(z_arr, z_col0, gate_arr, gate_col0, skip, skip_idx, kf_re, kf_im, order, dft,
              nseq, hy_w):
    r, la, k1p, seq_len = dft.r, dft.la, dft.k1p, dft.seq_len
    tc = _pick(hy_w, 256 if r > 1 else 1024, LANES)
    assert z_col0 % tc == 0 and gate_col0 % tc == 0

    cdim = lambda s, j: (0, 0)
    const = [pl.BlockSpec(dft.fwd.shape, cdim), pl.BlockSpec(dft.inv.shape, cdim)]
    cargs = [dft.fwd, dft.inv]
    if r > 1:
        const += [pl.BlockSpec(dft.tw_re.shape, cdim)] * 2
        cargs += [dft.tw_re, dft.tw_im]
    in_specs = const + [
        pl.BlockSpec((seq_len, tc), lambda s, j: (s, z_col0 // tc + j)),
        pl.BlockSpec((seq_len, tc), lambda s, j: (s, gate_col0 // tc + j)),
        pl.BlockSpec((None, 1, tc), lambda s, j: (skip_idx, 0, j)),
        pl.BlockSpec((None, r, k1p, tc), lambda s, j: (order, 0, 0, j)),
        pl.BlockSpec((None, r, k1p, tc), lambda s, j: (order, 0, 0, j)),
    ]
    return pl.pallas_call(
        functools.partial(_longconv_kernel, r=r, k1p=k1p, la=la),
        grid=(nseq, hy_w // tc),
        in_specs=in_specs,
        out_specs=pl.BlockSpec((seq_len, tc), lambda s, j: (s, j)),
        out_shape=jax.ShapeDtypeStruct((nseq * seq_len, hy_w), BF16),
        scratch_shapes=[pltpu.VMEM((tc // LANES, seq_len, LANES), F32),
                        pltpu.VMEM((r, 2 * k1p, tc), F32),
                        pltpu.VMEM((r, 2 * k1p, tc), BF16)],
        compiler_params=_cparams(2),
        name="hy_longconv",
    )(*cargs, z_arr, gate_arr, skip, kf_re, kf_im)


def _hyena(proj, layer, p, st, u_col0, hy_w, dfts):
    skip = p["hy_skip"].reshape(-1, 1, hy_w)
    outs = []
    for nseq, seq_len, row0 in ((st.dec_batch, st.dec_seq, 0), (st.batch, st.seq, st.ts)):
        dft = dfts[seq_len]
        uc = _conv3(proj, p["hy_conv_w"], p["hy_conv_b"].reshape(-1, 1, 3 * hy_w), layer,
                    u_col0, nseq, seq_len, row0)
        taps = _hyena_filters(seq_len, layer, p, hy_w)
        kf_re, kf_im = _filter_spectrum(taps, dft, hy_w)
        z = _longconv(uc, 0, uc, hy_w, skip, layer * HY_ORDER, kf_re, kf_im, 0, dft, nseq, hy_w)
        z = _longconv(z, 0, uc, 2 * hy_w, skip, layer * HY_ORDER + 1, kf_re, kf_im, 1, dft,
                      nseq, hy_w)
        outs.append(z)
    return outs


def _merge_kernel(as_ref, ac_ref, hs_ref, hc_ref, ga_ref, gh_ref, o_ref, *, n_sample_tiles):
    aw = as_ref.shape[1]
    sample = pl.program_id(0) < n_sample_tiles
    a = jnp.where(sample, as_ref[...], ac_ref[...]).astype(F32)
    h = jnp.where(sample, hs_ref[...], hc_ref[...]).astype(F32)
    o_ref[:, :aw] = _rms(a, ga_ref[...]).astype(o_ref.dtype)
    o_ref[:, aw:] = _rms(h, gh_ref[...]).astype(o_ref.dtype)


def _merge_norm(attn_s, attn_c, hy_s, hy_c, g, layer, st):
    aw, hw = attn_s.shape[1], hy_s.shape[1]
    assert aw == hw
    tm = st.token_tile(512)
    ns = st.ts // tm
    g3 = g.reshape(g.shape[0], 1, aw + hw)
    s_idx = lambda i: (jnp.minimum(i, ns - 1), 0)
    c_idx = lambda i: (jnp.maximum(i - ns, 0), 0)
    return pl.pallas_call(
        functools.partial(_merge_kernel, n_sample_tiles=ns),
        grid=(st.t // tm,),
        in_specs=[pl.BlockSpec((tm, aw), s_idx), pl.BlockSpec((tm, aw), c_idx),
                  pl.BlockSpec((tm, hw), s_idx), pl.BlockSpec((tm, hw), c_idx),
                  pl.BlockSpec((None, 1, aw), lambda i: (layer, 0, 0)),
                  pl.BlockSpec((None, 1, hw), lambda i: (layer, 0, 1))],
        out_specs=pl.BlockSpec((tm, aw + hw), lambda i: (i, 0)),
        out_shape=jax.ShapeDtypeStruct((st.t, aw + hw), BF16),
        compiler_params=_cparams(1),
        name="merge_norm",
    )(attn_s, attn_c, hy_s, hy_c, g3, g3)


def _route_kernel(x_ref, g_ref, sc_ref, sh_ref, rw_ref, hp_ref, w1_ref, w2_ref, code_ref,
                  rank_ref, cnt_ref, carry_ref, *, n_experts):
    i = pl.program_id(0)

    @pl.when(i == 0)
    def _init():
        carry_ref[...] = jnp.zeros_like(carry_ref)

    h = _rms(x_ref[...], g_ref[...]) * (1.0 + sc_ref[...]) + sh_ref[...]
    hp_ref[...] = h

    logits = jnp.dot(h, rw_ref[...], precision=lax.Precision.HIGHEST, preferred_element_type=F32)
    lane = lax.broadcasted_iota(jnp.int32, logits.shape, 1)
    neg = jnp.float32(-jnp.inf)
    logits = jnp.where(lane < n_experts, logits, neg)
    m1 = jnp.max(logits, axis=-1, keepdims=True)
    i1 = jnp.min(jnp.where(logits == m1, lane, LANES), axis=-1, keepdims=True)
    rest = jnp.where(lane == i1, neg, logits)
    m2 = jnp.max(rest, axis=-1, keepdims=True)
    i2 = jnp.min(jnp.where(rest == m2, lane, LANES), axis=-1, keepdims=True)
    e2 = jnp.exp(m2 - m1)
    w1 = 1.0 / (1.0 + e2)
    w2 = e2 / (1.0 + e2)
    w1_ref[...] = jnp.broadcast_to(w1, w1_ref.shape)
    w2_ref[...] = jnp.broadcast_to(w2, w2_ref.shape)

    code = jnp.where(lane == i1, 1.0, jnp.where(lane == i2, 2.0, 0.0))
    code_t = code.T[:code_ref.shape[0], :]
    code_ref[...] = code_t.astype(jnp.int32)
    sel = (code_t > 0.0).astype(BF16)
    tm = sel.shape[1]
    upper = (lax.broadcasted_iota(jnp.int32, (tm, tm), 0)
             < lax.broadcasted_iota(jnp.int32, (tm, tm), 1)).astype(BF16)
    excl = jnp.dot(sel, upper, preferred_element_type=F32)
    carry = carry_ref[...]
    rank_ref[...] = (excl + carry[:, :1]).astype(jnp.int32)
    carry = carry + jnp.sum(sel.astype(F32), axis=-1, keepdims=True)
    carry_ref[...] = carry
    cnt_ref[...] = carry.astype(jnp.int32)


def _route(x, g, modr, rw, st, layer, n_experts):
    t, d = x.shape
    tm = st.token_tile(512)
    assert tm % LANES == 0 and n_experts <= 8
    rwp = _pad2(rw, d, LANES)
    row_spec = pl.BlockSpec((tm, LANES), lambda i: (i, 0))
    col_spec = pl.BlockSpec((8, tm), lambda i: (0, i))
    return pl.pallas_call(
        functools.partial(_route_kernel, n_experts=n_experts),
        grid=(t // tm,),
        in_specs=[pl.BlockSpec((tm, d), lambda i: (i, 0)),
                  pl.BlockSpec((None, 1, d), lambda i: (layer, 0, 0)),
                  pl.BlockSpec((None, 1, d), _mod_index(st, layer, 4, tm)),
                  pl.BlockSpec((None, 1, d), _mod_index(st, layer, 3, tm)),
                  pl.BlockSpec((d, LANES), lambda i: (0, 0))],
        out_specs=[pl.BlockSpec((tm, d), lambda i: (i, 0)), row_spec, row_spec,
                   col_spec, col_spec, pl.BlockSpec((8, LANES), lambda i: (0, 0))],
        out_shape=[jax.ShapeDtypeStruct((t, d), F32),
                   jax.ShapeDtypeStruct((t, LANES), F32),
                   jax.ShapeDtypeStruct((t, LANES), F32),
                   jax.ShapeDtypeStruct((8, t), jnp.int32),
                   jax.ShapeDtypeStruct((8, t), jnp.int32),
                   jax.ShapeDtypeStruct((8, LANES), jnp.int32)],
        scratch_shapes=[pltpu.VMEM((8, LANES), F32)],
        compiler_params=_cparams(1),
        name="moe_route",
    )(x, g, modr, modr, rwp)


DISPATCH_ROWS = 256


def _dispatch_kernel(tok_ref, h_ref, xs_ref, rows_ref, sem):
    td = tok_ref.shape[-1]

    def row_copy(src_row, dst_row):
        return pltpu.make_async_copy(h_ref.at[pl.ds(src_row, 1)], rows_ref.at[pl.ds(dst_row, 1)],
                                     sem)

    def issue(rr, carry):
        row_copy(tok_ref[0, rr], rr).start()
        return carry

    lax.fori_loop(0, td, issue, 0, unroll=8)

    def drain(rr, carry):
        row_copy(0, 0).wait()
        return carry

    lax.fori_loop(0, td, drain, 0, unroll=8)
    xs_ref[...] = rows_ref[...].astype(xs_ref.dtype)


def _dispatch(h, slot_token):
    n_slots = slot_token.shape[0]
    d = h.shape[1]
    td = _pick(n_slots, DISPATCH_ROWS, LANES)
    nt = n_slots // td
    return pl.pallas_call(
        _dispatch_kernel,
        grid=(nt,),
        in_specs=[pl.BlockSpec((None, 1, td), lambda i: (i, 0, 0), memory_space=pltpu.SMEM),
                  pl.BlockSpec(memory_space=pl.ANY)],
        out_specs=pl.BlockSpec((td, d), lambda i: (i, 0)),
        out_shape=jax.ShapeDtypeStruct((n_slots, d), BF16),
        scratch_shapes=[pltpu.VMEM((td, d), F32), pltpu.SemaphoreType.DMA(())],
        compiler_params=_cparams(1),
        name="moe_dispatch",
    )(slot_token.reshape(nt, 1, td), h)


def _combine_kernel(s0_ref, s1_ref, ys_ref, x_ref, gate_ref, w1_ref, w2_ref, o_ref, ya, yb, sem):
    td = s0_ref.shape[-1]

    def row_copy(src_row, buf, dst_row):
        return pltpu.make_async_copy(ys_ref.at[pl.ds(src_row, 1)], buf.at[pl.ds(dst_row, 1)], sem)

    def issue(rr, carry):
        row_copy(s0_ref[0, rr], ya, rr).start()
        row_copy(s1_ref[0, rr], yb, rr).start()
        return carry

    lax.fori_loop(0, td, issue, 0, unroll=8)

    def drain(rr, carry):
        row_copy(0, ya, 0).wait()
        row_copy(0, yb, 0).wait()
        return carry

    lax.fori_loop(0, td, drain, 0, unroll=8)
    w1 = _lane_tile(w1_ref[...], ya.shape[1])
    w2 = _lane_tile(w2_ref[...], ya.shape[1])
    o_ref[...] = x_ref[...] + gate_ref[...] * (w1 * ya[...] + w2 * yb[...])


def _combine(ys, slot0, slot1, x, modr, w1, w2, st, layer):
    t, d = x.shape
    td = _pick(math.gcd(st.dec_seq, st.tc), DISPATCH_ROWS, LANES)
    nt = t // td
    smem = lambda: pl.BlockSpec((None, 1, td), lambda i: (i, 0, 0), memory_space=pltpu.SMEM)
    return pl.pallas_call(
        _combine_kernel,
        grid=(nt,),
        in_specs=[smem(), smem(),
                  pl.BlockSpec(memory_space=pl.ANY),
                  pl.BlockSpec((td, d), lambda i: (i, 0)),
                  pl.BlockSpec((None, 1, d), _mod_index(st, layer, 5, td)),
                  pl.BlockSpec((td, LANES), lambda i: (i, 0)),
                  pl.BlockSpec((td, LANES), lambda i: (i, 0))],
        out_specs=pl.BlockSpec((td, d), lambda i: (i, 0)),
        out_shape=jax.ShapeDtypeStruct((t, d), F32),
        scratch_shapes=[pltpu.VMEM((td, d), F32), pltpu.VMEM((td, d), F32),
                        pltpu.SemaphoreType.DMA(())],
        compiler_params=_cparams(1),
        name="moe_combine",
    )(slot0.reshape(nt, 1, td), slot1.reshape(nt, 1, td), ys, x, modr, w1, w2)


MOE_ROWS = 512


def _moe_layer(x, p, modr, st, layer, moe_idx):
    t, d = x.shape
    n_experts = p["router_w"].shape[-1]
    d_ff = p["moe_w_gate"].shape[-1]
    tm = MOE_ROWS
    hp, w1, w2, code, rank, cnt = _route(x, p["norm2_g"].reshape(-1, 1, d), modr,
                                         p["router_w"][moe_idx], st, layer, n_experts)
    counts = cnt[:n_experts, 0]
    padded = (counts + tm - 1) // tm * tm
    ends = jnp.cumsum(padded)
    starts = ends - padded
    n_tiles = (2 * t) // tm + n_experts
    n_slots = n_tiles * tm
    dest = starts[:, None] + rank[:n_experts]
    slot0 = jnp.sum(jnp.where(code[:n_experts] == 1, dest, 0), axis=0).astype(jnp.int32)
    slot1 = jnp.sum(jnp.where(code[:n_experts] == 2, dest, 0), axis=0).astype(jnp.int32)
    tile_row = jnp.arange(n_tiles, dtype=jnp.int32) * tm
    tile_expert = jnp.minimum(jnp.sum(tile_row[:, None] >= ends[None, :], axis=1), n_experts - 1)
    wid = (moe_idx * n_experts + tile_expert).astype(jnp.int32)
    nvalid = (ends[-1:] // tm).astype(jnp.int32)
    expert_ids = jnp.arange(n_experts, dtype=jnp.int32)[:, None]
    token_ids = jnp.arange(t, dtype=jnp.int32)
    e0 = jnp.sum(jnp.where(code[:n_experts] == 1, expert_ids, 0), axis=0)
    e1 = jnp.sum(jnp.where(code[:n_experts] == 2, expert_ids, 0), axis=0)
    pair_token = jnp.sort(jnp.concatenate([e0 * t + token_ids, e1 * t + token_ids])) % t
    slot_expert = jnp.repeat(tile_expert, tm)
    local = jnp.arange(n_slots, dtype=jnp.int32) - starts[slot_expert]
    compact_start = jnp.cumsum(counts) - counts
    compact = compact_start[slot_expert] + jnp.clip(local, 0, jnp.maximum(counts[slot_expert] - 1, 0))
    slot_token = pair_token[jnp.clip(compact, 0, 2 * t - 1)].astype(jnp.int32)

    xs = _dispatch(hp, slot_token)
    wg = p["moe_w_gate"].reshape(-1, d, d_ff)
    wu = p["moe_w_up"].reshape(-1, d, d_ff)
    wd = p["moe_w_down"].reshape(-1, d_ff, d)
    hidden = _gmm(xs, (wg, wu), wid, nvalid, mode="swiglu", out_dtype=BF16, tm=tm,
                  tn=_pick(d_ff, 512, LANES), name="moe_gate_up")
    ys = _gmm(hidden, (wd,), wid, nvalid, mode="plain", out_dtype=F32, tm=tm,
              tn=_pick(d, 512, LANES), name="moe_down")
    return _combine(ys, slot0, slot1, x, modr, w1, w2, st, layer)


def _dense_layer(x, p, modr, st, layer, ffn_idx):
    t, d = x.shape
    d_ff = p["ffn_w_gate"].shape[-1]
    h = _norm_mod(x, p["norm2_g"].reshape(-1, 1, d), modr, st, layer, 4, 3)
    hidden = _gmm(h, (p["ffn_w_gate"], p["ffn_w_up"]), ffn_idx, None, mode="swiglu",
                  out_dtype=BF16, tm=st.token_tile(1024), tn=_pick(d_ff, 512, LANES),
                  name="ffn_gate_up")
    tm = st.token_tile(512)
    return _gmm(hidden, (p["ffn_w_down"],), ffn_idx, None, mode="resid", out_dtype=F32, tm=tm,
                tn=_pick(d, 512, LANES), resid=(x, modr, _mod_index(st, layer, 5, tm)),
                name="ffn_down")


def _dft_radix(seq_len):
    r = 1
    while seq_len // r > 512 and r < 8:
        r *= 2
    return r


def kernel(x_prompt, x_sample, cache_k, cache_v, c, c_ctx, norm1_g, norm2_g, w_mod, b_mod, w_in,
           q_norm_g, k_norm_g, hy_conv_w, hy_conv_b, filt_w1, filt_b1, filt_w2, filt_b2, filt_w3,
           filt_log_decay, hy_skip, out_norm_g, w_out, ffn_w_gate, ffn_w_up, ffn_w_down, router_w,
           moe_w_gate, moe_w_up, moe_w_down, final_norm_g):
    p = dict(norm2_g=norm2_g, hy_conv_w=hy_conv_w, hy_conv_b=hy_conv_b, filt_w1=filt_w1,
             filt_b1=filt_b1, filt_w2=filt_w2, filt_b2=filt_b2, filt_w3=filt_w3,
             filt_log_decay=filt_log_decay, hy_skip=hy_skip, ffn_w_gate=ffn_w_gate,
             ffn_w_up=ffn_w_up, ffn_w_down=ffn_w_down, router_w=router_w, moe_w_gate=moe_w_gate,
             moe_w_up=moe_w_up, moe_w_down=moe_w_down)
    batch, seq, d = x_prompt.shape
    dec_batch, dec_seq, _ = x_sample.shape
    depth = w_in.shape[0]
    in_w = w_in.shape[-1]
    n_kv = cache_k.shape[3]
    kv_w = n_kv * HEAD_DIM
    a_w = (3 * d + 2 * kv_w - in_w) // 2
    hy_w = d - a_w
    u_col0 = a_w + 2 * kv_w
    past = cache_k.shape[2]
    st = _Stream(dec_batch, dec_seq, batch, seq)
    assert dec_batch + 1 <= MOD_ROWS

    x = jnp.concatenate([x_sample.reshape(st.ts, d), x_prompt.reshape(st.tc, d)], axis=0)
    cvecs = jnp.zeros((MOD_ROWS, d), F32).at[:dec_batch].set(c).at[dec_batch].set(c_ctx)
    modr = _mod_all(cvecs, w_mod, b_mod).reshape(depth * MOD_ROWS * 6, 1, d)
    cos2, sin2 = _rope_tables(st)
    dfts = {n: _Dft(n, _dft_radix(n)) for n in {dec_seq, seq}}
    norm1_g3 = norm1_g.reshape(depth, 1, d)
    qg3 = q_norm_g.reshape(depth, 1, HEAD_DIM)
    kg3 = k_norm_g.reshape(depth, 1, HEAD_DIM)
    tm = st.token_tile(1024)
    cache_k4 = cache_k.reshape(dec_batch, depth, past, kv_w)
    cache_v4 = cache_v.reshape(dec_batch, depth, past, kv_w)

    new_k, new_v = [], []
    for layer in range(depth):
        h = _norm_mod(x, norm1_g3, modr, st, layer, 1, 0)
        proj = _gmm(h, (w_in,), layer, None, mode="plain", out_dtype=BF16, tm=tm,
                    tn=_pick(in_w, 768, LANES), name="w_in")
        qn, kn, kc = _qk_norm_rope(proj, qg3, kg3, cos2, sin2, layer, a_w, kv_w, st)
        new_k.append(kc[st.ts:].reshape(batch, seq, n_kv, HEAD_DIM))
        new_v.append(proj[st.ts:, a_w + kv_w:a_w + 2 * kv_w].astype(F32)
                     .reshape(batch, seq, n_kv, HEAD_DIM))

        attn_s = _attention(qn, kn, proj, a_w + kv_w, n_kv, dec_batch, dec_seq, 0,
                            cache=(cache_k4, cache_v4, layer))
        attn_c = _attention(qn, kn, proj, a_w + kv_w, n_kv, batch, seq, st.ts)
        hy_s, hy_c = _hyena(proj, layer, p, st, u_col0, hy_w, dfts)
        merged = _merge_norm(attn_s, attn_c, hy_s, hy_c, out_norm_g, layer, st)
        x = _gmm(merged, (w_out,), layer, None, mode="resid", out_dtype=F32, tm=tm,
                 tn=_pick(d, 1024, LANES), resid=(x, modr, _mod_index(st, layer, 2, tm)),
                 name="w_out")
        if layer % 2 == 0:
            x = _dense_layer(x, p, modr, st, layer, layer // 2)
        else:
            x = _moe_layer(x, p, modr, st, layer, layer // 2)

    y = _final_norm(x, final_norm_g)
    y_sample = y[:st.ts].reshape(dec_batch, dec_seq, d)
    y_prompt = y[st.ts:].reshape(batch, seq, d)
    return (y_prompt, y_sample, jnp.stack(new_k, axis=1), jnp.stack(new_v, axis=1))
```

```python
import cmath
import functools
import math

import jax
import jax.numpy as jnp
import numpy as np
from jax import lax
from jax.experimental import pallas as pl
from jax.experimental.pallas import tpu as pltpu

F32 = jnp.float32
BF16 = jnp.bfloat16

HEAD_DIM = 128
GRID_W = 64
ROPE_THETA = 10000.0
EPS = 1e-6
POS_FREQS = 16
HY_ORDER = 2
LANES = 128
MOD_ROWS = 8
VMEM_LIMIT_BYTES = 56 * 1024 * 1024


def _cparams(n_axes):
    return pltpu.CompilerParams(dimension_semantics=("arbitrary",) * n_axes,
                                vmem_limit_bytes=VMEM_LIMIT_BYTES)


def _pick(n, pref, mult):
    best = None
    d = mult
    while d <= min(n, pref):
        if n % d == 0:
            best = d
        d += mult
    assert best is not None, (n, pref, mult)
    return best


def _roundup(x, m):
    return (x + m - 1) // m * m


def _mod_kernel(c_ref, w_ref, b_ref, o_ref):
    c = c_ref[...]
    s = (c * jax.nn.sigmoid(c)).astype(BF16)
    o_ref[...] = jnp.dot(s, w_ref[...].astype(BF16), preferred_element_type=F32) + b_ref[...]


def _mod_all(cvecs, w_mod, b_mod):
    depth, d, n = w_mod.shape
    tn = _pick(n, 1024, LANES)
    return pl.pallas_call(
        _mod_kernel,
        grid=(depth, n // tn),
        in_specs=[pl.BlockSpec((MOD_ROWS, d), lambda l, j: (0, 0)),
                  pl.BlockSpec((None, d, tn), lambda l, j: (l, 0, j)),
                  pl.BlockSpec((None, 1, tn), lambda l, j: (l, 0, j))],
        out_specs=pl.BlockSpec((None, MOD_ROWS, tn), lambda l, j: (l, 0, j)),
        out_shape=jax.ShapeDtypeStruct((depth, MOD_ROWS, n), F32),
        compiler_params=_cparams(2),
        name="mod_all",
    )(cvecs, w_mod, b_mod.reshape(depth, 1, n))


def _rms(x, g):
    return x * lax.rsqrt(jnp.mean(x * x, axis=-1, keepdims=True) + EPS) * g


def _norm_mod_kernel(x_ref, g_ref, sc_ref, sh_ref, o_ref):
    y = _rms(x_ref[...], g_ref[...])
    o_ref[...] = (y * (1.0 + sc_ref[...]) + sh_ref[...]).astype(o_ref.dtype)


def _norm_kernel(x_ref, g_ref, o_ref):
    o_ref[...] = _rms(x_ref[...], g_ref[...]).astype(o_ref.dtype)


class _Stream:
    def __init__(self, dec_batch, dec_seq, batch, seq):
        self.dec_batch, self.dec_seq, self.batch, self.seq = dec_batch, dec_seq, batch, seq
        self.ts = dec_batch * dec_seq
        self.tc = batch * seq
        self.t = self.ts + self.tc

    def group(self, row):
        return jnp.minimum(row // self.dec_seq, self.dec_batch)

    def token_tile(self, pref):
        return _pick(math.gcd(self.dec_seq, self.tc), pref, 16)


def _mod_index(st, layer, which, tm):
    def index(i):
        return ((layer * MOD_ROWS + st.group(i * tm)) * 6 + which, 0, 0)
    return index


def _norm_mod(x, g, modr, st, layer, which_sc, which_sh):
    t, d = x.shape
    tm = st.token_tile(512)
    return pl.pallas_call(
        _norm_mod_kernel,
        grid=(t // tm,),
        in_specs=[pl.BlockSpec((tm, d), lambda i: (i, 0)),
                  pl.BlockSpec((None, 1, d), lambda i: (layer, 0, 0)),
                  pl.BlockSpec((None, 1, d), _mod_index(st, layer, which_sc, tm)),
                  pl.BlockSpec((None, 1, d), _mod_index(st, layer, which_sh, tm))],
        out_specs=pl.BlockSpec((tm, d), lambda i: (i, 0)),
        out_shape=jax.ShapeDtypeStruct((t, d), BF16),
        compiler_params=_cparams(1),
        name="norm_mod",
    )(x, g, modr, modr)


def _final_norm(x, g):
    t, d = x.shape
    tm = _pick(t, 512, 8)
    return pl.pallas_call(
        _norm_kernel,
        grid=(t // tm,),
        in_specs=[pl.BlockSpec((tm, d), lambda i: (i, 0)),
                  pl.BlockSpec((1, d), lambda i: (0, 0))],
        out_specs=pl.BlockSpec((tm, d), lambda i: (i, 0)),
        out_shape=jax.ShapeDtypeStruct((t, d), F32),
        compiler_params=_cparams(1),
        name="final_norm",
    )(x, g.reshape(1, d))


def _gmm_kernel(wid_ref, nv_ref, *refs, mode):
    i = pl.program_id(1)
    if mode == "swiglu":
        a_ref, wg_ref, wu_ref, o_ref, wgb, wub = refs
        wpairs = ((wg_ref, wgb), (wu_ref, wub))
    elif mode == "resid":
        a_ref, w_ref, x_ref, gate_ref, o_ref, wb = refs
        wpairs = ((w_ref, wb),)
    else:
        a_ref, w_ref, o_ref, wb = refs
        wpairs = ((w_ref, wb),)

    prev = wid_ref[jnp.maximum(i - 1, 0)]
    new_w = jnp.logical_or(i == 0, wid_ref[i] != prev)

    @pl.when(jnp.logical_and(new_w, i < nv_ref[0]))
    def _cast():
        for src, dst in wpairs:
            dst[...] = src[...].astype(BF16)

    @pl.when(i >= nv_ref[0])
    def _unused_tile():
        o_ref[...] = jnp.zeros_like(o_ref)

    @pl.when(i < nv_ref[0])
    def _compute():
        a = a_ref[...]
        if mode == "swiglu":
            g = jnp.dot(a, wgb[...], preferred_element_type=F32)
            u = jnp.dot(a, wub[...], preferred_element_type=F32)
            o_ref[...] = (g * jax.nn.sigmoid(g) * u).astype(o_ref.dtype)
        elif mode == "resid":
            acc = jnp.dot(a, wb[...], preferred_element_type=F32)
            o_ref[...] = x_ref[...] + gate_ref[...] * acc
        else:
            o_ref[...] = jnp.dot(a, wb[...], preferred_element_type=F32).astype(o_ref.dtype)


def _gmm(a, ws, wid, nvalid, *, mode, out_dtype, tm, tn, resid=None, name="gmm"):
    m = a.shape[0]
    _, k, n = ws[0].shape
    ka = a.shape[1]
    nt = m // tm
    if isinstance(wid, int):
        wid, nvalid = jnp.full((nt,), wid, jnp.int32), jnp.full((1,), nt, jnp.int32)
    assert m % tm == 0 and n % tn == 0 and wid.shape == (nt,)

    def a_index(j, i, wid_ref, nv_ref):
        return (jnp.minimum(i, nv_ref[0] - 1), 0)

    def w_index(j, i, wid_ref, nv_ref):
        return (wid_ref[i], 0, j)

    def o_index(j, i, wid_ref, nv_ref):
        return (i, j)

    in_specs = [pl.BlockSpec((tm, ka), a_index)]
    in_specs += [pl.BlockSpec((None, k, tn), w_index) for _ in ws]
    args = [a, *ws]
    if mode == "resid":
        x, modr, mod_index = resid
        in_specs += [pl.BlockSpec((tm, tn), o_index),
                     pl.BlockSpec((None, 1, tn),
                                  lambda j, i, wid_ref, nv_ref: mod_index(i)[:2] + (j,))]
        args += [x, modr]
    grid_spec = pltpu.PrefetchScalarGridSpec(
        num_scalar_prefetch=2,
        grid=(n // tn, nt),
        in_specs=in_specs,
        out_specs=pl.BlockSpec((tm, tn), o_index),
        scratch_shapes=[pltpu.VMEM((k, tn), BF16) for _ in ws],
    )
    return pl.pallas_call(
        functools.partial(_gmm_kernel, mode=mode),
        grid_spec=grid_spec,
        out_shape=jax.ShapeDtypeStruct((m, n), out_dtype),
        compiler_params=_cparams(2),
        name=name,
    )(wid, nvalid, *args)


def _qk_kernel(q_ref, k_ref, qg_ref, kg_ref, cos_ref, sin_ref, qo_ref, ko_ref, kc_ref, *, scale):
    cos = cos_ref[...]
    sin = sin_ref[...]

    def head(x, g):
        xn = _rms(x.astype(F32), g)
        return xn, xn * cos + pltpu.roll(xn, HEAD_DIM // 2, 1) * sin

    for h in range(q_ref.shape[1] // HEAD_DIM):
        sl = slice(h * HEAD_DIM, (h + 1) * HEAD_DIM)
        _, roped = head(q_ref[:, sl], qg_ref[...])
        qo_ref[:, sl] = (roped * scale).astype(qo_ref.dtype)
    for h in range(k_ref.shape[1] // HEAD_DIM):
        sl = slice(h * HEAD_DIM, (h + 1) * HEAD_DIM)
        xn, roped = head(k_ref[:, sl], kg_ref[...])
        ko_ref[:, sl] = roped.astype(ko_ref.dtype)
        kc_ref[:, sl] = xn


def _qk_norm_rope(proj, qg, kg, cos2, sin2, layer, a_w, kv_w, st):
    t = proj.shape[0]
    tm = st.token_tile(512)
    assert a_w % kv_w == 0
    return pl.pallas_call(
        functools.partial(_qk_kernel, scale=1.0 / math.sqrt(HEAD_DIM)),
        grid=(t // tm,),
        in_specs=[pl.BlockSpec((tm, a_w), lambda i: (i, 0)),
                  pl.BlockSpec((tm, kv_w), lambda i: (i, a_w // kv_w)),
                  pl.BlockSpec((None, 1, HEAD_DIM), lambda i: (layer, 0, 0)),
                  pl.BlockSpec((None, 1, HEAD_DIM), lambda i: (layer, 0, 0)),
                  pl.BlockSpec((tm, HEAD_DIM), lambda i: (i, 0)),
                  pl.BlockSpec((tm, HEAD_DIM), lambda i: (i, 0))],
        out_specs=[pl.BlockSpec((tm, a_w), lambda i: (i, 0)),
                   pl.BlockSpec((tm, kv_w), lambda i: (i, 0)),
                   pl.BlockSpec((tm, kv_w), lambda i: (i, 0))],
        out_shape=[jax.ShapeDtypeStruct((t, a_w), BF16),
                   jax.ShapeDtypeStruct((t, kv_w), BF16),
                   jax.ShapeDtypeStruct((t, kv_w), F32)],
        compiler_params=_cparams(1),
        name="qk_norm_rope",
    )(proj, proj, qg, kg, cos2, sin2)


def _rope_tables(st):
    pos = np.arange(st.dec_seq)
    inv = ROPE_THETA ** (-np.arange(0, HEAD_DIM // 2, 2, dtype=np.float64) / (HEAD_DIM // 2))
    ang = np.concatenate([(pos // GRID_W)[:, None] * inv, (pos % GRID_W)[:, None] * inv], axis=-1)
    cos2 = np.concatenate([np.cos(ang), np.cos(ang)], axis=-1)
    sin2 = np.concatenate([-np.sin(ang), np.sin(ang)], axis=-1)
    cos_all = np.concatenate([np.tile(cos2, (st.dec_batch, 1)), np.ones((st.tc, HEAD_DIM))])
    sin_all = np.concatenate([np.tile(sin2, (st.dec_batch, 1)), np.zeros((st.tc, HEAD_DIM))])
    return jnp.asarray(cos_all, F32), jnp.asarray(sin_all, F32)


KEY_CHUNK = 512


def _attn_kernel(*refs, g, has_cache):
    if has_cache:
        q_ref, ck_ref, cv_ref, k_ref, v_ref, o_ref = refs
    else:
        q_ref, k_ref, v_ref, o_ref = refs
    tq = q_ref.shape[0]
    q = jnp.concatenate([q_ref[:, h * HEAD_DIM:(h + 1) * HEAD_DIM] for h in range(g)], axis=0)
    lk = k_ref.shape[0]
    parts = []
    if has_cache:
        parts.append((ck_ref, cv_ref, 0, ck_ref.shape[0]))
    c0 = 0
    while c0 < lk:
        n = min(KEY_CHUNK, lk - c0)
        parts.append((k_ref, v_ref, c0, n))
        c0 += n
    m = acc = None
    for kr, vr, c0, n in parts:
        kc = kr[c0:c0 + n, :].astype(BF16)
        va = jnp.concatenate([vr[c0:c0 + n, :].astype(BF16), jnp.ones((n, HEAD_DIM), BF16)], axis=1)
        s = lax.dot_general(q, kc, (((1,), (1,)), ((), ())), preferred_element_type=F32)
        mc = jnp.max(s, axis=-1, keepdims=True)
        if m is None:
            m = mc
            acc = jnp.dot(jnp.exp(s - m).astype(BF16), va, preferred_element_type=F32)
        else:
            m_new = jnp.maximum(m, mc)
            acc = (jnp.exp(m - m_new) * acc
                   + jnp.dot(jnp.exp(s - m_new).astype(BF16), va, preferred_element_type=F32))
            m = m_new
    o = acc[:, :HEAD_DIM] / acc[:, HEAD_DIM:]
    for h in range(g):
        o_ref[:, h * HEAD_DIM:(h + 1) * HEAD_DIM] = o[h * tq:(h + 1) * tq].astype(o_ref.dtype)


def _attention(qn, kn, proj, v_col0, n_kv, nseq, seq_len, row0, cache=None):
    aw = qn.shape[1]
    g = aw // HEAD_DIM // n_kv
    tq = _pick(seq_len, 256, 16)
    nq = seq_len // tq
    assert row0 % seq_len == 0 and v_col0 % HEAD_DIM == 0
    in_specs = [pl.BlockSpec((tq, g * HEAD_DIM), lambda b, kv, i: (row0 // tq + b * nq + i, kv))]
    args = [qn]
    if cache is not None:
        ck, cv, layer = cache
        past = ck.shape[2]
        cspec = pl.BlockSpec((None, None, past, HEAD_DIM), lambda b, kv, i: (b, layer, 0, kv))
        in_specs += [cspec, cspec]
        args += [ck, cv]
    in_specs += [pl.BlockSpec((seq_len, HEAD_DIM), lambda b, kv, i: (row0 // seq_len + b, kv)),
                 pl.BlockSpec((seq_len, HEAD_DIM),
                              lambda b, kv, i: (row0 // seq_len + b, v_col0 // HEAD_DIM + kv))]
    args += [kn, proj]
    return pl.pallas_call(
        functools.partial(_attn_kernel, g=g, has_cache=cache is not None),
        grid=(nseq, n_kv, nq),
        in_specs=in_specs,
        out_specs=pl.BlockSpec((tq, g * HEAD_DIM), lambda b, kv, i: (b * nq + i, kv)),
        out_shape=jax.ShapeDtypeStruct((nseq * seq_len, aw), BF16),
        compiler_params=_cparams(3),
        name="attention",
    )(*args)


def _filter_kernel(feat_ref, w1_ref, b1_ref, w2_ref, b2_ref, w3_ref, ld_ref, o_ref, h_ref, *,
                   seq_len, hy_w):
    hp = lax.Precision.HIGHEST

    @pl.when(pl.program_id(1) == 0)
    def _hidden():
        h = jnp.sin(jnp.dot(feat_ref[...], w1_ref[...], precision=hp, preferred_element_type=F32)
                    + b1_ref[...])
        h_ref[...] = jnp.sin(jnp.dot(h, w2_ref[...], precision=hp, preferred_element_type=F32)
                             + b2_ref[...])

    h = jnp.dot(h_ref[...], w3_ref[...], precision=hp, preferred_element_type=F32)
    tl, tn = h.shape
    row = lax.broadcasted_iota(jnp.int32, h.shape, 0) + pl.program_id(0) * tl
    t = row.astype(F32) / seq_len
    h = h * jnp.exp(-jnp.exp(ld_ref[...]) * t)
    col = lax.broadcasted_iota(jnp.int32, h.shape, 1) + pl.program_id(1) * tn
    backward = (col // hy_w) % 2 == 1
    o_ref[...] = jnp.where(jnp.logical_and(backward, row == 0), 0.0, h).astype(o_ref.dtype)


def _filter_features(seq_len):
    t = np.arange(seq_len, dtype=np.float32) / np.float32(seq_len)
    fr = np.arange(1, POS_FREQS + 1, dtype=np.float32)
    ang = float(np.float32(2.0 * math.pi)) * t[:, None].astype(np.float64) * fr
    feat = np.concatenate([t[:, None], np.sin(ang), np.cos(ang)], axis=-1)
    out = np.zeros((seq_len, LANES), np.float32)
    out[:, :feat.shape[1]] = feat
    return jnp.asarray(out)


def _pad2(x, rows, cols):
    return jnp.pad(x, ((0, rows - x.shape[0]), (0, cols - x.shape[1])))


def _hyena_filters(seq_len, layer, p, hy_w):
    n = p["filt_w3"].shape[-1]
    feat = _filter_features(seq_len)
    w1 = _pad2(p["filt_w1"][layer], LANES, LANES)
    b1 = _pad2(p["filt_b1"][layer][None, :], 1, LANES)
    w2 = _pad2(p["filt_w2"][layer], LANES, LANES)
    b2 = _pad2(p["filt_b2"][layer][None, :], 1, LANES)
    w3 = _pad2(p["filt_w3"][layer], LANES, n)
    ld = p["filt_log_decay"][layer][None, :]
    tl = _pick(seq_len, 1024, 16)
    tn = _pick(n, 2048, LANES)
    return pl.pallas_call(
        functools.partial(_filter_kernel, seq_len=seq_len, hy_w=hy_w),
        grid=(seq_len // tl, n // tn),
        in_specs=[pl.BlockSpec((tl, LANES), lambda i, j: (i, 0)),
                  pl.BlockSpec((LANES, LANES), lambda i, j: (0, 0)),
                  pl.BlockSpec((1, LANES), lambda i, j: (0, 0)),
                  pl.BlockSpec((LANES, LANES), lambda i, j: (0, 0)),
                  pl.BlockSpec((1, LANES), lambda i, j: (0, 0)),
                  pl.BlockSpec((LANES, tn), lambda i, j: (0, j)),
                  pl.BlockSpec((1, tn), lambda i, j: (0, j))],
        out_specs=pl.BlockSpec((tl, tn), lambda i, j: (i, j)),
        out_shape=jax.ShapeDtypeStruct((seq_len, n), BF16),
        scratch_shapes=[pltpu.VMEM((tl, LANES), F32)],
        compiler_params=_cparams(2),
        name="hy_filter",
    )(feat, w1, b1, w2, b2, w3, ld)


class _Dft:
    def __init__(self, seq_len, radix):
        self.seq_len, self.r = seq_len, radix
        n = 2 * seq_len
        self.n1 = n // radix
        self.la = seq_len // radix
        k1 = self.n1 // 2 + 1
        self.k1p = _roundup(k1, 16)
        a = np.arange(self.la)[None, :]
        kk = np.arange(self.k1p)[:, None]
        valid = kk < k1
        th = 2.0 * math.pi * kk * a / self.n1
        fwd = np.concatenate([np.where(valid, np.cos(th), 0.0), np.where(valid, -np.sin(th), 0.0)])
        wgt = np.where((kk == 0) | (kk == self.n1 // 2), 1.0, 2.0) * valid / n
        inv = np.concatenate([wgt * np.cos(th), -wgt * np.sin(th)]).T
        self.fwd = jnp.asarray(fwd, F32)
        self.inv = jnp.asarray(inv, F32)
        ph = 2.0 * math.pi * kk / n
        ones = np.ones((1, LANES))
        self.tw_re = jnp.asarray(np.cos(ph) * ones, F32)
        self.tw_im = jnp.asarray(-np.sin(ph) * ones, F32)


def _cmul_const(x, w):
    wr, wi = w.real, w.imag
    if abs(wi) < 1e-12:
        if abs(wr - 1.0) < 1e-12:
            return x
        if abs(wr + 1.0) < 1e-12:
            return (-x[0], -x[1])
    if abs(wr) < 1e-12:
        if abs(wi - 1.0) < 1e-12:
            return (-x[1], x[0])
        if abs(wi + 1.0) < 1e-12:
            return (x[1], -x[0])
    return (x[0] * wr - x[1] * wi, x[0] * wi + x[1] * wr)


def _fft_list(xs, sign):
    n = len(xs)
    if n == 1:
        return xs
    ev = _fft_list(xs[0::2], sign)
    od = _fft_list(xs[1::2], sign)
    out = [None] * n
    for k in range(n // 2):
        t = _cmul_const(od[k], cmath.exp(sign * 2j * math.pi * k / n))
        out[k] = (ev[k][0] + t[0], ev[k][1] + t[1])
        out[k + n // 2] = (ev[k][0] - t[0], ev[k][1] - t[1])
    return out


ROW_CHUNK = 16


def _lane_tile(x, width):
    return x if width == LANES else jnp.concatenate([x] * (width // LANES), axis=1)


def _twiddles(tw_re_ref, tw_im_ref, rows, r, tc):
    if r == 1:
        return []
    w1 = (tw_re_ref[rows, :], tw_im_ref[rows, :])
    ws = [w1]
    for _ in range(r - 2):
        pr, pi = ws[-1]
        ws.append((pr * w1[0] - pi * w1[1], pr * w1[1] + pi * w1[0]))
    return [(_lane_tile(wr, tc), _lane_tile(wi, tc)) for wr, wi in ws]


def _spectrum_chunk(y_ref, tws, c0, k1p, r):
    rows = pl.ds(c0, ROW_CHUNK)
    rows_im = pl.ds(k1p + c0, ROW_CHUNK)
    ts = []
    for b in range(r):
        yr = y_ref[b, rows, :]
        yi = y_ref[b, rows_im, :]
        if b == 0:
            ts.append((yr, yi))
        else:
            wr, wi = tws[b - 1]
            ts.append((yr * wr - yi * wi, yr * wi + yi * wr))
    return _fft_list(ts, -1)


def _lanes(q):
    return slice(q * LANES, (q + 1) * LANES)


def _conv3_group(u_ref, w_ref, b_ref, q):
    x = u_ref[:, _lanes(q)].astype(F32)
    n = x.shape[0]
    row = lax.broadcasted_iota(jnp.int32, x.shape, 0)
    prev = jnp.where(row == 0, 0.0, pltpu.roll(x, 1, 0))
    nxt = jnp.where(row == n - 1, 0.0, pltpu.roll(x, n - 1, 0))
    w = w_ref[:, _lanes(q)]
    return prev * w[0:1] + x * w[1:2] + nxt * w[2:3] + b_ref[:, _lanes(q)]


def _stage1(f_ref, load_group, xf_ref, y_ref, r, la):
    nq = xf_ref.shape[0]
    for q in range(nq):
        xf_ref[q] = load_group(q)
    f = f_ref[...].astype(BF16)
    for b in range(r):
        xb = jnp.concatenate([xf_ref[q, pl.ds(b, la, stride=r), :] for q in range(nq)], axis=1)
        y_ref[b] = jnp.dot(f, xb.astype(BF16), preferred_element_type=F32)


def _spectrum_kernel(*refs, r, k1p, la):
    if r > 1:
        f_ref, tw_re_ref, tw_im_ref, hf_ref, hb_ref, kr_ref, ki_ref, xf_ref, y_ref, acc_ref = refs
    else:
        f_ref, hf_ref, hb_ref, kr_ref, ki_ref, xf_ref, y_ref, acc_ref = refs
        tw_re_ref = tw_im_ref = None
    tc = y_ref.shape[-1]

    def sweep(src_ref, backward):
        _stage1(f_ref, lambda q: src_ref[:, _lanes(q)].astype(F32), xf_ref, y_ref, r, la)

        def chunk(c, carry):
            c0 = pl.multiple_of(c * ROW_CHUNK, ROW_CHUNK)
            rows = pl.ds(c0, ROW_CHUNK)
            rows_im = pl.ds(k1p + c0, ROW_CHUNK)
            x = _spectrum_chunk(y_ref, _twiddles(tw_re_ref, tw_im_ref, rows, r, tc), c0, k1p, r)
            for k2 in range(r):
                if backward:
                    kr_ref[k2, rows, :] = (acc_ref[k2, rows, :] + x[k2][0]).astype(kr_ref.dtype)
                    ki_ref[k2, rows, :] = (acc_ref[k2, rows_im, :] - x[k2][1]).astype(ki_ref.dtype)
                else:
                    acc_ref[k2, rows, :] = x[k2][0]
                    acc_ref[k2, rows_im, :] = x[k2][1]
            return carry

        lax.fori_loop(0, k1p // ROW_CHUNK, chunk, 0)

    sweep(hf_ref, False)
    sweep(hb_ref, True)


def _filter_spectrum(taps, dft, hy_w):
    r, la, k1p, seq_len = dft.r, dft.la, dft.k1p, dft.seq_len
    tc = _pick(hy_w, 256, LANES)
    nct = hy_w // tc

    def tap_spec(direction):
        return pl.BlockSpec((seq_len, tc), lambda o, j: (0, (o * 2 + direction) * nct + j))

    const = [pl.BlockSpec(dft.fwd.shape, lambda o, j: (0, 0))]
    cargs = [dft.fwd]
    if r > 1:
        const += [pl.BlockSpec(dft.tw_re.shape, lambda o, j: (0, 0))] * 2
        cargs += [dft.tw_re, dft.tw_im]
    out_spec = pl.BlockSpec((None, r, k1p, tc), lambda o, j: (o, 0, 0, j))
    shape = jax.ShapeDtypeStruct((HY_ORDER, r, k1p, hy_w), BF16)
    return pl.pallas_call(
        functools.partial(_spectrum_kernel, r=r, k1p=k1p, la=la),
        grid=(HY_ORDER, nct),
        in_specs=const + [tap_spec(0), tap_spec(1)],
        out_specs=[out_spec, out_spec],
        out_shape=[shape, shape],
        scratch_shapes=[pltpu.VMEM((tc // LANES, seq_len, LANES), F32),
                        pltpu.VMEM((r, 2 * k1p, tc), F32),
                        pltpu.VMEM((r, 2 * k1p, tc), F32)],
        compiler_params=_cparams(2),
        name="hy_filter_spectrum",
    )(*cargs, taps, taps)


def _longconv_kernel(*refs, r, k1p, la, z_conv):
    refs = list(refs)
    f_ref, fi_ref = refs[:2]
    del refs[:2]
    tw_re_ref = tw_im_ref = None
    if r > 1:
        tw_re_ref, tw_im_ref = refs[:2]
        del refs[:2]
    z_ref = refs.pop(0)
    if z_conv:
        zw_ref, zb_ref = refs[:2]
        del refs[:2]
    gate_ref, gw_ref, gb_ref, skip_ref, kr_ref, ki_ref, o_ref, xf_ref, y_ref, g_ref = refs
    tc = y_ref.shape[-1]
    if z_conv:
        load_z = lambda q: _conv3_group(z_ref, zw_ref, zb_ref, q)
    else:
        load_z = lambda q: z_ref[:, _lanes(q)].astype(F32)
    _stage1(f_ref, load_z, xf_ref, y_ref, r, la)

    def chunk(c, carry):
        c0 = pl.multiple_of(c * ROW_CHUNK, ROW_CHUNK)
        rows = pl.ds(c0, ROW_CHUNK)
        rows_im = pl.ds(k1p + c0, ROW_CHUNK)
        tws = _twiddles(tw_re_ref, tw_im_ref, rows, r, tc)
        x = _spectrum_chunk(y_ref, tws, c0, k1p, r)
        prod = []
        for k2 in range(r):
            kr = kr_ref[k2, rows, :].astype(F32)
            ki = ki_ref[k2, rows, :].astype(F32)
            prod.append((x[k2][0] * kr - x[k2][1] * ki, x[k2][0] * ki + x[k2][1] * kr))
        s = _fft_list(prod, +1)
        for b in range(r):
            if b == 0:
                gr, gi = s[0]
            else:
                wr, wi = tws[b - 1]
                gr = s[b][0] * wr + s[b][1] * wi
                gi = s[b][1] * wr - s[b][0] * wi
            g_ref[b, rows, :] = gr.astype(g_ref.dtype)
            g_ref[b, rows_im, :] = gi.astype(g_ref.dtype)
        return carry

    lax.fori_loop(0, k1p // ROW_CHUNK, chunk, 0)
    fi = fi_ref[...].astype(BF16)
    for b in range(r):
        yb = jnp.dot(fi, g_ref[b], preferred_element_type=F32)
        rows = pl.ds(b, la, stride=r)
        for q in range(xf_ref.shape[0]):
            xf_ref[q, rows, :] = yb[:, _lanes(q)] + skip_ref[:, _lanes(q)] * xf_ref[q, rows, :]
    for q in range(xf_ref.shape[0]):
        gate = _conv3_group(gate_ref, gw_ref, gb_ref, q)
        o_ref[:, _lanes(q)] = (gate * xf_ref[q]).astype(o_ref.dtype)


def _longconv(proj, u_col0, row0, layer, conv_w, conv_b, z_prev, order, skip, kf_re, kf_im, dft,
              nseq, hy_w):
    r, la, k1p, seq_len = dft.r, dft.la, dft.k1p, dft.seq_len
    tc = _pick(math.gcd(hy_w, u_col0), 256 if r > 1 else 1024, LANES)
    assert row0 % seq_len == 0
    nct = hy_w // tc
    seq0 = row0 // seq_len
    z_conv = z_prev is None

    def u_specs(part):
        return [pl.BlockSpec((seq_len, tc), lambda s, j: (seq0 + s, u_col0 // tc + part * nct + j)),
                pl.BlockSpec((None, 3, tc), lambda s, j: (layer, 0, part * nct + j)),
                pl.BlockSpec((None, 1, tc), lambda s, j: (layer, 0, part * nct + j))]

    cdim = lambda s, j: (0, 0)
    in_specs = [pl.BlockSpec(dft.fwd.shape, cdim), pl.BlockSpec(dft.inv.shape, cdim)]
    args = [dft.fwd, dft.inv]
    if r > 1:
        in_specs += [pl.BlockSpec(dft.tw_re.shape, cdim)] * 2
        args += [dft.tw_re, dft.tw_im]
    if z_conv:
        in_specs += u_specs(0)
        args += [proj, conv_w, conv_b]
    else:
        in_specs.append(pl.BlockSpec((seq_len, tc), lambda s, j: (s, j)))
        args.append(z_prev)
    in_specs += u_specs(1 + order)
    args += [proj, conv_w, conv_b]
    in_specs += [pl.BlockSpec((None, 1, tc), lambda s, j: (layer * HY_ORDER + order, 0, j)),
                 pl.BlockSpec((None, r, k1p, tc), lambda s, j: (order, 0, 0, j)),
                 pl.BlockSpec((None, r, k1p, tc), lambda s, j: (order, 0, 0, j))]
    args += [skip, kf_re, kf_im]
    return pl.pallas_call(
        functools.partial(_longconv_kernel, r=r, k1p=k1p, la=la, z_conv=z_conv),
        grid=(nseq, nct),
        in_specs=in_specs,
        out_specs=pl.BlockSpec((seq_len, tc), lambda s, j: (s, j)),
        out_shape=jax.ShapeDtypeStruct((nseq * seq_len, hy_w), BF16),
        scratch_shapes=[pltpu.VMEM((tc // LANES, seq_len, LANES), F32),
                        pltpu.VMEM((r, 2 * k1p, tc), F32),
                        pltpu.VMEM((r, 2 * k1p, tc), BF16)],
        compiler_params=_cparams(2),
        name="hy_longconv",
    )(*args)


def _hyena(proj, layer, p, st, u_col0, hy_w, dfts):
    skip = p["hy_skip"].reshape(-1, 1, hy_w)
    conv_w = p["hy_conv_w"]
    conv_b = p["hy_conv_b"].reshape(-1, 1, 3 * hy_w)
    outs = []
    for nseq, seq_len, row0 in ((st.dec_batch, st.dec_seq, 0), (st.batch, st.seq, st.ts)):
        dft = dfts[seq_len]
        taps = _hyena_filters(seq_len, layer, p, hy_w)
        kf_re, kf_im = _filter_spectrum(taps, dft, hy_w)
        z = None
        for order in range(HY_ORDER):
            z = _longconv(proj, u_col0, row0, layer, conv_w, conv_b, z, order, skip, kf_re, kf_im,
                          dft, nseq, hy_w)
        outs.append(z)
    return outs


def _merge_kernel(as_ref, ac_ref, hs_ref, hc_ref, ga_ref, gh_ref, o_ref, *, n_sample_tiles):
    aw = as_ref.shape[1]
    sample = pl.program_id(0) < n_sample_tiles
    a = jnp.where(sample, as_ref[...], ac_ref[...]).astype(F32)
    h = jnp.where(sample, hs_ref[...], hc_ref[...]).astype(F32)
    o_ref[:, :aw] = _rms(a, ga_ref[...]).astype(o_ref.dtype)
    o_ref[:, aw:] = _rms(h, gh_ref[...]).astype(o_ref.dtype)


def _merge_norm(attn_s, attn_c, hy_s, hy_c, g, layer, st):
    aw, hw = attn_s.shape[1], hy_s.shape[1]
    assert aw == hw
    tm = st.token_tile(512)
    ns = st.ts // tm
    g3 = g.reshape(g.shape[0], 1, aw + hw)
    s_idx = lambda i: (jnp.minimum(i, ns - 1), 0)
    c_idx = lambda i: (jnp.maximum(i - ns, 0), 0)
    return pl.pallas_call(
        functools.partial(_merge_kernel, n_sample_tiles=ns),
        grid=(st.t // tm,),
        in_specs=[pl.BlockSpec((tm, aw), s_idx), pl.BlockSpec((tm, aw), c_idx),
                  pl.BlockSpec((tm, hw), s_idx), pl.BlockSpec((tm, hw), c_idx),
                  pl.BlockSpec((None, 1, aw), lambda i: (layer, 0, 0)),
                  pl.BlockSpec((None, 1, hw), lambda i: (layer, 0, 1))],
        out_specs=pl.BlockSpec((tm, aw + hw), lambda i: (i, 0)),
        out_shape=jax.ShapeDtypeStruct((st.t, aw + hw), BF16),
        compiler_params=_cparams(1),
        name="merge_norm",
    )(attn_s, attn_c, hy_s, hy_c, g3, g3)


def _route_kernel(x_ref, g_ref, sc_ref, sh_ref, rw_ref, hp_ref, w1_ref, w2_ref, code_ref,
                  rank_ref, cnt_ref, carry_ref, *, n_experts):
    i = pl.program_id(0)

    @pl.when(i == 0)
    def _init():
        carry_ref[...] = jnp.zeros_like(carry_ref)

    h = _rms(x_ref[...], g_ref[...]) * (1.0 + sc_ref[...]) + sh_ref[...]
    hp_ref[...] = h

    logits = jnp.dot(h, rw_ref[...], precision=lax.Precision.HIGHEST, preferred_element_type=F32)
    lane = lax.broadcasted_iota(jnp.int32, logits.shape, 1)
    neg = jnp.float32(-jnp.inf)
    logits = jnp.where(lane < n_experts, logits, neg)
    m1 = jnp.max(logits, axis=-1, keepdims=True)
    i1 = jnp.min(jnp.where(logits == m1, lane, LANES), axis=-1, keepdims=True)
    rest = jnp.where(lane == i1, neg, logits)
    m2 = jnp.max(rest, axis=-1, keepdims=True)
    i2 = jnp.min(jnp.where(rest == m2, lane, LANES), axis=-1, keepdims=True)
    e2 = jnp.exp(m2 - m1)
    w1 = 1.0 / (1.0 + e2)
    w2 = e2 / (1.0 + e2)
    w1_ref[...] = jnp.broadcast_to(w1, w1_ref.shape)
    w2_ref[...] = jnp.broadcast_to(w2, w2_ref.shape)

    code = jnp.where(lane == i1, 1.0, jnp.where(lane == i2, 2.0, 0.0))
    code_t = code.T[:code_ref.shape[0], :]
    code_ref[...] = code_t.astype(jnp.int32)
    sel = (code_t > 0.0).astype(BF16)
    tm = sel.shape[1]
    upper = (lax.broadcasted_iota(jnp.int32, (tm, tm), 0)
             < lax.broadcasted_iota(jnp.int32, (tm, tm), 1)).astype(BF16)
    excl = jnp.dot(sel, upper, preferred_element_type=F32)
    carry = carry_ref[...]
    rank_ref[...] = (excl + carry[:, :1]).astype(jnp.int32)
    carry = carry + jnp.sum(sel.astype(F32), axis=-1, keepdims=True)
    carry_ref[...] = carry
    cnt_ref[...] = carry.astype(jnp.int32)


def _route(x, g, modr, rw, st, layer, n_experts):
    t, d = x.shape
    tm = st.token_tile(512)
    assert tm % LANES == 0 and n_experts <= 8
    rwp = _pad2(rw, d, LANES)
    row_spec = pl.BlockSpec((tm, LANES), lambda i: (i, 0))
    col_spec = pl.BlockSpec((8, tm), lambda i: (0, i))
    return pl.pallas_call(
        functools.partial(_route_kernel, n_experts=n_experts),
        grid=(t // tm,),
        in_specs=[pl.BlockSpec((tm, d), lambda i: (i, 0)),
                  pl.BlockSpec((None, 1, d), lambda i: (layer, 0, 0)),
                  pl.BlockSpec((None, 1, d), _mod_index(st, layer, 4, tm)),
                  pl.BlockSpec((None, 1, d), _mod_index(st, layer, 3, tm)),
                  pl.BlockSpec((d, LANES), lambda i: (0, 0))],
        out_specs=[pl.BlockSpec((tm, d), lambda i: (i, 0)), row_spec, row_spec,
                   col_spec, col_spec, pl.BlockSpec((8, LANES), lambda i: (0, 0))],
        out_shape=[jax.ShapeDtypeStruct((t, d), F32),
                   jax.ShapeDtypeStruct((t, LANES), F32),
                   jax.ShapeDtypeStruct((t, LANES), F32),
                   jax.ShapeDtypeStruct((8, t), jnp.int32),
                   jax.ShapeDtypeStruct((8, t), jnp.int32),
                   jax.ShapeDtypeStruct((8, LANES), jnp.int32)],
        scratch_shapes=[pltpu.VMEM((8, LANES), F32)],
        compiler_params=_cparams(1),
        name="moe_route",
    )(x, g, modr, modr, rwp)


DISPATCH_ROWS = 256


GATHER_UNROLL = 8


def _issue_row_gather(idx_ref, src_ref, dst_ref, sem):
    def trip(g, carry):
        for u in range(GATHER_UNROLL):
            rr = g * GATHER_UNROLL + u
            pltpu.make_async_copy(src_ref.at[pl.ds(idx_ref[0, rr], 1)], dst_ref.at[pl.ds(rr, 1)],
                                  sem).start(priority=u % 2)
        return carry

    lax.fori_loop(0, dst_ref.shape[0] // GATHER_UNROLL, trip, 0)


def _wait_row_gather(src_ref, dst_ref, sem):
    pltpu.make_async_copy(src_ref.at[pl.ds(0, dst_ref.shape[0])], dst_ref, sem).wait()


def _pipelined_gather(idx_refs, next_idx_refs, src_ref, bufs, sems):
    i = pl.program_id(0)
    cur = i % 2

    @pl.when(i == 0)
    def _first():
        for idx_ref, buf in zip(idx_refs, bufs):
            _issue_row_gather(idx_ref, src_ref, buf.at[0], sems.at[0])

    @pl.when(i + 1 < pl.num_programs(0))
    def _prefetch():
        for idx_ref, buf in zip(next_idx_refs, bufs):
            _issue_row_gather(idx_ref, src_ref, buf.at[1 - cur], sems.at[1 - cur])

    for buf in bufs:
        _wait_row_gather(src_ref, buf.at[cur], sems.at[cur])
    return cur


def _dispatch_kernel(tok_ref, tok_next_ref, h_ref, xs_ref, rows_ref, sems):
    cur = _pipelined_gather([tok_ref], [tok_next_ref], h_ref, [rows_ref], sems)
    xs_ref[...] = rows_ref[cur].astype(xs_ref.dtype)


def _index_specs(nt, td):
    return [pl.BlockSpec((None, 1, td), lambda i: (i, 0, 0), memory_space=pltpu.SMEM),
            pl.BlockSpec((None, 1, td), lambda i: (jnp.minimum(i + 1, nt - 1), 0, 0),
                         memory_space=pltpu.SMEM)]


def _dispatch(h, slot_token):
    n_slots = slot_token.shape[0]
    d = h.shape[1]
    td = _pick(n_slots, DISPATCH_ROWS, LANES)
    nt = n_slots // td
    tok = slot_token.reshape(nt, 1, td)
    return pl.pallas_call(
        _dispatch_kernel,
        grid=(nt,),
        in_specs=_index_specs(nt, td) + [pl.BlockSpec(memory_space=pl.ANY)],
        out_specs=pl.BlockSpec((td, d), lambda i: (i, 0)),
        out_shape=jax.ShapeDtypeStruct((n_slots, d), BF16),
        scratch_shapes=[pltpu.VMEM((2, td, d), F32), pltpu.SemaphoreType.DMA((2,))],
        compiler_params=_cparams(1),
        name="moe_dispatch",
    )(tok, tok, h)


def _combine_kernel(*refs, post):
    (s0_ref, s0n_ref, s1_ref, s1n_ref, ys_ref, x_ref, gate_ref, w1_ref, w2_ref) = refs[:9]
    rest = refs[9:]
    n_post = {"none": 0, "norm_mod": 3, "final": 1}[post]
    post_refs, rest = rest[:n_post], rest[n_post:]
    n_out = 1 if post in ("none", "final") else 2
    outs, (ya, yb, sems) = rest[:n_out], rest[n_out:]
    cur = _pipelined_gather([s0_ref, s1_ref], [s0n_ref, s1n_ref], ys_ref, [ya, yb], sems)
    w1 = _lane_tile(w1_ref[...], ya.shape[-1])
    w2 = _lane_tile(w2_ref[...], ya.shape[-1])
    xn = x_ref[...] + gate_ref[...] * (w1 * ya[cur] + w2 * yb[cur])
    if post == "final":
        outs[0][...] = _rms(xn, post_refs[0][...])
    else:
        outs[0][...] = xn
    if post == "norm_mod":
        g_ref, sc_ref, sh_ref = post_refs
        outs[1][...] = (_rms(xn, g_ref[...]) * (1.0 + sc_ref[...]) + sh_ref[...]).astype(outs[1].dtype)


def _combine(ys, slot0, slot1, x, modr, w1, w2, st, layer, post, post_args=()):
    t, d = x.shape
    td = _pick(math.gcd(st.dec_seq, st.tc), DISPATCH_ROWS, LANES)
    nt = t // td
    row = pl.BlockSpec((td, d), lambda i: (i, 0))
    in_specs = (_index_specs(nt, td) + _index_specs(nt, td)
                + [pl.BlockSpec(memory_space=pl.ANY), row,
                   pl.BlockSpec((None, 1, d), _mod_index(st, layer, 5, td)),
                   pl.BlockSpec((td, LANES), lambda i: (i, 0)),
                   pl.BlockSpec((td, LANES), lambda i: (i, 0))])
    s0 = slot0.reshape(nt, 1, td)
    s1 = slot1.reshape(nt, 1, td)
    args = [s0, s0, s1, s1, ys, x, modr, w1, w2]
    out_specs, out_shape = [row], [jax.ShapeDtypeStruct((t, d), F32)]
    if post == "norm_mod":
        in_specs += [pl.BlockSpec((None, 1, d), lambda i: (layer + 1, 0, 0)),
                     pl.BlockSpec((None, 1, d), _mod_index(st, layer + 1, 1, td)),
                     pl.BlockSpec((None, 1, d), _mod_index(st, layer + 1, 0, td))]
        args += [post_args[0], modr, modr]
        out_specs.append(row)
        out_shape.append(jax.ShapeDtypeStruct((t, d), BF16))
    elif post == "final":
        in_specs.append(pl.BlockSpec((1, d), lambda i: (0, 0)))
        args.append(post_args[0])
    return pl.pallas_call(
        functools.partial(_combine_kernel, post=post),
        grid=(nt,),
        in_specs=in_specs,
        out_specs=out_specs,
        out_shape=out_shape,
        scratch_shapes=[pltpu.VMEM((2, td, d), F32), pltpu.VMEM((2, td, d), F32),
                        pltpu.SemaphoreType.DMA((2,))],
        compiler_params=_cparams(1),
        name="moe_combine",
    )(*args)


MOE_ROWS = 512


def _moe_layer(x, p, modr, st, layer, moe_idx, post, post_args):
    t, d = x.shape
    n_experts = p["router_w"].shape[-1]
    d_ff = p["moe_w_gate"].shape[-1]
    tm = MOE_ROWS
    hp, w1, w2, code, rank, cnt = _route(x, p["norm2_g"].reshape(-1, 1, d), modr,
                                         p["router_w"][moe_idx], st, layer, n_experts)
    counts = cnt[:n_experts, 0]
    padded = (counts + tm - 1) // tm * tm
    ends = jnp.cumsum(padded)
    starts = ends - padded
    n_tiles = (2 * t) // tm + n_experts
    n_slots = n_tiles * tm
    dest = starts[:, None] + rank[:n_experts]
    slot0 = jnp.sum(jnp.where(code[:n_experts] == 1, dest, 0), axis=0).astype(jnp.int32)
    slot1 = jnp.sum(jnp.where(code[:n_experts] == 2, dest, 0), axis=0).astype(jnp.int32)
    tile_row = jnp.arange(n_tiles, dtype=jnp.int32) * tm
    tile_expert = jnp.minimum(jnp.sum(tile_row[:, None] >= ends[None, :], axis=1), n_experts - 1)
    wid = (moe_idx * n_experts + tile_expert).astype(jnp.int32)
    nvalid = (ends[-1:] // tm).astype(jnp.int32)
    expert_ids = jnp.arange(n_experts, dtype=jnp.int32)[:, None]
    token_ids = jnp.arange(t, dtype=jnp.int32)
    e0 = jnp.sum(jnp.where(code[:n_experts] == 1, expert_ids, 0), axis=0)
    e1 = jnp.sum(jnp.where(code[:n_experts] == 2, expert_ids, 0), axis=0)
    pair_token = jnp.sort(jnp.concatenate([e0 * t + token_ids, e1 * t + token_ids])) % t
    slot_expert = jnp.repeat(tile_expert, tm)
    local = jnp.arange(n_slots, dtype=jnp.int32) - starts[slot_expert]
    compact_start = jnp.cumsum(counts) - counts
    compact = compact_start[slot_expert] + jnp.clip(local, 0, jnp.maximum(counts[slot_expert] - 1, 0))
    slot_token = pair_token[jnp.clip(compact, 0, 2 * t - 1)].astype(jnp.int32)

    xs = _dispatch(hp, slot_token)
    wg = p["moe_w_gate"].reshape(-1, d, d_ff)
    wu = p["moe_w_up"].reshape(-1, d, d_ff)
    wd = p["moe_w_down"].reshape(-1, d_ff, d)
    hidden = _gmm(xs, (wg, wu), wid, nvalid, mode="swiglu", out_dtype=BF16, tm=tm,
                  tn=_pick(d_ff, 512, LANES), name="moe_gate_up")
    ys = _gmm(hidden, (wd,), wid, nvalid, mode="plain", out_dtype=F32, tm=tm,
              tn=_pick(d, 512, LANES), name="moe_down")
    return _combine(ys, slot0, slot1, x, modr, w1, w2, st, layer, post, post_args)


def _dense_layer(x, p, modr, st, layer, ffn_idx):
    t, d = x.shape
    d_ff = p["ffn_w_gate"].shape[-1]
    h = _norm_mod(x, p["norm2_g"].reshape(-1, 1, d), modr, st, layer, 4, 3)
    hidden = _gmm(h, (p["ffn_w_gate"], p["ffn_w_up"]), ffn_idx, None, mode="swiglu",
                  out_dtype=BF16, tm=st.token_tile(1024), tn=_pick(d_ff, 512, LANES),
                  name="ffn_gate_up")
    tm = st.token_tile(512)
    return _gmm(hidden, (p["ffn_w_down"],), ffn_idx, None, mode="resid", out_dtype=F32, tm=tm,
                tn=_pick(d, 512, LANES), resid=(x, modr, _mod_index(st, layer, 5, tm)),
                name="ffn_down")


def _dft_radix(seq_len):
    r = 1
    while seq_len // r > 512 and r < 8:
        r *= 2
    return r


def kernel(x_prompt, x_sample, cache_k, cache_v, c, c_ctx, norm1_g, norm2_g, w_mod, b_mod, w_in,
           q_norm_g, k_norm_g, hy_conv_w, hy_conv_b, filt_w1, filt_b1, filt_w2, filt_b2, filt_w3,
           filt_log_decay, hy_skip, out_norm_g, w_out, ffn_w_gate, ffn_w_up, ffn_w_down, router_w,
           moe_w_gate, moe_w_up, moe_w_down, final_norm_g):
    p = dict(norm2_g=norm2_g, hy_conv_w=hy_conv_w, hy_conv_b=hy_conv_b, filt_w1=filt_w1,
             filt_b1=filt_b1, filt_w2=filt_w2, filt_b2=filt_b2, filt_w3=filt_w3,
             filt_log_decay=filt_log_decay, hy_skip=hy_skip, ffn_w_gate=ffn_w_gate,
             ffn_w_up=ffn_w_up, ffn_w_down=ffn_w_down, router_w=router_w, moe_w_gate=moe_w_gate,
             moe_w_up=moe_w_up, moe_w_down=moe_w_down)
    batch, seq, d = x_prompt.shape
    dec_batch, dec_seq, _ = x_sample.shape
    depth = w_in.shape[0]
    in_w = w_in.shape[-1]
    n_kv = cache_k.shape[3]
    kv_w = n_kv * HEAD_DIM
    a_w = (3 * d + 2 * kv_w - in_w) // 2
    hy_w = d - a_w
    u_col0 = a_w + 2 * kv_w
    past = cache_k.shape[2]
    st = _Stream(dec_batch, dec_seq, batch, seq)
    assert dec_batch + 1 <= MOD_ROWS

    x = jnp.concatenate([x_sample.reshape(st.ts, d), x_prompt.reshape(st.tc, d)], axis=0)
    cvecs = jnp.zeros((MOD_ROWS, d), F32).at[:dec_batch].set(c).at[dec_batch].set(c_ctx)
    modr = _mod_all(cvecs, w_mod, b_mod).reshape(depth * MOD_ROWS * 6, 1, d)
    cos2, sin2 = _rope_tables(st)
    dfts = {n: _Dft(n, _dft_radix(n)) for n in {dec_seq, seq}}
    norm1_g3 = norm1_g.reshape(depth, 1, d)
    qg3 = q_norm_g.reshape(depth, 1, HEAD_DIM)
    kg3 = k_norm_g.reshape(depth, 1, HEAD_DIM)
    tm = st.token_tile(1024)
    cache_k4 = cache_k.reshape(dec_batch, depth, past, kv_w)
    cache_v4 = cache_v.reshape(dec_batch, depth, past, kv_w)

    new_k, new_v = [], []
    h = y = None
    for layer in range(depth):
        if h is None:
            h = _norm_mod(x, norm1_g3, modr, st, layer, 1, 0)
        proj = _gmm(h, (w_in,), layer, None, mode="plain", out_dtype=BF16, tm=tm,
                    tn=_pick(in_w, 768, LANES), name="w_in")
        qn, kn, kc = _qk_norm_rope(proj, qg3, kg3, cos2, sin2, layer, a_w, kv_w, st)
        new_k.append(kc[st.ts:].reshape(batch, seq, n_kv, HEAD_DIM))
        new_v.append(proj[st.ts:, a_w + kv_w:a_w + 2 * kv_w].astype(F32)
                     .reshape(batch, seq, n_kv, HEAD_DIM))

        attn_s = _attention(qn, kn, proj, a_w + kv_w, n_kv, dec_batch, dec_seq, 0,
                            cache=(cache_k4, cache_v4, layer))
        attn_c = _attention(qn, kn, proj, a_w + kv_w, n_kv, batch, seq, st.ts)
        hy_s, hy_c = _hyena(proj, layer, p, st, u_col0, hy_w, dfts)
        merged = _merge_norm(attn_s, attn_c, hy_s, hy_c, out_norm_g, layer, st)
        x = _gmm(merged, (w_out,), layer, None, mode="resid", out_dtype=F32, tm=tm,
                 tn=_pick(d, 1024, LANES), resid=(x, modr, _mod_index(st, layer, 2, tm)),
                 name="w_out")
        h = None
        if layer % 2 == 0:
            x = _dense_layer(x, p, modr, st, layer, layer // 2)
        elif layer + 1 < depth:
            x, h = _moe_layer(x, p, modr, st, layer, layer // 2, "norm_mod", (norm1_g3,))
        else:
            (y,) = _moe_layer(x, p, modr, st, layer, layer // 2, "final",
                              (final_norm_g.reshape(1, d),))
    if y is None:
        y = _final_norm(x, final_norm_g)
    y_sample = y[:st.ts].reshape(dec_batch, dec_seq, d)
    y_prompt = y[st.ts:].reshape(batch, seq, d)
    return (y_prompt, y_sample, jnp.stack(new_k, axis=1), jnp.stack(new_v, axis=1))
```

```python
import cmath
import functools
import math

import jax
import jax.numpy as jnp
import numpy as np
from jax import lax
from jax.experimental import pallas as pl
from jax.experimental.pallas import tpu as pltpu

F32 = jnp.float32
BF16 = jnp.bfloat16

HEAD_DIM = 128
GRID_W = 64
ROPE_THETA = 10000.0
EPS = 1e-6
POS_FREQS = 16
HY_ORDER = 2
LANES = 128
MOD_ROWS = 8
VMEM_LIMIT_BYTES = 56 * 1024 * 1024


def _cparams(n_axes):
    return pltpu.CompilerParams(dimension_semantics=("arbitrary",) * n_axes,
                                vmem_limit_bytes=VMEM_LIMIT_BYTES)


def _pick(n, pref, mult):
    best = None
    d = mult
    while d <= min(n, pref):
        if n % d == 0:
            best = d
        d += mult
    assert best is not None, (n, pref, mult)
    return best


def _roundup(x, m):
    return (x + m - 1) // m * m


def _mod_kernel(c_ref, w_ref, b_ref, o_ref):
    c = c_ref[...]
    s = (c * jax.nn.sigmoid(c)).astype(BF16)
    o_ref[...] = jnp.dot(s, w_ref[...].astype(BF16), preferred_element_type=F32) + b_ref[...]


def _mod_all(cvecs, w_mod, b_mod):
    depth, d, n = w_mod.shape
    tn = _pick(n, 1024, LANES)
    return pl.pallas_call(
        _mod_kernel,
        grid=(depth, n // tn),
        in_specs=[pl.BlockSpec((MOD_ROWS, d), lambda l, j: (0, 0)),
                  pl.BlockSpec((None, d, tn), lambda l, j: (l, 0, j)),
                  pl.BlockSpec((None, 1, tn), lambda l, j: (l, 0, j))],
        out_specs=pl.BlockSpec((None, MOD_ROWS, tn), lambda l, j: (l, 0, j)),
        out_shape=jax.ShapeDtypeStruct((depth, MOD_ROWS, n), F32),
        compiler_params=_cparams(2),
        name="mod_all",
    )(cvecs, w_mod, b_mod.reshape(depth, 1, n))


def _rms(x, g):
    return x * lax.rsqrt(jnp.mean(x * x, axis=-1, keepdims=True) + EPS) * g


def _norm_mod_kernel(x_ref, g_ref, sc_ref, sh_ref, o_ref):
    y = _rms(x_ref[...], g_ref[...])
    o_ref[...] = (y * (1.0 + sc_ref[...]) + sh_ref[...]).astype(o_ref.dtype)


def _norm_kernel(x_ref, g_ref, o_ref):
    o_ref[...] = _rms(x_ref[...], g_ref[...]).astype(o_ref.dtype)


class _Stream:
    def __init__(self, dec_batch, dec_seq, batch, seq):
        self.dec_batch, self.dec_seq, self.batch, self.seq = dec_batch, dec_seq, batch, seq
        self.ts = dec_batch * dec_seq
        self.tc = batch * seq
        self.t = self.ts + self.tc

    def group(self, row):
        return jnp.minimum(row // self.dec_seq, self.dec_batch)

    def token_tile(self, pref):
        return _pick(math.gcd(self.dec_seq, self.tc), pref, 16)


def _mod_index(st, layer, which, tm):
    def index(i):
        return ((layer * MOD_ROWS + st.group(i * tm)) * 6 + which, 0, 0)
    return index


def _norm_mod(x, g, modr, st, layer, which_sc, which_sh):
    t, d = x.shape
    tm = st.token_tile(512)
    return pl.pallas_call(
        _norm_mod_kernel,
        grid=(t // tm,),
        in_specs=[pl.BlockSpec((tm, d), lambda i: (i, 0)),
                  pl.BlockSpec((None, 1, d), lambda i: (layer, 0, 0)),
                  pl.BlockSpec((None, 1, d), _mod_index(st, layer, which_sc, tm)),
                  pl.BlockSpec((None, 1, d), _mod_index(st, layer, which_sh, tm))],
        out_specs=pl.BlockSpec((tm, d), lambda i: (i, 0)),
        out_shape=jax.ShapeDtypeStruct((t, d), BF16),
        compiler_params=_cparams(1),
        name="norm_mod",
    )(x, g, modr, modr)


def _final_norm(x, g):
    t, d = x.shape
    tm = _pick(t, 512, 8)
    return pl.pallas_call(
        _norm_kernel,
        grid=(t // tm,),
        in_specs=[pl.BlockSpec((tm, d), lambda i: (i, 0)),
                  pl.BlockSpec((1, d), lambda i: (0, 0))],
        out_specs=pl.BlockSpec((tm, d), lambda i: (i, 0)),
        out_shape=jax.ShapeDtypeStruct((t, d), F32),
        compiler_params=_cparams(1),
        name="final_norm",
    )(x, g.reshape(1, d))


def _gmm_kernel(wid_ref, nv_ref, *refs, mode):
    i = pl.program_id(1)
    if mode == "swiglu":
        a_ref, wg_ref, wu_ref, o_ref, wgb, wub = refs
        wpairs = ((wg_ref, wgb), (wu_ref, wub))
    elif mode == "resid":
        a_ref, w_ref, x_ref, gate_ref, o_ref, wb = refs
        wpairs = ((w_ref, wb),)
    else:
        a_ref, w_ref, o_ref, wb = refs
        wpairs = ((w_ref, wb),)

    prev = wid_ref[jnp.maximum(i - 1, 0)]
    new_w = jnp.logical_or(i == 0, wid_ref[i] != prev)

    @pl.when(jnp.logical_and(new_w, i < nv_ref[0]))
    def _cast():
        for src, dst in wpairs:
            dst[...] = src[...].astype(BF16)

    @pl.when(i >= nv_ref[0])
    def _unused_tile():
        o_ref[...] = jnp.zeros_like(o_ref)

    @pl.when(i < nv_ref[0])
    def _compute():
        a = a_ref[...]
        if mode == "swiglu":
            g = jnp.dot(a, wgb[...], preferred_element_type=F32)
            u = jnp.dot(a, wub[...], preferred_element_type=F32)
            o_ref[...] = (g * jax.nn.sigmoid(g) * u).astype(o_ref.dtype)
        elif mode == "resid":
            acc = jnp.dot(a, wb[...], preferred_element_type=F32)
            o_ref[...] = x_ref[...] + gate_ref[...] * acc
        else:
            o_ref[...] = jnp.dot(a, wb[...], preferred_element_type=F32).astype(o_ref.dtype)


def _gmm(a, ws, wid, nvalid, *, mode, out_dtype, tm, tn, resid=None, name="gmm"):
    m = a.shape[0]
    _, k, n = ws[0].shape
    ka = a.shape[1]
    nt = m // tm
    if isinstance(wid, int):
        wid, nvalid = jnp.full((nt,), wid, jnp.int32), jnp.full((1,), nt, jnp.int32)
    assert m % tm == 0 and n % tn == 0 and wid.shape == (nt,)

    def a_index(j, i, wid_ref, nv_ref):
        return (jnp.minimum(i, nv_ref[0] - 1), 0)

    def w_index(j, i, wid_ref, nv_ref):
        return (wid_ref[i], 0, j)

    def o_index(j, i, wid_ref, nv_ref):
        return (i, j)

    in_specs = [pl.BlockSpec((tm, ka), a_index)]
    in_specs += [pl.BlockSpec((None, k, tn), w_index) for _ in ws]
    args = [a, *ws]
    if mode == "resid":
        x, modr, mod_index = resid
        in_specs += [pl.BlockSpec((tm, tn), o_index),
                     pl.BlockSpec((None, 1, tn),
                                  lambda j, i, wid_ref, nv_ref: mod_index(i)[:2] + (j,))]
        args += [x, modr]
    grid_spec = pltpu.PrefetchScalarGridSpec(
        num_scalar_prefetch=2,
        grid=(n // tn, nt),
        in_specs=in_specs,
        out_specs=pl.BlockSpec((tm, tn), o_index),
        scratch_shapes=[pltpu.VMEM((k, tn), BF16) for _ in ws],
    )
    return pl.pallas_call(
        functools.partial(_gmm_kernel, mode=mode),
        grid_spec=grid_spec,
        out_shape=jax.ShapeDtypeStruct((m, n), out_dtype),
        compiler_params=_cparams(2),
        name=name,
    )(wid, nvalid, *args)


def _qk_kernel(q_ref, k_ref, qg_ref, kg_ref, cos_ref, sin_ref, qo_ref, ko_ref, kc_ref, *, scale):
    cos = cos_ref[...]
    sin = sin_ref[...]

    def head(x, g):
        xn = _rms(x.astype(F32), g)
        return xn, xn * cos + pltpu.roll(xn, HEAD_DIM // 2, 1) * sin

    for h in range(q_ref.shape[1] // HEAD_DIM):
        sl = slice(h * HEAD_DIM, (h + 1) * HEAD_DIM)
        _, roped = head(q_ref[:, sl], qg_ref[...])
        qo_ref[:, sl] = (roped * scale).astype(qo_ref.dtype)
    for h in range(k_ref.shape[1] // HEAD_DIM):
        sl = slice(h * HEAD_DIM, (h + 1) * HEAD_DIM)
        xn, roped = head(k_ref[:, sl], kg_ref[...])
        ko_ref[:, sl] = roped.astype(ko_ref.dtype)
        kc_ref[:, sl] = xn


def _qk_norm_rope(proj, qg, kg, cos2, sin2, layer, a_w, kv_w, st):
    t = proj.shape[0]
    tm = st.token_tile(512)
    assert a_w % kv_w == 0
    return pl.pallas_call(
        functools.partial(_qk_kernel, scale=1.0 / math.sqrt(HEAD_DIM)),
        grid=(t // tm,),
        in_specs=[pl.BlockSpec((tm, a_w), lambda i: (i, 0)),
                  pl.BlockSpec((tm, kv_w), lambda i: (i, a_w // kv_w)),
                  pl.BlockSpec((None, 1, HEAD_DIM), lambda i: (layer, 0, 0)),
                  pl.BlockSpec((None, 1, HEAD_DIM), lambda i: (layer, 0, 0)),
                  pl.BlockSpec((tm, HEAD_DIM), lambda i: (i, 0)),
                  pl.BlockSpec((tm, HEAD_DIM), lambda i: (i, 0))],
        out_specs=[pl.BlockSpec((tm, a_w), lambda i: (i, 0)),
                   pl.BlockSpec((tm, kv_w), lambda i: (i, 0)),
                   pl.BlockSpec((tm, kv_w), lambda i: (i, 0))],
        out_shape=[jax.ShapeDtypeStruct((t, a_w), BF16),
                   jax.ShapeDtypeStruct((t, kv_w), BF16),
                   jax.ShapeDtypeStruct((t, kv_w), F32)],
        compiler_params=_cparams(1),
        name="qk_norm_rope",
    )(proj, proj, qg, kg, cos2, sin2)


def _rope_tables(st):
    pos = np.arange(st.dec_seq)
    inv = ROPE_THETA ** (-np.arange(0, HEAD_DIM // 2, 2, dtype=np.float64) / (HEAD_DIM // 2))
    ang = np.concatenate([(pos // GRID_W)[:, None] * inv, (pos % GRID_W)[:, None] * inv], axis=-1)
    cos2 = np.concatenate([np.cos(ang), np.cos(ang)], axis=-1)
    sin2 = np.concatenate([-np.sin(ang), np.sin(ang)], axis=-1)
    cos_all = np.concatenate([np.tile(cos2, (st.dec_batch, 1)), np.ones((st.tc, HEAD_DIM))])
    sin_all = np.concatenate([np.tile(sin2, (st.dec_batch, 1)), np.zeros((st.tc, HEAD_DIM))])
    return jnp.asarray(cos_all, F32), jnp.asarray(sin_all, F32)


KEY_CHUNK = 512


def _attn_kernel(*refs, g, has_cache):
    if has_cache:
        q_ref, ck_ref, cv_ref, k_ref, v_ref, o_ref = refs
    else:
        q_ref, k_ref, v_ref, o_ref = refs
    tq = q_ref.shape[0]
    q = jnp.concatenate([q_ref[:, h * HEAD_DIM:(h + 1) * HEAD_DIM] for h in range(g)], axis=0)
    lk = k_ref.shape[0]
    parts = []
    if has_cache:
        parts.append((ck_ref, cv_ref, 0, ck_ref.shape[0]))
    c0 = 0
    while c0 < lk:
        n = min(KEY_CHUNK, lk - c0)
        parts.append((k_ref, v_ref, c0, n))
        c0 += n
    m = acc = None
    for kr, vr, c0, n in parts:
        kc = kr[c0:c0 + n, :].astype(BF16)
        va = jnp.concatenate([vr[c0:c0 + n, :].astype(BF16), jnp.ones((n, HEAD_DIM), BF16)], axis=1)
        s = lax.dot_general(q, kc, (((1,), (1,)), ((), ())), preferred_element_type=F32)
        mc = jnp.max(s, axis=-1, keepdims=True)
        if m is None:
            m = mc
            acc = jnp.dot(jnp.exp(s - m).astype(BF16), va, preferred_element_type=F32)
        else:
            m_new = jnp.maximum(m, mc)
            acc = (jnp.exp(m - m_new) * acc
                   + jnp.dot(jnp.exp(s - m_new).astype(BF16), va, preferred_element_type=F32))
            m = m_new
    o = acc[:, :HEAD_DIM] / acc[:, HEAD_DIM:]
    for h in range(g):
        o_ref[:, h * HEAD_DIM:(h + 1) * HEAD_DIM] = o[h * tq:(h + 1) * tq].astype(o_ref.dtype)


def _attention(qn, kn, proj, v_col0, n_kv, nseq, seq_len, row0, cache=None):
    aw = qn.shape[1]
    g = aw // HEAD_DIM // n_kv
    tq = _pick(seq_len, 256, 16)
    nq = seq_len // tq
    assert row0 % seq_len == 0 and v_col0 % HEAD_DIM == 0
    in_specs = [pl.BlockSpec((tq, g * HEAD_DIM), lambda b, kv, i: (row0 // tq + b * nq + i, kv))]
    args = [qn]
    if cache is not None:
        ck, cv, layer = cache
        past = ck.shape[2]
        cspec = pl.BlockSpec((None, None, past, HEAD_DIM), lambda b, kv, i: (b, layer, 0, kv))
        in_specs += [cspec, cspec]
        args += [ck, cv]
    in_specs += [pl.BlockSpec((seq_len, HEAD_DIM), lambda b, kv, i: (row0 // seq_len + b, kv)),
                 pl.BlockSpec((seq_len, HEAD_DIM),
                              lambda b, kv, i: (row0 // seq_len + b, v_col0 // HEAD_DIM + kv))]
    args += [kn, proj]
    return pl.pallas_call(
        functools.partial(_attn_kernel, g=g, has_cache=cache is not None),
        grid=(nseq, n_kv, nq),
        in_specs=in_specs,
        out_specs=pl.BlockSpec((tq, g * HEAD_DIM), lambda b, kv, i: (b * nq + i, kv)),
        out_shape=jax.ShapeDtypeStruct((nseq * seq_len, aw), BF16),
        compiler_params=_cparams(3),
        name="attention",
    )(*args)


def _filter_kernel(feat_ref, w1_ref, b1_ref, w2_ref, b2_ref, w3_ref, ld_ref, o_ref, h_ref, *,
                   seq_len, hy_w):
    hp = lax.Precision.HIGHEST

    @pl.when(pl.program_id(1) == 0)
    def _hidden():
        h = jnp.sin(jnp.dot(feat_ref[...], w1_ref[...], precision=hp, preferred_element_type=F32)
                    + b1_ref[...])
        h_ref[...] = jnp.sin(jnp.dot(h, w2_ref[...], precision=hp, preferred_element_type=F32)
                             + b2_ref[...])

    h = jnp.dot(h_ref[...].astype(BF16), w3_ref[...].astype(BF16), preferred_element_type=F32)
    tl, tn = h.shape
    row = lax.broadcasted_iota(jnp.int32, (tl, 1), 0) + pl.program_id(0) * tl
    t = row.astype(F32) / seq_len
    h = h * jnp.exp(-jnp.exp(ld_ref[...]) * t)
    col = lax.broadcasted_iota(jnp.int32, (1, tn), 1) + pl.program_id(1) * tn
    backward = (col // hy_w) % 2 == 1
    o_ref[...] = jnp.where(row == 0, jnp.where(backward, 0.0, h), h).astype(o_ref.dtype)


def _filter_features(seq_len):
    t = np.arange(seq_len, dtype=np.float32) / np.float32(seq_len)
    fr = np.arange(1, POS_FREQS + 1, dtype=np.float32)
    ang = float(np.float32(2.0 * math.pi)) * t[:, None].astype(np.float64) * fr
    feat = np.concatenate([t[:, None], np.sin(ang), np.cos(ang)], axis=-1)
    out = np.zeros((seq_len, LANES), np.float32)
    out[:, :feat.shape[1]] = feat
    return jnp.asarray(out)


def _pad2(x, rows, cols):
    return jnp.pad(x, ((0, rows - x.shape[0]), (0, cols - x.shape[1])))


def _hyena_filters(seq_len, layer, p, hy_w):
    n = p["filt_w3"].shape[-1]
    feat = _filter_features(seq_len)
    w1 = _pad2(p["filt_w1"][layer], LANES, LANES)
    b1 = _pad2(p["filt_b1"][layer][None, :], 1, LANES)
    w2 = _pad2(p["filt_w2"][layer], LANES, LANES)
    b2 = _pad2(p["filt_b2"][layer][None, :], 1, LANES)
    w3 = _pad2(p["filt_w3"][layer], LANES, n)
    ld = p["filt_log_decay"][layer][None, :]
    tl = _pick(seq_len, 1024, 16)
    tn = _pick(n, 2048, LANES)
    return pl.pallas_call(
        functools.partial(_filter_kernel, seq_len=seq_len, hy_w=hy_w),
        grid=(seq_len // tl, n // tn),
        in_specs=[pl.BlockSpec((tl, LANES), lambda i, j: (i, 0)),
                  pl.BlockSpec((LANES, LANES), lambda i, j: (0, 0)),
                  pl.BlockSpec((1, LANES), lambda i, j: (0, 0)),
                  pl.BlockSpec((LANES, LANES), lambda i, j: (0, 0)),
                  pl.BlockSpec((1, LANES), lambda i, j: (0, 0)),
                  pl.BlockSpec((LANES, tn), lambda i, j: (0, j)),
                  pl.BlockSpec((1, tn), lambda i, j: (0, j))],
        out_specs=pl.BlockSpec((tl, tn), lambda i, j: (i, j)),
        out_shape=jax.ShapeDtypeStruct((seq_len, n), BF16),
        scratch_shapes=[pltpu.VMEM((tl, LANES), F32)],
        compiler_params=_cparams(2),
        name="hy_filter",
    )(feat, w1, b1, w2, b2, w3, ld)


class _Dft:
    def __init__(self, seq_len, radix):
        self.seq_len, self.r = seq_len, radix
        n = 2 * seq_len
        self.n1 = n // radix
        self.la = seq_len // radix
        k1 = self.n1 // 2 + 1
        self.k1p = _roundup(k1, 16)
        a = np.arange(self.la)[None, :]
        kk = np.arange(self.k1p)[:, None]
        valid = kk < k1
        th = 2.0 * math.pi * kk * a / self.n1
        fwd = np.concatenate([np.where(valid, np.cos(th), 0.0), np.where(valid, -np.sin(th), 0.0)])
        wgt = np.where((kk == 0) | (kk == self.n1 // 2), 1.0, 2.0) * valid / n
        inv = np.concatenate([wgt * np.cos(th), -wgt * np.sin(th)]).T
        self.fwd = jnp.asarray(fwd, F32)
        self.inv = jnp.asarray(inv, F32)
        ph = 2.0 * math.pi * kk / n
        ones = np.ones((1, LANES))
        self.tw_re = jnp.asarray(np.cos(ph) * ones, F32)
        self.tw_im = jnp.asarray(-np.sin(ph) * ones, F32)


def _cmul_const(x, w):
    wr, wi = w.real, w.imag
    if abs(wi) < 1e-12:
        if abs(wr - 1.0) < 1e-12:
            return x
        if abs(wr + 1.0) < 1e-12:
            return (-x[0], -x[1])
    if abs(wr) < 1e-12:
        if abs(wi - 1.0) < 1e-12:
            return (-x[1], x[0])
        if abs(wi + 1.0) < 1e-12:
            return (x[1], -x[0])
    return (x[0] * wr - x[1] * wi, x[0] * wi + x[1] * wr)


def _fft_list(xs, sign):
    n = len(xs)
    if n == 1:
        return xs
    ev = _fft_list(xs[0::2], sign)
    od = _fft_list(xs[1::2], sign)
    out = [None] * n
    for k in range(n // 2):
        t = _cmul_const(od[k], cmath.exp(sign * 2j * math.pi * k / n))
        out[k] = (ev[k][0] + t[0], ev[k][1] + t[1])
        out[k + n // 2] = (ev[k][0] - t[0], ev[k][1] - t[1])
    return out


ROW_CHUNK = 16
LONGCONV_ROWS = 1024


def _lane_tile(x, width):
    return x if width == LANES else jnp.concatenate([x] * (width // LANES), axis=1)


def _twiddles(tw_re_ref, tw_im_ref, rows, r, tc):
    if r == 1:
        return []
    w1 = (tw_re_ref[rows, :], tw_im_ref[rows, :])
    ws = [w1]
    for _ in range(r - 2):
        pr, pi = ws[-1]
        ws.append((pr * w1[0] - pi * w1[1], pr * w1[1] + pi * w1[0]))
    return [(_lane_tile(wr, tc), _lane_tile(wi, tc)) for wr, wi in ws]


def _spectrum_chunk(y_ref, tws, c0, k1p, r):
    rows = pl.ds(c0, ROW_CHUNK)
    rows_im = pl.ds(k1p + c0, ROW_CHUNK)
    ts = []
    for b in range(r):
        yr = y_ref[b, rows, :]
        yi = y_ref[b, rows_im, :]
        if b == 0:
            ts.append((yr, yi))
        else:
            wr, wi = tws[b - 1]
            ts.append((yr * wr - yi * wi, yr * wi + yi * wr))
    return _fft_list(ts, -1)


def _lanes(q):
    return slice(q * LANES, (q + 1) * LANES)


def _conv3_group(u_ref, w_ref, b_ref, q, seq):
    x = u_ref[seq, _lanes(q)].astype(F32)
    n = x.shape[0]
    row = lax.broadcasted_iota(jnp.int32, x.shape, 0)
    prev = jnp.where(row == 0, 0.0, pltpu.roll(x, 1, 0))
    nxt = jnp.where(row == n - 1, 0.0, pltpu.roll(x, n - 1, 0))
    w = w_ref[:, _lanes(q)]
    return prev * w[0:1] + x * w[1:2] + nxt * w[2:3] + b_ref[:, _lanes(q)]


def _stage1(f_ref, load_group, xf_ref, y_ref, r, la):
    nq = xf_ref.shape[0]
    for q in range(nq):
        xf_ref[q] = load_group(q)
    f = f_ref[...].astype(BF16)
    for b in range(r):
        xb = jnp.concatenate([xf_ref[q, pl.ds(b, la, stride=r), :] for q in range(nq)], axis=1)
        y_ref[b] = jnp.dot(f, xb.astype(BF16), preferred_element_type=F32)


def _spectrum_kernel(*refs, r, k1p, la):
    if r > 1:
        f_ref, tw_re_ref, tw_im_ref, hf_ref, hb_ref, kr_ref, ki_ref, xf_ref, y_ref, acc_ref = refs
    else:
        f_ref, hf_ref, hb_ref, kr_ref, ki_ref, xf_ref, y_ref, acc_ref = refs
        tw_re_ref = tw_im_ref = None
    tc = y_ref.shape[-1]

    def sweep(src_ref, backward):
        _stage1(f_ref, lambda q: src_ref[:, _lanes(q)].astype(F32), xf_ref, y_ref, r, la)

        def chunk(c, carry):
            c0 = pl.multiple_of(c * ROW_CHUNK, ROW_CHUNK)
            rows = pl.ds(c0, ROW_CHUNK)
            rows_im = pl.ds(k1p + c0, ROW_CHUNK)
            x = _spectrum_chunk(y_ref, _twiddles(tw_re_ref, tw_im_ref, rows, r, tc), c0, k1p, r)
            for k2 in range(r):
                if backward:
                    kr_ref[k2, rows, :] = (acc_ref[k2, rows, :] + x[k2][0]).astype(kr_ref.dtype)
                    ki_ref[k2, rows, :] = (acc_ref[k2, rows_im, :] - x[k2][1]).astype(ki_ref.dtype)
                else:
                    acc_ref[k2, rows, :] = x[k2][0]
                    acc_ref[k2, rows_im, :] = x[k2][1]
            return carry

        lax.fori_loop(0, k1p // ROW_CHUNK, chunk, 0)

    sweep(hf_ref, False)
    sweep(hb_ref, True)


def _filter_spectrum(taps, dft, hy_w):
    r, la, k1p, seq_len = dft.r, dft.la, dft.k1p, dft.seq_len
    tc = _pick(hy_w, 256, LANES)
    nct = hy_w // tc

    def tap_spec(direction):
        return pl.BlockSpec((seq_len, tc), lambda o, j: (0, (o * 2 + direction) * nct + j))

    const = [pl.BlockSpec(dft.fwd.shape, lambda o, j: (0, 0))]
    cargs = [dft.fwd]
    if r > 1:
        const += [pl.BlockSpec(dft.tw_re.shape, lambda o, j: (0, 0))] * 2
        cargs += [dft.tw_re, dft.tw_im]
    out_spec = pl.BlockSpec((None, r, k1p, tc), lambda o, j: (o, 0, 0, j))
    shape = jax.ShapeDtypeStruct((HY_ORDER, r, k1p, hy_w), BF16)
    return pl.pallas_call(
        functools.partial(_spectrum_kernel, r=r, k1p=k1p, la=la),
        grid=(HY_ORDER, nct),
        in_specs=const + [tap_spec(0), tap_spec(1)],
        out_specs=[out_spec, out_spec],
        out_shape=[shape, shape],
        scratch_shapes=[pltpu.VMEM((tc // LANES, seq_len, LANES), F32),
                        pltpu.VMEM((r, 2 * k1p, tc), F32),
                        pltpu.VMEM((r, 2 * k1p, tc), F32)],
        compiler_params=_cparams(2),
        name="hy_filter_spectrum",
    )(*cargs, taps, taps)


def _longconv_kernel(*refs, n_sub, **kw):
    seq_len = refs[-3].shape[1]
    for sub in range(n_sub):
        _longconv_one(refs, slice(sub * seq_len, (sub + 1) * seq_len), **kw)


def _longconv_one(refs, seq, *, r, k1p, la, z_conv):
    refs = list(refs)
    f_ref, fi_ref = refs[:2]
    del refs[:2]
    tw_re_ref = tw_im_ref = None
    if r > 1:
        tw_re_ref, tw_im_ref = refs[:2]
        del refs[:2]
    z_ref = refs.pop(0)
    if z_conv:
        zw_ref, zb_ref = refs[:2]
        del refs[:2]
    gate_ref, gw_ref, gb_ref, skip_ref, kr_ref, ki_ref, o_ref, xf_ref, y_ref, g_ref = refs
    tc = y_ref.shape[-1]
    if z_conv:
        load_z = lambda q: _conv3_group(z_ref, zw_ref, zb_ref, q, seq)
    else:
        load_z = lambda q: z_ref[seq, _lanes(q)].astype(F32)
    _stage1(f_ref, load_z, xf_ref, y_ref, r, la)

    def chunk(c, carry):
        c0 = pl.multiple_of(c * ROW_CHUNK, ROW_CHUNK)
        rows = pl.ds(c0, ROW_CHUNK)
        rows_im = pl.ds(k1p + c0, ROW_CHUNK)
        tws = _twiddles(tw_re_ref, tw_im_ref, rows, r, tc)
        x = _spectrum_chunk(y_ref, tws, c0, k1p, r)
        prod = []
        for k2 in range(r):
            kr = kr_ref[k2, rows, :].astype(F32)
            ki = ki_ref[k2, rows, :].astype(F32)
            prod.append((x[k2][0] * kr - x[k2][1] * ki, x[k2][0] * ki + x[k2][1] * kr))
        s = _fft_list(prod, +1)
        for b in range(r):
            if b == 0:
                gr, gi = s[0]
            else:
                wr, wi = tws[b - 1]
                gr = s[b][0] * wr + s[b][1] * wi
                gi = s[b][1] * wr - s[b][0] * wi
            g_ref[b, rows, :] = gr.astype(g_ref.dtype)
            g_ref[b, rows_im, :] = gi.astype(g_ref.dtype)
        return carry

    lax.fori_loop(0, k1p // ROW_CHUNK, chunk, 0)
    fi = fi_ref[...].astype(BF16)
    for b in range(r):
        yb = jnp.dot(fi, g_ref[b], preferred_element_type=F32)
        rows = pl.ds(b, la, stride=r)
        for q in range(xf_ref.shape[0]):
            xf_ref[q, rows, :] = yb[:, _lanes(q)] + skip_ref[:, _lanes(q)] * xf_ref[q, rows, :]
    for q in range(xf_ref.shape[0]):
        gate = _conv3_group(gate_ref, gw_ref, gb_ref, q, seq)
        o_ref[seq, _lanes(q)] = (gate * xf_ref[q]).astype(o_ref.dtype)


def _longconv(proj, u_col0, row0, layer, conv_w, conv_b, z_prev, order, skip, kf_re, kf_im, dft,
              nseq, hy_w):
    r, la, k1p, seq_len = dft.r, dft.la, dft.k1p, dft.seq_len
    tc = _pick(math.gcd(hy_w, u_col0), 256 if r > 1 else 1024, LANES)
    n_sub = _pick(math.gcd(nseq, row0 // seq_len) or nseq, max(1, LONGCONV_ROWS // seq_len), 1)
    blk = n_sub * seq_len
    assert row0 % blk == 0
    nct = hy_w // tc
    seq0 = row0 // blk
    z_conv = z_prev is None

    def u_specs(part):
        return [pl.BlockSpec((blk, tc), lambda s, j: (seq0 + s, u_col0 // tc + part * nct + j)),
                pl.BlockSpec((None, 3, tc), lambda s, j: (layer, 0, part * nct + j)),
                pl.BlockSpec((None, 1, tc), lambda s, j: (layer, 0, part * nct + j))]

    cdim = lambda s, j: (0, 0)
    in_specs = [pl.BlockSpec(dft.fwd.shape, cdim), pl.BlockSpec(dft.inv.shape, cdim)]
    args = [dft.fwd, dft.inv]
    if r > 1:
        in_specs += [pl.BlockSpec(dft.tw_re.shape, cdim)] * 2
        args += [dft.tw_re, dft.tw_im]
    if z_conv:
        in_specs += u_specs(0)
        args += [proj, conv_w, conv_b]
    else:
        in_specs.append(pl.BlockSpec((blk, tc), lambda s, j: (s, j)))
        args.append(z_prev)
    in_specs += u_specs(1 + order)
    args += [proj, conv_w, conv_b]
    in_specs += [pl.BlockSpec((None, 1, tc), lambda s, j: (layer * HY_ORDER + order, 0, j)),
                 pl.BlockSpec((None, r, k1p, tc), lambda s, j: (order, 0, 0, j)),
                 pl.BlockSpec((None, r, k1p, tc), lambda s, j: (order, 0, 0, j))]
    args += [skip, kf_re, kf_im]
    return pl.pallas_call(
        functools.partial(_longconv_kernel, n_sub=n_sub, r=r, k1p=k1p, la=la, z_conv=z_conv),
        grid=(nseq // n_sub, nct),
        in_specs=in_specs,
        out_specs=pl.BlockSpec((blk, tc), lambda s, j: (s, j)),
        out_shape=jax.ShapeDtypeStruct((nseq * seq_len, hy_w), BF16),
        scratch_shapes=[pltpu.VMEM((tc // LANES, seq_len, LANES), F32),
                        pltpu.VMEM((r, 2 * k1p, tc), F32),
                        pltpu.VMEM((r, 2 * k1p, tc), BF16)],
        compiler_params=_cparams(2),
        name="hy_longconv",
    )(*args)


def _hyena(proj, layer, p, st, u_col0, hy_w, dfts):
    skip = p["hy_skip"].reshape(-1, 1, hy_w)
    conv_w = p["hy_conv_w"]
    conv_b = p["hy_conv_b"].reshape(-1, 1, 3 * hy_w)
    outs = []
    for nseq, seq_len, row0 in ((st.dec_batch, st.dec_seq, 0), (st.batch, st.seq, st.ts)):
        dft = dfts[seq_len]
        taps = _hyena_filters(seq_len, layer, p, hy_w)
        kf_re, kf_im = _filter_spectrum(taps, dft, hy_w)
        z = None
        for order in range(HY_ORDER):
            z = _longconv(proj, u_col0, row0, layer, conv_w, conv_b, z, order, skip, kf_re, kf_im,
                          dft, nseq, hy_w)
        outs.append(z)
    return outs


def _merge_kernel(as_ref, ac_ref, hs_ref, hc_ref, ga_ref, gh_ref, o_ref, *, n_sample_tiles):
    aw = as_ref.shape[1]
    sample = pl.program_id(0) < n_sample_tiles
    a = jnp.where(sample, as_ref[...], ac_ref[...]).astype(F32)
    h = jnp.where(sample, hs_ref[...], hc_ref[...]).astype(F32)
    o_ref[:, :aw] = _rms(a, ga_ref[...]).astype(o_ref.dtype)
    o_ref[:, aw:] = _rms(h, gh_ref[...]).astype(o_ref.dtype)


def _merge_norm(attn_s, attn_c, hy_s, hy_c, g, layer, st):
    aw, hw = attn_s.shape[1], hy_s.shape[1]
    assert aw == hw
    tm = st.token_tile(512)
    ns = st.ts // tm
    g3 = g.reshape(g.shape[0], 1, aw + hw)
    s_idx = lambda i: (jnp.minimum(i, ns - 1), 0)
    c_idx = lambda i: (jnp.maximum(i - ns, 0), 0)
    return pl.pallas_call(
        functools.partial(_merge_kernel, n_sample_tiles=ns),
        grid=(st.t // tm,),
        in_specs=[pl.BlockSpec((tm, aw), s_idx), pl.BlockSpec((tm, aw), c_idx),
                  pl.BlockSpec((tm, hw), s_idx), pl.BlockSpec((tm, hw), c_idx),
                  pl.BlockSpec((None, 1, aw), lambda i: (layer, 0, 0)),
                  pl.BlockSpec((None, 1, hw), lambda i: (layer, 0, 1))],
        out_specs=pl.BlockSpec((tm, aw + hw), lambda i: (i, 0)),
        out_shape=jax.ShapeDtypeStruct((st.t, aw + hw), BF16),
        compiler_params=_cparams(1),
        name="merge_norm",
    )(attn_s, attn_c, hy_s, hy_c, g3, g3)


def _route_kernel(x_ref, g_ref, sc_ref, sh_ref, rw_ref, hp_ref, w1_ref, w2_ref, code_ref,
                  rank_ref, cnt_ref, carry_ref, *, n_experts):
    i = pl.program_id(0)

    @pl.when(i == 0)
    def _init():
        carry_ref[...] = jnp.zeros_like(carry_ref)

    h = _rms(x_ref[...], g_ref[...]) * (1.0 + sc_ref[...]) + sh_ref[...]
    nq = h.shape[1] // LANES
    for q in range(nq):
        hp_ref[pl.ds(q, h.shape[0], stride=nq), :] = h[:, _lanes(q)]

    logits = jnp.dot(h, rw_ref[...], precision=lax.Precision.HIGHEST, preferred_element_type=F32)
    lane = lax.broadcasted_iota(jnp.int32, logits.shape, 1)
    neg = jnp.float32(-jnp.inf)
    logits = jnp.where(lane < n_experts, logits, neg)
    m1 = jnp.max(logits, axis=-1, keepdims=True)
    i1 = jnp.min(jnp.where(logits == m1, lane, LANES), axis=-1, keepdims=True)
    rest = jnp.where(lane == i1, neg, logits)
    m2 = jnp.max(rest, axis=-1, keepdims=True)
    i2 = jnp.min(jnp.where(rest == m2, lane, LANES), axis=-1, keepdims=True)
    e2 = jnp.exp(m2 - m1)
    w1 = 1.0 / (1.0 + e2)
    w2 = e2 / (1.0 + e2)
    w1_ref[...] = jnp.broadcast_to(w1, w1_ref.shape)
    w2_ref[...] = jnp.broadcast_to(w2, w2_ref.shape)

    code = jnp.where(lane == i1, 1.0, jnp.where(lane == i2, 2.0, 0.0))
    code_t = code.T[:code_ref.shape[0], :]
    code_ref[...] = code_t.astype(jnp.int32)
    sel = (code_t > 0.0).astype(BF16)
    tm = sel.shape[1]
    upper = (lax.broadcasted_iota(jnp.int32, (tm, tm), 0)
             < lax.broadcasted_iota(jnp.int32, (tm, tm), 1)).astype(BF16)
    excl = jnp.dot(sel, upper, preferred_element_type=F32)
    carry = carry_ref[...]
    rank_ref[...] = (excl + carry[:, :1]).astype(jnp.int32)
    carry = carry + jnp.sum(sel.astype(F32), axis=-1, keepdims=True)
    carry_ref[...] = carry
    cnt_ref[...] = carry.astype(jnp.int32)


def _route(x, g, modr, rw, st, layer, n_experts):
    t, d = x.shape
    tm = st.token_tile(512)
    assert tm % LANES == 0 and n_experts <= 8
    rwp = _pad2(rw, d, LANES)
    row_spec = pl.BlockSpec((tm, LANES), lambda i: (i, 0))
    col_spec = pl.BlockSpec((8, tm), lambda i: (0, i))
    return pl.pallas_call(
        functools.partial(_route_kernel, n_experts=n_experts),
        grid=(t // tm,),
        in_specs=[pl.BlockSpec((tm, d), lambda i: (i, 0)),
                  pl.BlockSpec((None, 1, d), lambda i: (layer, 0, 0)),
                  pl.BlockSpec((None, 1, d), _mod_index(st, layer, 4, tm)),
                  pl.BlockSpec((None, 1, d), _mod_index(st, layer, 3, tm)),
                  pl.BlockSpec((d, LANES), lambda i: (0, 0))],
        out_specs=[pl.BlockSpec((tm * d // LANES, LANES), lambda i: (i, 0)), row_spec, row_spec,
                   col_spec, col_spec, pl.BlockSpec((8, LANES), lambda i: (0, 0))],
        out_shape=[jax.ShapeDtypeStruct((t * d // LANES, LANES), F32),
                   jax.ShapeDtypeStruct((t, LANES), F32),
                   jax.ShapeDtypeStruct((t, LANES), F32),
                   jax.ShapeDtypeStruct((8, t), jnp.int32),
                   jax.ShapeDtypeStruct((8, t), jnp.int32),
                   jax.ShapeDtypeStruct((8, LANES), jnp.int32)],
        scratch_shapes=[pltpu.VMEM((8, LANES), F32)],
        compiler_params=_cparams(1),
        name="moe_route",
    )(x, g, modr, modr, rwp)


DISPATCH_ROWS = 256


GATHER_UNROLL = 8


def _issue_row_gather(idx_ref, src_ref, dst_ref, sem, slab):
    def trip(g, carry):
        for u in range(GATHER_UNROLL):
            rr = g * GATHER_UNROLL + u
            src = src_ref.at[pl.ds(pl.multiple_of(idx_ref[0, rr] * slab, slab), slab)]
            dst = dst_ref.at[pl.ds(pl.multiple_of(rr * slab, slab), slab)]
            pltpu.make_async_copy(src, dst, sem).start(priority=u % 2)
        return carry

    lax.fori_loop(0, dst_ref.shape[0] // slab // GATHER_UNROLL, trip, 0)


def _wait_row_gather(dst_ref, sem):
    pltpu.make_async_copy(dst_ref, dst_ref, sem).wait()


def _pipelined_gather(idx_refs, next_idx_refs, src_ref, bufs, sems, slab=1):
    i = pl.program_id(0)
    cur = i % 2

    @pl.when(i == 0)
    def _first():
        for idx_ref, buf in zip(idx_refs, bufs):
            _issue_row_gather(idx_ref, src_ref, buf.at[0], sems.at[0], slab)

    @pl.when(i + 1 < pl.num_programs(0))
    def _prefetch():
        for idx_ref, buf in zip(next_idx_refs, bufs):
            _issue_row_gather(idx_ref, src_ref, buf.at[1 - cur], sems.at[1 - cur], slab)

    for buf in bufs:
        _wait_row_gather(buf.at[cur], sems.at[cur])
    return cur


def _dispatch_kernel(tok_ref, tok_next_ref, h_ref, xs_ref, rows_ref, sems):
    td, d = xs_ref.shape
    nq = d // LANES
    cur = _pipelined_gather([tok_ref], [tok_next_ref], h_ref, [rows_ref], sems, slab=nq)
    for q in range(nq):
        xs_ref[:, _lanes(q)] = rows_ref[cur, pl.ds(q, td, stride=nq), :].astype(xs_ref.dtype)


def _index_specs(nt, td):
    return [pl.BlockSpec((None, 1, td), lambda i: (i, 0, 0), memory_space=pltpu.SMEM),
            pl.BlockSpec((None, 1, td), lambda i: (jnp.minimum(i + 1, nt - 1), 0, 0),
                         memory_space=pltpu.SMEM)]


def _dispatch(h, slot_token, d):
    n_slots = slot_token.shape[0]
    td = _pick(n_slots, DISPATCH_ROWS, LANES)
    nt = n_slots // td
    tok = slot_token.reshape(nt, 1, td)
    return pl.pallas_call(
        _dispatch_kernel,
        grid=(nt,),
        in_specs=_index_specs(nt, td) + [pl.BlockSpec(memory_space=pl.ANY)],
        out_specs=pl.BlockSpec((td, d), lambda i: (i, 0)),
        out_shape=jax.ShapeDtypeStruct((n_slots, d), BF16),
        scratch_shapes=[pltpu.VMEM((2, td * d // LANES, LANES), F32),
                        pltpu.SemaphoreType.DMA((2,))],
        compiler_params=_cparams(1),
        name="moe_dispatch",
    )(tok, tok, h)


def _combine_kernel(*refs, post):
    (s0_ref, s0n_ref, s1_ref, s1n_ref, ys_ref, x_ref, gate_ref, w1_ref, w2_ref) = refs[:9]
    rest = refs[9:]
    n_post = {"none": 0, "norm_mod": 3, "final": 1}[post]
    post_refs, rest = rest[:n_post], rest[n_post:]
    n_out = 1 if post in ("none", "final") else 2
    outs, (ya, yb, sems) = rest[:n_out], rest[n_out:]
    cur = _pipelined_gather([s0_ref, s1_ref], [s0n_ref, s1n_ref], ys_ref, [ya, yb], sems)
    w1 = _lane_tile(w1_ref[...], ya.shape[-1])
    w2 = _lane_tile(w2_ref[...], ya.shape[-1])
    xn = x_ref[...] + gate_ref[...] * (w1 * ya[cur] + w2 * yb[cur])
    if post == "final":
        outs[0][...] = _rms(xn, post_refs[0][...])
    else:
        outs[0][...] = xn
    if post == "norm_mod":
        g_ref, sc_ref, sh_ref = post_refs
        outs[1][...] = (_rms(xn, g_ref[...]) * (1.0 + sc_ref[...]) + sh_ref[...]).astype(outs[1].dtype)


def _combine(ys, slot0, slot1, x, modr, w1, w2, st, layer, post, post_args=()):
    t, d = x.shape
    td = _pick(math.gcd(st.dec_seq, st.tc), DISPATCH_ROWS, LANES)
    nt = t // td
    row = pl.BlockSpec((td, d), lambda i: (i, 0))
    in_specs = (_index_specs(nt, td) + _index_specs(nt, td)
                + [pl.BlockSpec(memory_space=pl.ANY), row,
                   pl.BlockSpec((None, 1, d), _mod_index(st, layer, 5, td)),
                   pl.BlockSpec((td, LANES), lambda i: (i, 0)),
                   pl.BlockSpec((td, LANES), lambda i: (i, 0))])
    s0 = slot0.reshape(nt, 1, td)
    s1 = slot1.reshape(nt, 1, td)
    args = [s0, s0, s1, s1, ys, x, modr, w1, w2]
    out_specs, out_shape = [row], [jax.ShapeDtypeStruct((t, d), F32)]
    if post == "norm_mod":
        in_specs += [pl.BlockSpec((None, 1, d), lambda i: (layer + 1, 0, 0)),
                     pl.BlockSpec((None, 1, d), _mod_index(st, layer + 1, 1, td)),
                     pl.BlockSpec((None, 1, d), _mod_index(st, layer + 1, 0, td))]
        args += [post_args[0], modr, modr]
        out_specs.append(row)
        out_shape.append(jax.ShapeDtypeStruct((t, d), BF16))
    elif post == "final":
        in_specs.append(pl.BlockSpec((1, d), lambda i: (0, 0)))
        args.append(post_args[0])
    return pl.pallas_call(
        functools.partial(_combine_kernel, post=post),
        grid=(nt,),
        in_specs=in_specs,
        out_specs=out_specs,
        out_shape=out_shape,
        scratch_shapes=[pltpu.VMEM((2, td, d), F32), pltpu.VMEM((2, td, d), F32),
                        pltpu.SemaphoreType.DMA((2,))],
        compiler_params=_cparams(1),
        name="moe_combine",
    )(*args)


MOE_ROWS = 1024
MOE_DOWN_SPLIT = 2


def _moe_layer(x, p, modr, st, layer, moe_idx, post, post_args):
    t, d = x.shape
    n_experts = p["router_w"].shape[-1]
    d_ff = p["moe_w_gate"].shape[-1]
    tm = MOE_ROWS
    hp, w1, w2, code, rank, cnt = _route(x, p["norm2_g"].reshape(-1, 1, d), modr,
                                         p["router_w"][moe_idx], st, layer, n_experts)
    counts = cnt[:n_experts, 0]
    padded = (counts + tm - 1) // tm * tm
    ends = jnp.cumsum(padded)
    starts = ends - padded
    n_tiles = pl.cdiv(2 * t, tm) + n_experts
    n_slots = n_tiles * tm
    dest = starts[:, None] + rank[:n_experts]
    slot0 = jnp.sum(jnp.where(code[:n_experts] == 1, dest, 0), axis=0).astype(jnp.int32)
    slot1 = jnp.sum(jnp.where(code[:n_experts] == 2, dest, 0), axis=0).astype(jnp.int32)
    tile_row = jnp.arange(n_tiles, dtype=jnp.int32) * tm
    tile_expert = jnp.minimum(jnp.sum(tile_row[:, None] >= ends[None, :], axis=1), n_experts - 1)
    wid = (moe_idx * n_experts + tile_expert).astype(jnp.int32)
    nvalid = (ends[-1:] // tm).astype(jnp.int32)
    expert_ids = jnp.arange(n_experts, dtype=jnp.int32)[:, None]
    token_ids = jnp.arange(t, dtype=jnp.int32)
    e0 = jnp.sum(jnp.where(code[:n_experts] == 1, expert_ids, 0), axis=0)
    e1 = jnp.sum(jnp.where(code[:n_experts] == 2, expert_ids, 0), axis=0)
    pair_token = jnp.sort(jnp.concatenate([e0 * t + token_ids, e1 * t + token_ids])) % t
    slot_expert = jnp.repeat(tile_expert, tm)
    local = jnp.arange(n_slots, dtype=jnp.int32) - starts[slot_expert]
    compact_start = jnp.cumsum(counts) - counts
    compact = compact_start[slot_expert] + jnp.clip(local, 0, jnp.maximum(counts[slot_expert] - 1, 0))
    slot_token = pair_token[jnp.clip(compact, 0, 2 * t - 1)].astype(jnp.int32)

    xs = _dispatch(hp, slot_token, d)
    wg = p["moe_w_gate"].reshape(-1, d, d_ff)
    wu = p["moe_w_up"].reshape(-1, d, d_ff)
    wd = p["moe_w_down"].reshape(-1, d_ff, d)
    hidden = _gmm(xs, (wg, wu), wid, nvalid, mode="swiglu", out_dtype=BF16, tm=tm,
                  tn=_pick(d_ff, 512, LANES), name="moe_gate_up")
    ys = _gmm(hidden, (wd,), jnp.repeat(wid, MOE_DOWN_SPLIT), nvalid * MOE_DOWN_SPLIT,
              mode="plain", out_dtype=F32, tm=tm // MOE_DOWN_SPLIT, tn=_pick(d, 512, LANES),
              name="moe_down")
    return _combine(ys, slot0, slot1, x, modr, w1, w2, st, layer, post, post_args)


def _dense_layer(x, p, modr, st, layer, ffn_idx):
    t, d = x.shape
    d_ff = p["ffn_w_gate"].shape[-1]
    h = _norm_mod(x, p["norm2_g"].reshape(-1, 1, d), modr, st, layer, 4, 3)
    hidden = _gmm(h, (p["ffn_w_gate"], p["ffn_w_up"]), ffn_idx, None, mode="swiglu",
                  out_dtype=BF16, tm=st.token_tile(2048), tn=_pick(d_ff, 512, LANES),
                  name="ffn_gate_up")
    tm = st.token_tile(512)
    return _gmm(hidden, (p["ffn_w_down"],), ffn_idx, None, mode="resid", out_dtype=F32, tm=tm,
                tn=_pick(d, 512, LANES), resid=(x, modr, _mod_index(st, layer, 5, tm)),
                name="ffn_down")


def _dft_radix(seq_len):
    r = 1
    while seq_len // r > 512 and r < 8:
        r *= 2
    return r


def kernel(x_prompt, x_sample, cache_k, cache_v, c, c_ctx, norm1_g, norm2_g, w_mod, b_mod, w_in,
           q_norm_g, k_norm_g, hy_conv_w, hy_conv_b, filt_w1, filt_b1, filt_w2, filt_b2, filt_w3,
           filt_log_decay, hy_skip, out_norm_g, w_out, ffn_w_gate, ffn_w_up, ffn_w_down, router_w,
           moe_w_gate, moe_w_up, moe_w_down, final_norm_g):
    p = dict(norm2_g=norm2_g, hy_conv_w=hy_conv_w, hy_conv_b=hy_conv_b, filt_w1=filt_w1,
             filt_b1=filt_b1, filt_w2=filt_w2, filt_b2=filt_b2, filt_w3=filt_w3,
             filt_log_decay=filt_log_decay, hy_skip=hy_skip, ffn_w_gate=ffn_w_gate,
             ffn_w_up=ffn_w_up, ffn_w_down=ffn_w_down, router_w=router_w, moe_w_gate=moe_w_gate,
             moe_w_up=moe_w_up, moe_w_down=moe_w_down)
    batch, seq, d = x_prompt.shape
    dec_batch, dec_seq, _ = x_sample.shape
    depth = w_in.shape[0]
    in_w = w_in.shape[-1]
    n_kv = cache_k.shape[3]
    kv_w = n_kv * HEAD_DIM
    a_w = (3 * d + 2 * kv_w - in_w) // 2
    hy_w = d - a_w
    u_col0 = a_w + 2 * kv_w
    past = cache_k.shape[2]
    st = _Stream(dec_batch, dec_seq, batch, seq)
    assert dec_batch + 1 <= MOD_ROWS

    x = jnp.concatenate([x_sample.reshape(st.ts, d), x_prompt.reshape(st.tc, d)], axis=0)
    cvecs = jnp.zeros((MOD_ROWS, d), F32).at[:dec_batch].set(c).at[dec_batch].set(c_ctx)
    modr = _mod_all(cvecs, w_mod, b_mod).reshape(depth * MOD_ROWS * 6, 1, d)
    cos2, sin2 = _rope_tables(st)
    dfts = {n: _Dft(n, _dft_radix(n)) for n in {dec_seq, seq}}
    norm1_g3 = norm1_g.reshape(depth, 1, d)
    qg3 = q_norm_g.reshape(depth, 1, HEAD_DIM)
    kg3 = k_norm_g.reshape(depth, 1, HEAD_DIM)
    tm = st.token_tile(1024)
    cache_k4 = cache_k.reshape(dec_batch, depth, past, kv_w)
    cache_v4 = cache_v.reshape(dec_batch, depth, past, kv_w)

    new_k, new_v = [], []
    h = y = None
    for layer in range(depth):
        if h is None:
            h = _norm_mod(x, norm1_g3, modr, st, layer, 1, 0)
        proj = _gmm(h, (w_in,), layer, None, mode="plain", out_dtype=BF16,
                    tm=st.token_tile(2048), tn=_pick(in_w, 768, LANES), name="w_in")
        qn, kn, kc = _qk_norm_rope(proj, qg3, kg3, cos2, sin2, layer, a_w, kv_w, st)
        new_k.append(kc[st.ts:].reshape(batch, seq, n_kv, HEAD_DIM))
        new_v.append(proj[st.ts:, a_w + kv_w:a_w + 2 * kv_w].astype(F32)
                     .reshape(batch, seq, n_kv, HEAD_DIM))

        attn_s = _attention(qn, kn, proj, a_w + kv_w, n_kv, dec_batch, dec_seq, 0,
                            cache=(cache_k4, cache_v4, layer))
        attn_c = _attention(qn, kn, proj, a_w + kv_w, n_kv, batch, seq, st.ts)
        hy_s, hy_c = _hyena(proj, layer, p, st, u_col0, hy_w, dfts)
        merged = _merge_norm(attn_s, attn_c, hy_s, hy_c, out_norm_g, layer, st)
        x = _gmm(merged, (w_out,), layer, None, mode="resid", out_dtype=F32, tm=tm,
                 tn=_pick(d, 1024, LANES), resid=(x, modr, _mod_index(st, layer, 2, tm)),
                 name="w_out")
        h = None
        if layer % 2 == 0:
            x = _dense_layer(x, p, modr, st, layer, layer // 2)
        elif layer + 1 < depth:
            x, h = _moe_layer(x, p, modr, st, layer, layer // 2, "norm_mod", (norm1_g3,))
        else:
            (y,) = _moe_layer(x, p, modr, st, layer, layer // 2, "final",
                              (final_norm_g.reshape(1, d),))
    if y is None:
        y = _final_norm(x, final_norm_g)
    y_sample = y[:st.ts].reshape(dec_batch, dec_seq, d)
    y_prompt = y[st.ts:].reshape(batch, seq, d)
    return (y_prompt, y_sample, jnp.stack(new_k, axis=1), jnp.stack(new_v, axis=1))
```

```python
import cmath
import functools
import math

import jax
import jax.numpy as jnp
import numpy as np
from jax import lax
from jax.experimental import pallas as pl
from jax.experimental.pallas import tpu as pltpu

F32 = jnp.float32
BF16 = jnp.bfloat16

HEAD_DIM = 128
GRID_W = 64
ROPE_THETA = 10000.0
EPS = 1e-6
POS_FREQS = 16
HY_ORDER = 2
LANES = 128
MOD_ROWS = 8
VMEM_LIMIT_BYTES = 56 * 1024 * 1024


def _cparams(n_axes):
    return pltpu.CompilerParams(dimension_semantics=("arbitrary",) * n_axes,
                                vmem_limit_bytes=VMEM_LIMIT_BYTES)


def _pick(n, pref, mult):
    best = None
    d = mult
    while d <= min(n, pref):
        if n % d == 0:
            best = d
        d += mult
    assert best is not None, (n, pref, mult)
    return best


def _roundup(x, m):
    return (x + m - 1) // m * m


def _mod_kernel(c_ref, w_ref, b_ref, o_ref):
    c = c_ref[...]
    s = (c * jax.nn.sigmoid(c)).astype(BF16)
    o_ref[...] = jnp.dot(s, w_ref[...].astype(BF16), preferred_element_type=F32) + b_ref[...]


def _mod_all(cvecs, w_mod, b_mod):
    depth, d, n = w_mod.shape
    tn = _pick(n, 1024, LANES)
    return pl.pallas_call(
        _mod_kernel,
        grid=(depth, n // tn),
        in_specs=[pl.BlockSpec((MOD_ROWS, d), lambda l, j: (0, 0)),
                  pl.BlockSpec((None, d, tn), lambda l, j: (l, 0, j)),
                  pl.BlockSpec((None, 1, tn), lambda l, j: (l, 0, j))],
        out_specs=pl.BlockSpec((None, MOD_ROWS, tn), lambda l, j: (l, 0, j)),
        out_shape=jax.ShapeDtypeStruct((depth, MOD_ROWS, n), F32),
        compiler_params=_cparams(2),
        name="mod_all",
    )(cvecs, w_mod, b_mod.reshape(depth, 1, n))


def _rms(x, g):
    return x * lax.rsqrt(jnp.mean(x * x, axis=-1, keepdims=True) + EPS) * g


def _norm_mod_kernel(x_ref, g_ref, sc_ref, sh_ref, o_ref):
    y = _rms(x_ref[...], g_ref[...])
    o_ref[...] = (y * (1.0 + sc_ref[...]) + sh_ref[...]).astype(o_ref.dtype)


def _norm_kernel(x_ref, g_ref, o_ref):
    o_ref[...] = _rms(x_ref[...], g_ref[...]).astype(o_ref.dtype)


class _Stream:
    def __init__(self, dec_batch, dec_seq, batch, seq):
        self.dec_batch, self.dec_seq, self.batch, self.seq = dec_batch, dec_seq, batch, seq
        self.ts = dec_batch * dec_seq
        self.tc = batch * seq
        self.t = self.ts + self.tc

    def group(self, row):
        return jnp.minimum(row // self.dec_seq, self.dec_batch)

    def token_tile(self, pref):
        return _pick(math.gcd(self.dec_seq, self.tc), pref, 16)


def _mod_index(st, layer, which, tm):
    def index(i):
        return ((layer * MOD_ROWS + st.group(i * tm)) * 6 + which, 0, 0)
    return index


def _norm_mod(x, g, modr, st, layer, which_sc, which_sh):
    t, d = x.shape
    tm = st.token_tile(512)
    return pl.pallas_call(
        _norm_mod_kernel,
        grid=(t // tm,),
        in_specs=[pl.BlockSpec((tm, d), lambda i: (i, 0)),
                  pl.BlockSpec((None, 1, d), lambda i: (layer, 0, 0)),
                  pl.BlockSpec((None, 1, d), _mod_index(st, layer, which_sc, tm)),
                  pl.BlockSpec((None, 1, d), _mod_index(st, layer, which_sh, tm))],
        out_specs=pl.BlockSpec((tm, d), lambda i: (i, 0)),
        out_shape=jax.ShapeDtypeStruct((t, d), BF16),
        compiler_params=_cparams(1),
        name="norm_mod",
    )(x, g, modr, modr)


def _final_norm(x, g):
    t, d = x.shape
    tm = _pick(t, 512, 8)
    return pl.pallas_call(
        _norm_kernel,
        grid=(t // tm,),
        in_specs=[pl.BlockSpec((tm, d), lambda i: (i, 0)),
                  pl.BlockSpec((1, d), lambda i: (0, 0))],
        out_specs=pl.BlockSpec((tm, d), lambda i: (i, 0)),
        out_shape=jax.ShapeDtypeStruct((t, d), F32),
        compiler_params=_cparams(1),
        name="final_norm",
    )(x, g.reshape(1, d))


def _gmm_kernel(wid_ref, nv_ref, *refs, mode):
    i = pl.program_id(1)
    if mode == "swiglu":
        a_ref, wg_ref, wu_ref, o_ref, wgb, wub = refs
        wpairs = ((wg_ref, wgb), (wu_ref, wub))
    elif mode == "resid":
        a_ref, w_ref, x_ref, gate_ref, o_ref, wb = refs
        wpairs = ((w_ref, wb),)
    else:
        a_ref, w_ref, o_ref, wb = refs
        wpairs = ((w_ref, wb),)

    prev = wid_ref[jnp.maximum(i - 1, 0)]
    new_w = jnp.logical_or(i == 0, wid_ref[i] != prev)

    @pl.when(jnp.logical_and(new_w, i < nv_ref[0]))
    def _cast():
        for src, dst in wpairs:
            dst[...] = src[...].astype(BF16)

    @pl.when(i >= nv_ref[0])
    def _unused_tile():
        o_ref[...] = jnp.zeros_like(o_ref)

    @pl.when(i < nv_ref[0])
    def _compute():
        a = a_ref[...]
        if mode == "swiglu":
            g = jnp.dot(a, wgb[...], preferred_element_type=F32)
            u = jnp.dot(a, wub[...], preferred_element_type=F32)
            o_ref[...] = (g * jax.nn.sigmoid(g) * u).astype(o_ref.dtype)
        elif mode == "resid":
            acc = jnp.dot(a, wb[...], preferred_element_type=F32)
            o_ref[...] = x_ref[...] + gate_ref[...] * acc
        else:
            o_ref[...] = jnp.dot(a, wb[...], preferred_element_type=F32).astype(o_ref.dtype)


def _gmm(a, ws, wid, nvalid, *, mode, out_dtype, tm, tn, resid=None, name="gmm"):
    m = a.shape[0]
    _, k, n = ws[0].shape
    ka = a.shape[1]
    nt = m // tm
    if isinstance(wid, int):
        wid, nvalid = jnp.full((nt,), wid, jnp.int32), jnp.full((1,), nt, jnp.int32)
    assert m % tm == 0 and n % tn == 0 and wid.shape == (nt,)

    def a_index(j, i, wid_ref, nv_ref):
        return (jnp.minimum(i, nv_ref[0] - 1), 0)

    def w_index(j, i, wid_ref, nv_ref):
        return (wid_ref[i], 0, j)

    def o_index(j, i, wid_ref, nv_ref):
        return (i, j)

    in_specs = [pl.BlockSpec((tm, ka), a_index)]
    in_specs += [pl.BlockSpec((None, k, tn), w_index) for _ in ws]
    args = [a, *ws]
    if mode == "resid":
        x, modr, mod_index = resid
        in_specs += [pl.BlockSpec((tm, tn), o_index),
                     pl.BlockSpec((None, 1, tn),
                                  lambda j, i, wid_ref, nv_ref: mod_index(i)[:2] + (j,))]
        args += [x, modr]
    grid_spec = pltpu.PrefetchScalarGridSpec(
        num_scalar_prefetch=2,
        grid=(n // tn, nt),
        in_specs=in_specs,
        out_specs=pl.BlockSpec((tm, tn), o_index),
        scratch_shapes=[pltpu.VMEM((k, tn), BF16) for _ in ws],
    )
    return pl.pallas_call(
        functools.partial(_gmm_kernel, mode=mode),
        grid_spec=grid_spec,
        out_shape=jax.ShapeDtypeStruct((m, n), out_dtype),
        compiler_params=_cparams(2),
        name=name,
    )(wid, nvalid, *args)


def _qk_kernel(q_ref, k_ref, qg_ref, kg_ref, cos_ref, sin_ref, qo_ref, ko_ref, kc_ref, *, scale):
    cos = cos_ref[...]
    sin = sin_ref[...]

    def head(x, g):
        xn = _rms(x.astype(F32), g)
        return xn, xn * cos + pltpu.roll(xn, HEAD_DIM // 2, 1) * sin

    for h in range(q_ref.shape[1] // HEAD_DIM):
        sl = slice(h * HEAD_DIM, (h + 1) * HEAD_DIM)
        _, roped = head(q_ref[:, sl], qg_ref[...])
        qo_ref[:, sl] = (roped * scale).astype(qo_ref.dtype)
    for h in range(k_ref.shape[1] // HEAD_DIM):
        sl = slice(h * HEAD_DIM, (h + 1) * HEAD_DIM)
        xn, roped = head(k_ref[:, sl], kg_ref[...])
        ko_ref[:, sl] = roped.astype(ko_ref.dtype)
        kc_ref[:, sl] = xn


def _qk_norm_rope(proj, qg, kg, cos2, sin2, layer, a_w, kv_w, st):
    t = proj.shape[0]
    tm = st.token_tile(512)
    assert a_w % kv_w == 0
    return pl.pallas_call(
        functools.partial(_qk_kernel, scale=1.0 / math.sqrt(HEAD_DIM)),
        grid=(t // tm,),
        in_specs=[pl.BlockSpec((tm, a_w), lambda i: (i, 0)),
                  pl.BlockSpec((tm, kv_w), lambda i: (i, a_w // kv_w)),
                  pl.BlockSpec((None, 1, HEAD_DIM), lambda i: (layer, 0, 0)),
                  pl.BlockSpec((None, 1, HEAD_DIM), lambda i: (layer, 0, 0)),
                  pl.BlockSpec((tm, HEAD_DIM), lambda i: (i, 0)),
                  pl.BlockSpec((tm, HEAD_DIM), lambda i: (i, 0))],
        out_specs=[pl.BlockSpec((tm, a_w), lambda i: (i, 0)),
                   pl.BlockSpec((tm, kv_w), lambda i: (i, 0)),
                   pl.BlockSpec((tm, kv_w), lambda i: (i, 0))],
        out_shape=[jax.ShapeDtypeStruct((t, a_w), BF16),
                   jax.ShapeDtypeStruct((t, kv_w), BF16),
                   jax.ShapeDtypeStruct((t, kv_w), F32)],
        compiler_params=_cparams(1),
        name="qk_norm_rope",
    )(proj, proj, qg, kg, cos2, sin2)


def _rope_tables(st):
    pos = np.arange(st.dec_seq)
    inv = ROPE_THETA ** (-np.arange(0, HEAD_DIM // 2, 2, dtype=np.float64) / (HEAD_DIM // 2))
    ang = np.concatenate([(pos // GRID_W)[:, None] * inv, (pos % GRID_W)[:, None] * inv], axis=-1)
    cos2 = np.concatenate([np.cos(ang), np.cos(ang)], axis=-1)
    sin2 = np.concatenate([-np.sin(ang), np.sin(ang)], axis=-1)
    cos_all = np.concatenate([np.tile(cos2, (st.dec_batch, 1)), np.ones((st.tc, HEAD_DIM))])
    sin_all = np.concatenate([np.tile(sin2, (st.dec_batch, 1)), np.zeros((st.tc, HEAD_DIM))])
    return jnp.asarray(cos_all, F32), jnp.asarray(sin_all, F32)


KEY_CHUNK = 1024


def _attn_kernel(*refs, g, has_cache):
    if has_cache:
        q_ref, ck_ref, cv_ref, k_ref, v_ref, o_ref = refs
    else:
        q_ref, k_ref, v_ref, o_ref = refs
    tq = q_ref.shape[0]
    q = jnp.concatenate([q_ref[:, h * HEAD_DIM:(h + 1) * HEAD_DIM] for h in range(g)], axis=0)
    lk = k_ref.shape[0]
    parts = []
    if has_cache:
        parts.append((ck_ref, cv_ref, 0, ck_ref.shape[0]))
    c0 = 0
    while c0 < lk:
        n = min(KEY_CHUNK, lk - c0)
        parts.append((k_ref, v_ref, c0, n))
        c0 += n
    m = acc = None
    for kr, vr, c0, n in parts:
        kc = kr[c0:c0 + n, :].astype(BF16)
        va = jnp.concatenate([vr[c0:c0 + n, :].astype(BF16), jnp.ones((n, HEAD_DIM), BF16)], axis=1)
        s = lax.dot_general(q, kc, (((1,), (1,)), ((), ())), preferred_element_type=F32)
        mc = jnp.max(s, axis=-1, keepdims=True)
        if m is None:
            m = mc
            acc = jnp.dot(jnp.exp(s - m).astype(BF16), va, preferred_element_type=F32)
        else:
            m_new = jnp.maximum(m, mc)
            acc = (jnp.exp(m - m_new) * acc
                   + jnp.dot(jnp.exp(s - m_new).astype(BF16), va, preferred_element_type=F32))
            m = m_new
    o = acc[:, :HEAD_DIM] / acc[:, HEAD_DIM:]
    for h in range(g):
        o_ref[:, h * HEAD_DIM:(h + 1) * HEAD_DIM] = o[h * tq:(h + 1) * tq].astype(o_ref.dtype)


def _attention(qn, kn, proj, v_col0, n_kv, nseq, seq_len, row0, cache=None):
    aw = qn.shape[1]
    g = aw // HEAD_DIM // n_kv
    tq = _pick(seq_len, 256, 16)
    nq = seq_len // tq
    assert row0 % seq_len == 0 and v_col0 % HEAD_DIM == 0
    in_specs = [pl.BlockSpec((tq, g * HEAD_DIM), lambda b, kv, i: (row0 // tq + b * nq + i, kv))]
    args = [qn]
    if cache is not None:
        ck, cv, layer = cache
        past = ck.shape[2]
        cspec = pl.BlockSpec((None, None, past, HEAD_DIM), lambda b, kv, i: (b, layer, 0, kv))
        in_specs += [cspec, cspec]
        args += [ck, cv]
    in_specs += [pl.BlockSpec((seq_len, HEAD_DIM), lambda b, kv, i: (row0 // seq_len + b, kv)),
                 pl.BlockSpec((seq_len, HEAD_DIM),
                              lambda b, kv, i: (row0 // seq_len + b, v_col0 // HEAD_DIM + kv))]
    args += [kn, proj]
    return pl.pallas_call(
        functools.partial(_attn_kernel, g=g, has_cache=cache is not None),
        grid=(nseq, n_kv, nq),
        in_specs=in_specs,
        out_specs=pl.BlockSpec((tq, g * HEAD_DIM), lambda b, kv, i: (b * nq + i, kv)),
        out_shape=jax.ShapeDtypeStruct((nseq * seq_len, aw), BF16),
        compiler_params=_cparams(3),
        name="attention",
    )(*args)


def _filter_kernel(feat_ref, w1_ref, b1_ref, w2_ref, b2_ref, w3_ref, ld_ref, o_ref, h_ref, *,
                   seq_len, hy_w):
    hp = lax.Precision.HIGHEST

    @pl.when(pl.program_id(1) == 0)
    def _hidden():
        h = jnp.sin(jnp.dot(feat_ref[...], w1_ref[...], precision=hp, preferred_element_type=F32)
                    + b1_ref[...])
        h_ref[...] = jnp.sin(jnp.dot(h, w2_ref[...], precision=hp, preferred_element_type=F32)
                             + b2_ref[...])

    h = jnp.dot(h_ref[...].astype(BF16), w3_ref[...].astype(BF16), preferred_element_type=F32)
    tl, tn = h.shape
    row = lax.broadcasted_iota(jnp.int32, (tl, 1), 0) + pl.program_id(0) * tl
    t = row.astype(F32) / seq_len
    h = h * jnp.exp(-jnp.exp(ld_ref[...]) * t)
    col = lax.broadcasted_iota(jnp.int32, (1, tn), 1) + pl.program_id(1) * tn
    backward = (col // hy_w) % 2 == 1
    o_ref[...] = jnp.where(row == 0, jnp.where(backward, 0.0, h), h).astype(o_ref.dtype)


def _filter_features(seq_len):
    t = np.arange(seq_len, dtype=np.float32) / np.float32(seq_len)
    fr = np.arange(1, POS_FREQS + 1, dtype=np.float32)
    ang = float(np.float32(2.0 * math.pi)) * t[:, None].astype(np.float64) * fr
    feat = np.concatenate([t[:, None], np.sin(ang), np.cos(ang)], axis=-1)
    out = np.zeros((seq_len, LANES), np.float32)
    out[:, :feat.shape[1]] = feat
    return jnp.asarray(out)


def _pad2(x, rows, cols):
    return jnp.pad(x, ((0, rows - x.shape[0]), (0, cols - x.shape[1])))


def _hyena_filters(seq_len, layer, p, hy_w):
    n = p["filt_w3"].shape[-1]
    feat = _filter_features(seq_len)
    w1 = _pad2(p["filt_w1"][layer], LANES, LANES)
    b1 = _pad2(p["filt_b1"][layer][None, :], 1, LANES)
    w2 = _pad2(p["filt_w2"][layer], LANES, LANES)
    b2 = _pad2(p["filt_b2"][layer][None, :], 1, LANES)
    w3 = _pad2(p["filt_w3"][layer], LANES, n)
    ld = p["filt_log_decay"][layer][None, :]
    tl = _pick(seq_len, 1024, 16)
    tn = _pick(n, 2048, LANES)
    return pl.pallas_call(
        functools.partial(_filter_kernel, seq_len=seq_len, hy_w=hy_w),
        grid=(seq_len // tl, n // tn),
        in_specs=[pl.BlockSpec((tl, LANES), lambda i, j: (i, 0)),
                  pl.BlockSpec((LANES, LANES), lambda i, j: (0, 0)),
                  pl.BlockSpec((1, LANES), lambda i, j: (0, 0)),
                  pl.BlockSpec((LANES, LANES), lambda i, j: (0, 0)),
                  pl.BlockSpec((1, LANES), lambda i, j: (0, 0)),
                  pl.BlockSpec((LANES, tn), lambda i, j: (0, j)),
                  pl.BlockSpec((1, tn), lambda i, j: (0, j))],
        out_specs=pl.BlockSpec((tl, tn), lambda i, j: (i, j)),
        out_shape=jax.ShapeDtypeStruct((seq_len, n), BF16),
        scratch_shapes=[pltpu.VMEM((tl, LANES), F32)],
        compiler_params=_cparams(2),
        name="hy_filter",
    )(feat, w1, b1, w2, b2, w3, ld)


class _Dft:
    def __init__(self, seq_len, radix):
        self.seq_len, self.r = seq_len, radix
        n = 2 * seq_len
        self.n1 = n // radix
        self.la = seq_len // radix
        k1 = self.n1 // 2 + 1
        self.k1p = _roundup(k1, 16)
        a = np.arange(self.la)[None, :]
        kk = np.arange(self.k1p)[:, None]
        valid = kk < k1
        th = 2.0 * math.pi * kk * a / self.n1
        fwd = np.concatenate([np.where(valid, np.cos(th), 0.0), np.where(valid, -np.sin(th), 0.0)])
        wgt = np.where((kk == 0) | (kk == self.n1 // 2), 1.0, 2.0) * valid / n
        inv = np.concatenate([wgt * np.cos(th), -wgt * np.sin(th)]).T
        self.fwd = jnp.asarray(fwd, F32)
        self.inv = jnp.asarray(inv, F32)
        ph = 2.0 * math.pi * kk / n
        ones = np.ones((1, LANES))
        self.tw_re = jnp.asarray(np.cos(ph) * ones, F32)
        self.tw_im = jnp.asarray(-np.sin(ph) * ones, F32)


def _cmul_const(x, w):
    wr, wi = w.real, w.imag
    if abs(wi) < 1e-12:
        if abs(wr - 1.0) < 1e-12:
            return x
        if abs(wr + 1.0) < 1e-12:
            return (-x[0], -x[1])
    if abs(wr) < 1e-12:
        if abs(wi - 1.0) < 1e-12:
            return (-x[1], x[0])
        if abs(wi + 1.0) < 1e-12:
            return (x[1], -x[0])
    return (x[0] * wr - x[1] * wi, x[0] * wi + x[1] * wr)


def _fft_list(xs, sign):
    n = len(xs)
    if n == 1:
        return xs
    ev = _fft_list(xs[0::2], sign)
    od = _fft_list(xs[1::2], sign)
    out = [None] * n
    for k in range(n // 2):
        t = _cmul_const(od[k], cmath.exp(sign * 2j * math.pi * k / n))
        out[k] = (ev[k][0] + t[0], ev[k][1] + t[1])
        out[k + n // 2] = (ev[k][0] - t[0], ev[k][1] - t[1])
    return out


ROW_CHUNK = 16
LONGCONV_ROWS = 1024


def _lane_tile(x, width):
    return x if width == LANES else jnp.concatenate([x] * (width // LANES), axis=1)


def _twiddles(tw_re_ref, tw_im_ref, rows, r, tc):
    if r == 1:
        return []
    w1 = (tw_re_ref[rows, :], tw_im_ref[rows, :])
    ws = [w1]
    for _ in range(r - 2):
        pr, pi = ws[-1]
        ws.append((pr * w1[0] - pi * w1[1], pr * w1[1] + pi * w1[0]))
    return [(_lane_tile(wr, tc), _lane_tile(wi, tc)) for wr, wi in ws]


def _spectrum_chunk(y_ref, tws, c0, k1p, r):
    rows = pl.ds(c0, ROW_CHUNK)
    rows_im = pl.ds(k1p + c0, ROW_CHUNK)
    ts = []
    for b in range(r):
        yr = y_ref[b, rows, :]
        yi = y_ref[b, rows_im, :]
        if b == 0:
            ts.append((yr, yi))
        else:
            wr, wi = tws[b - 1]
            ts.append((yr * wr - yi * wi, yr * wi + yi * wr))
    return _fft_list(ts, -1)


def _lanes(q):
    return slice(q * LANES, (q + 1) * LANES)


def _conv3_group(u_ref, w_ref, b_ref, q, seq):
    x = u_ref[seq, _lanes(q)].astype(F32)
    n = x.shape[0]
    row = lax.broadcasted_iota(jnp.int32, x.shape, 0)
    prev = jnp.where(row == 0, 0.0, pltpu.roll(x, 1, 0))
    nxt = jnp.where(row == n - 1, 0.0, pltpu.roll(x, n - 1, 0))
    w = w_ref[:, _lanes(q)]
    return prev * w[0:1] + x * w[1:2] + nxt * w[2:3] + b_ref[:, _lanes(q)]


def _stage1(f_ref, load_group, xf_ref, y_ref, r, la):
    nq = xf_ref.shape[0]
    for q in range(nq):
        xf_ref[q] = load_group(q)
    f = f_ref[...].astype(BF16)
    for b in range(r):
        xb = jnp.concatenate([xf_ref[q, pl.ds(b, la, stride=r), :] for q in range(nq)], axis=1)
        y_ref[b] = jnp.dot(f, xb.astype(BF16), preferred_element_type=F32)


def _spectrum_kernel(*refs, r, k1p, la):
    if r > 1:
        f_ref, tw_re_ref, tw_im_ref, hf_ref, hb_ref, kr_ref, ki_ref, xf_ref, y_ref, acc_ref = refs
    else:
        f_ref, hf_ref, hb_ref, kr_ref, ki_ref, xf_ref, y_ref, acc_ref = refs
        tw_re_ref = tw_im_ref = None
    tc = y_ref.shape[-1]

    def sweep(src_ref, backward):
        _stage1(f_ref, lambda q: src_ref[:, _lanes(q)].astype(F32), xf_ref, y_ref, r, la)

        def chunk(c, carry):
            c0 = pl.multiple_of(c * ROW_CHUNK, ROW_CHUNK)
            rows = pl.ds(c0, ROW_CHUNK)
            rows_im = pl.ds(k1p + c0, ROW_CHUNK)
            x = _spectrum_chunk(y_ref, _twiddles(tw_re_ref, tw_im_ref, rows, r, tc), c0, k1p, r)
            for k2 in range(r):
                if backward:
                    kr_ref[k2, rows, :] = (acc_ref[k2, rows, :] + x[k2][0]).astype(kr_ref.dtype)
                    ki_ref[k2, rows, :] = (acc_ref[k2, rows_im, :] - x[k2][1]).astype(ki_ref.dtype)
                else:
                    acc_ref[k2, rows, :] = x[k2][0]
                    acc_ref[k2, rows_im, :] = x[k2][1]
            return carry

        lax.fori_loop(0, k1p // ROW_CHUNK, chunk, 0)

    sweep(hf_ref, False)
    sweep(hb_ref, True)


def _filter_spectrum(taps, dft, hy_w):
    r, la, k1p, seq_len = dft.r, dft.la, dft.k1p, dft.seq_len
    tc = _pick(hy_w, 256, LANES)
    nct = hy_w // tc

    def tap_spec(direction):
        return pl.BlockSpec((seq_len, tc), lambda o, j: (0, (o * 2 + direction) * nct + j))

    const = [pl.BlockSpec(dft.fwd.shape, lambda o, j: (0, 0))]
    cargs = [dft.fwd]
    if r > 1:
        const += [pl.BlockSpec(dft.tw_re.shape, lambda o, j: (0, 0))] * 2
        cargs += [dft.tw_re, dft.tw_im]
    out_spec = pl.BlockSpec((None, r, k1p, tc), lambda o, j: (o, 0, 0, j))
    shape = jax.ShapeDtypeStruct((HY_ORDER, r, k1p, hy_w), BF16)
    return pl.pallas_call(
        functools.partial(_spectrum_kernel, r=r, k1p=k1p, la=la),
        grid=(HY_ORDER, nct),
        in_specs=const + [tap_spec(0), tap_spec(1)],
        out_specs=[out_spec, out_spec],
        out_shape=[shape, shape],
        scratch_shapes=[pltpu.VMEM((tc // LANES, seq_len, LANES), F32),
                        pltpu.VMEM((r, 2 * k1p, tc), F32),
                        pltpu.VMEM((r, 2 * k1p, tc), F32)],
        compiler_params=_cparams(2),
        name="hy_filter_spectrum",
    )(*cargs, taps, taps)


def _longconv_kernel(*refs, n_sub, **kw):
    seq_len = refs[-3].shape[1]
    for sub in range(n_sub):
        _longconv_one(refs, slice(sub * seq_len, (sub + 1) * seq_len), **kw)


def _longconv_one(refs, seq, *, r, k1p, la, z_conv):
    refs = list(refs)
    f_ref, fi_ref = refs[:2]
    del refs[:2]
    tw_re_ref = tw_im_ref = None
    if r > 1:
        tw_re_ref, tw_im_ref = refs[:2]
        del refs[:2]
    z_ref = refs.pop(0)
    if z_conv:
        zw_ref, zb_ref = refs[:2]
        del refs[:2]
    gate_ref, gw_ref, gb_ref, skip_ref, kr_ref, ki_ref, o_ref, xf_ref, y_ref, g_ref = refs
    tc = y_ref.shape[-1]
    if z_conv:
        load_z = lambda q: _conv3_group(z_ref, zw_ref, zb_ref, q, seq)
    else:
        load_z = lambda q: z_ref[seq, _lanes(q)].astype(F32)
    _stage1(f_ref, load_z, xf_ref, y_ref, r, la)

    def chunk(c, carry):
        c0 = pl.multiple_of(c * ROW_CHUNK, ROW_CHUNK)
        rows = pl.ds(c0, ROW_CHUNK)
        rows_im = pl.ds(k1p + c0, ROW_CHUNK)
        tws = _twiddles(tw_re_ref, tw_im_ref, rows, r, tc)
        x = _spectrum_chunk(y_ref, tws, c0, k1p, r)
        prod = []
        for k2 in range(r):
            kr = kr_ref[k2, rows, :].astype(F32)
            ki = ki_ref[k2, rows, :].astype(F32)
            prod.append((x[k2][0] * kr - x[k2][1] * ki, x[k2][0] * ki + x[k2][1] * kr))
        s = _fft_list(prod, +1)
        for b in range(r):
            if b == 0:
                gr, gi = s[0]
            else:
                wr, wi = tws[b - 1]
                gr = s[b][0] * wr + s[b][1] * wi
                gi = s[b][1] * wr - s[b][0] * wi
            g_ref[b, rows, :] = gr.astype(g_ref.dtype)
            g_ref[b, rows_im, :] = gi.astype(g_ref.dtype)
        return carry

    lax.fori_loop(0, k1p // ROW_CHUNK, chunk, 0)
    fi = fi_ref[...].astype(BF16)
    for b in range(r):
        yb = jnp.dot(fi, g_ref[b], preferred_element_type=F32)
        rows = pl.ds(b, la, stride=r)
        for q in range(xf_ref.shape[0]):
            xf_ref[q, rows, :] = yb[:, _lanes(q)] + skip_ref[:, _lanes(q)] * xf_ref[q, rows, :]
    for q in range(xf_ref.shape[0]):
        gate = _conv3_group(gate_ref, gw_ref, gb_ref, q, seq)
        o_ref[seq, _lanes(q)] = (gate * xf_ref[q]).astype(o_ref.dtype)


def _longconv(proj, u_col0, row0, layer, conv_w, conv_b, z_prev, order, skip, kf_re, kf_im, dft,
              nseq, hy_w):
    r, la, k1p, seq_len = dft.r, dft.la, dft.k1p, dft.seq_len
    tc = _pick(math.gcd(hy_w, u_col0), 256 if r > 1 else 1024, LANES)
    n_sub = _pick(math.gcd(nseq, row0 // seq_len) or nseq, max(1, LONGCONV_ROWS // seq_len), 1)
    blk = n_sub * seq_len
    assert row0 % blk == 0
    nct = hy_w // tc
    seq0 = row0 // blk
    z_conv = z_prev is None

    def u_specs(part):
        return [pl.BlockSpec((blk, tc), lambda s, j: (seq0 + s, u_col0 // tc + part * nct + j)),
                pl.BlockSpec((None, 3, tc), lambda s, j: (layer, 0, part * nct + j)),
                pl.BlockSpec((None, 1, tc), lambda s, j: (layer, 0, part * nct + j))]

    cdim = lambda s, j: (0, 0)
    in_specs = [pl.BlockSpec(dft.fwd.shape, cdim), pl.BlockSpec(dft.inv.shape, cdim)]
    args = [dft.fwd, dft.inv]
    if r > 1:
        in_specs += [pl.BlockSpec(dft.tw_re.shape, cdim)] * 2
        args += [dft.tw_re, dft.tw_im]
    if z_conv:
        in_specs += u_specs(0)
        args += [proj, conv_w, conv_b]
    else:
        in_specs.append(pl.BlockSpec((blk, tc), lambda s, j: (s, j)))
        args.append(z_prev)
    in_specs += u_specs(1 + order)
    args += [proj, conv_w, conv_b]
    in_specs += [pl.BlockSpec((None, 1, tc), lambda s, j: (layer * HY_ORDER + order, 0, j)),
                 pl.BlockSpec((None, r, k1p, tc), lambda s, j: (order, 0, 0, j)),
                 pl.BlockSpec((None, r, k1p, tc), lambda s, j: (order, 0, 0, j))]
    args += [skip, kf_re, kf_im]
    return pl.pallas_call(
        functools.partial(_longconv_kernel, n_sub=n_sub, r=r, k1p=k1p, la=la, z_conv=z_conv),
        grid=(nseq // n_sub, nct),
        in_specs=in_specs,
        out_specs=pl.BlockSpec((blk, tc), lambda s, j: (s, j)),
        out_shape=jax.ShapeDtypeStruct((nseq * seq_len, hy_w), BF16),
        scratch_shapes=[pltpu.VMEM((tc // LANES, seq_len, LANES), F32),
                        pltpu.VMEM((r, 2 * k1p, tc), F32),
                        pltpu.VMEM((r, 2 * k1p, tc), BF16)],
        compiler_params=_cparams(2),
        name="hy_longconv",
    )(*args)


def _hyena(proj, layer, p, st, u_col0, hy_w, dfts):
    skip = p["hy_skip"].reshape(-1, 1, hy_w)
    conv_w = p["hy_conv_w"]
    conv_b = p["hy_conv_b"].reshape(-1, 1, 3 * hy_w)
    outs = []
    for nseq, seq_len, row0 in ((st.dec_batch, st.dec_seq, 0), (st.batch, st.seq, st.ts)):
        dft = dfts[seq_len]
        taps = _hyena_filters(seq_len, layer, p, hy_w)
        kf_re, kf_im = _filter_spectrum(taps, dft, hy_w)
        z = None
        for order in range(HY_ORDER):
            z = _longconv(proj, u_col0, row0, layer, conv_w, conv_b, z, order, skip, kf_re, kf_im,
                          dft, nseq, hy_w)
        outs.append(z)
    return outs


def _merge_kernel(as_ref, ac_ref, hs_ref, hc_ref, ga_ref, gh_ref, o_ref, *, n_sample_tiles):
    aw = as_ref.shape[1]
    sample = pl.program_id(0) < n_sample_tiles
    a = jnp.where(sample, as_ref[...], ac_ref[...]).astype(F32)
    h = jnp.where(sample, hs_ref[...], hc_ref[...]).astype(F32)
    o_ref[:, :aw] = _rms(a, ga_ref[...]).astype(o_ref.dtype)
    o_ref[:, aw:] = _rms(h, gh_ref[...]).astype(o_ref.dtype)


def _merge_norm(attn_s, attn_c, hy_s, hy_c, g, layer, st):
    aw, hw = attn_s.shape[1], hy_s.shape[1]
    assert aw == hw
    tm = st.token_tile(512)
    ns = st.ts // tm
    g3 = g.reshape(g.shape[0], 1, aw + hw)
    s_idx = lambda i: (jnp.minimum(i, ns - 1), 0)
    c_idx = lambda i: (jnp.maximum(i - ns, 0), 0)
    return pl.pallas_call(
        functools.partial(_merge_kernel, n_sample_tiles=ns),
        grid=(st.t // tm,),
        in_specs=[pl.BlockSpec((tm, aw), s_idx), pl.BlockSpec((tm, aw), c_idx),
                  pl.BlockSpec((tm, hw), s_idx), pl.BlockSpec((tm, hw), c_idx),
                  pl.BlockSpec((None, 1, aw), lambda i: (layer, 0, 0)),
                  pl.BlockSpec((None, 1, hw), lambda i: (layer, 0, 1))],
        out_specs=pl.BlockSpec((tm, aw + hw), lambda i: (i, 0)),
        out_shape=jax.ShapeDtypeStruct((st.t, aw + hw), BF16),
        compiler_params=_cparams(1),
        name="merge_norm",
    )(attn_s, attn_c, hy_s, hy_c, g3, g3)


def _route_kernel(x_ref, g_ref, sc_ref, sh_ref, rw_ref, hp_ref, w1_ref, w2_ref, code_ref,
                  rank_ref, cnt_ref, carry_ref, *, n_experts):
    i = pl.program_id(0)

    @pl.when(i == 0)
    def _init():
        carry_ref[...] = jnp.zeros_like(carry_ref)

    h = _rms(x_ref[...], g_ref[...]) * (1.0 + sc_ref[...]) + sh_ref[...]
    hp_ref[...] = h

    logits = jnp.dot(h, rw_ref[...], precision=lax.Precision.HIGHEST, preferred_element_type=F32)
    lane = lax.broadcasted_iota(jnp.int32, logits.shape, 1)
    neg = jnp.float32(-jnp.inf)
    logits = jnp.where(lane < n_experts, logits, neg)
    m1 = jnp.max(logits, axis=-1, keepdims=True)
    i1 = jnp.min(jnp.where(logits == m1, lane, LANES), axis=-1, keepdims=True)
    rest = jnp.where(lane == i1, neg, logits)
    m2 = jnp.max(rest, axis=-1, keepdims=True)
    i2 = jnp.min(jnp.where(rest == m2, lane, LANES), axis=-1, keepdims=True)
    e2 = jnp.exp(m2 - m1)
    w1 = 1.0 / (1.0 + e2)
    w2 = e2 / (1.0 + e2)
    w1_ref[...] = jnp.broadcast_to(w1, w1_ref.shape)
    w2_ref[...] = jnp.broadcast_to(w2, w2_ref.shape)

    code = jnp.where(lane == i1, 1.0, jnp.where(lane == i2, 2.0, 0.0))
    code_t = code.T[:code_ref.shape[0], :]
    code_ref[...] = code_t.astype(jnp.int32)
    sel = (code_t > 0.0).astype(BF16)
    tm = sel.shape[1]
    upper = (lax.broadcasted_iota(jnp.int32, (tm, tm), 0)
             < lax.broadcasted_iota(jnp.int32, (tm, tm), 1)).astype(BF16)
    excl = jnp.dot(sel, upper, preferred_element_type=F32)
    carry = carry_ref[...]
    rank_ref[...] = (excl + carry[:, :1]).astype(jnp.int32)
    carry = carry + jnp.sum(sel.astype(F32), axis=-1, keepdims=True)
    carry_ref[...] = carry
    cnt_ref[...] = carry.astype(jnp.int32)


def _route(x, g, modr, rw, st, layer, n_experts):
    t, d = x.shape
    tm = st.token_tile(512)
    assert tm % LANES == 0 and n_experts <= 8
    rwp = _pad2(rw, d, LANES)
    row_spec = pl.BlockSpec((tm, LANES), lambda i: (i, 0))
    col_spec = pl.BlockSpec((8, tm), lambda i: (0, i))
    return pl.pallas_call(
        functools.partial(_route_kernel, n_experts=n_experts),
        grid=(t // tm,),
        in_specs=[pl.BlockSpec((tm, d), lambda i: (i, 0)),
                  pl.BlockSpec((None, 1, d), lambda i: (layer, 0, 0)),
                  pl.BlockSpec((None, 1, d), _mod_index(st, layer, 4, tm)),
                  pl.BlockSpec((None, 1, d), _mod_index(st, layer, 3, tm)),
                  pl.BlockSpec((d, LANES), lambda i: (0, 0))],
        out_specs=[pl.BlockSpec((tm, d), lambda i: (i, 0)), row_spec, row_spec,
                   col_spec, col_spec, pl.BlockSpec((8, LANES), lambda i: (0, 0))],
        out_shape=[jax.ShapeDtypeStruct((t, d), F32),
                   jax.ShapeDtypeStruct((t, LANES), F32),
                   jax.ShapeDtypeStruct((t, LANES), F32),
                   jax.ShapeDtypeStruct((8, t), jnp.int32),
                   jax.ShapeDtypeStruct((8, t), jnp.int32),
                   jax.ShapeDtypeStruct((8, LANES), jnp.int32)],
        scratch_shapes=[pltpu.VMEM((8, LANES), F32)],
        compiler_params=_cparams(1),
        name="moe_route",
    )(x, g, modr, modr, rwp)


DISPATCH_ROWS = 512


GATHER_UNROLL = 8


def _issue_row_gather(idx_ref, src_ref, dst_ref, sem):
    def trip(g, carry):
        for u in range(GATHER_UNROLL):
            rr = g * GATHER_UNROLL + u
            pltpu.make_async_copy(src_ref.at[pl.ds(idx_ref[0, rr], 1)], dst_ref.at[pl.ds(rr, 1)],
                                  sem).start(priority=u % 2)
        return carry

    lax.fori_loop(0, dst_ref.shape[0] // GATHER_UNROLL, trip, 0)


def _wait_row_gather(dst_ref, sem):
    pltpu.make_async_copy(dst_ref, dst_ref, sem).wait()


def _pipelined_gather(idx_refs, next_idx_refs, src_ref, bufs, sems, n_steps):
    i = pl.program_id(0)
    cur = i % 2

    @pl.when(jnp.logical_and(i == 0, n_steps > 0))
    def _first():
        for idx_ref, buf in zip(idx_refs, bufs):
            _issue_row_gather(idx_ref, src_ref, buf.at[0], sems.at[0])

    @pl.when(i + 1 < n_steps)
    def _prefetch():
        for idx_ref, buf in zip(next_idx_refs, bufs):
            _issue_row_gather(idx_ref, src_ref, buf.at[1 - cur], sems.at[1 - cur])

    @pl.when(i < n_steps)
    def _wait():
        for buf in bufs:
            _wait_row_gather(buf.at[cur], sems.at[cur])

    return cur


def _dispatch_kernel(nv_ref, tok_ref, tok_next_ref, h_ref, xs_ref, rows_ref, sems):
    n_steps = nv_ref[0]
    cur = _pipelined_gather([tok_ref], [tok_next_ref], h_ref, [rows_ref], sems, n_steps)

    @pl.when(pl.program_id(0) < n_steps)
    def _used():
        xs_ref[...] = rows_ref[cur].astype(xs_ref.dtype)

    @pl.when(pl.program_id(0) >= n_steps)
    def _unused():
        xs_ref[...] = jnp.zeros_like(xs_ref)


def _index_specs(nt, td):
    return [pl.BlockSpec((None, 1, td), lambda i, *_: (i, 0, 0), memory_space=pltpu.SMEM),
            pl.BlockSpec((None, 1, td), lambda i, *_: (jnp.minimum(i + 1, nt - 1), 0, 0),
                         memory_space=pltpu.SMEM)]


def _dispatch(h, slot_token, n_used_slots):
    n_slots = slot_token.shape[0]
    d = h.shape[1]
    td = _pick(n_slots, DISPATCH_ROWS, LANES)
    nt = n_slots // td
    tok = slot_token.reshape(nt, 1, td)
    grid_spec = pltpu.PrefetchScalarGridSpec(
        num_scalar_prefetch=1,
        grid=(nt,),
        in_specs=_index_specs(nt, td) + [pl.BlockSpec(memory_space=pl.ANY)],
        out_specs=pl.BlockSpec((td, d), lambda i, nv: (i, 0)),
        scratch_shapes=[pltpu.VMEM((2, td, d), F32), pltpu.SemaphoreType.DMA((2,))],
    )
    return pl.pallas_call(
        _dispatch_kernel,
        grid_spec=grid_spec,
        out_shape=jax.ShapeDtypeStruct((n_slots, d), BF16),
        compiler_params=_cparams(1),
        name="moe_dispatch",
    )((n_used_slots + td - 1) // td, tok, tok, h)


def _combine_kernel(*refs, post):
    (s0_ref, s0n_ref, s1_ref, s1n_ref, ys_ref, x_ref, gate_ref, w1_ref, w2_ref) = refs[:9]
    rest = refs[9:]
    n_post = {"none": 0, "norm_mod": 3, "final": 1}[post]
    post_refs, rest = rest[:n_post], rest[n_post:]
    n_out = 1 if post in ("none", "final") else 2
    outs, (ya, yb, sems) = rest[:n_out], rest[n_out:]
    cur = _pipelined_gather([s0_ref, s1_ref], [s0n_ref, s1n_ref], ys_ref, [ya, yb], sems,
                            pl.num_programs(0))
    w1 = _lane_tile(w1_ref[...], ya.shape[-1])
    w2 = _lane_tile(w2_ref[...], ya.shape[-1])
    xn = x_ref[...] + gate_ref[...] * (w1 * ya[cur] + w2 * yb[cur])
    if post == "final":
        outs[0][...] = _rms(xn, post_refs[0][...])
    else:
        outs[0][...] = xn
    if post == "norm_mod":
        g_ref, sc_ref, sh_ref = post_refs
        outs[1][...] = (_rms(xn, g_ref[...]) * (1.0 + sc_ref[...]) + sh_ref[...]).astype(outs[1].dtype)


def _combine(ys, slot0, slot1, x, modr, w1, w2, st, layer, post, post_args=()):
    t, d = x.shape
    td = _pick(math.gcd(st.dec_seq, st.tc), DISPATCH_ROWS, LANES)
    nt = t // td
    row = pl.BlockSpec((td, d), lambda i: (i, 0))
    in_specs = (_index_specs(nt, td) + _index_specs(nt, td)
                + [pl.BlockSpec(memory_space=pl.ANY), row,
                   pl.BlockSpec((None, 1, d), _mod_index(st, layer, 5, td)),
                   pl.BlockSpec((td, LANES), lambda i: (i, 0)),
                   pl.BlockSpec((td, LANES), lambda i: (i, 0))])
    s0 = slot0.reshape(nt, 1, td)
    s1 = slot1.reshape(nt, 1, td)
    args = [s0, s0, s1, s1, ys, x, modr, w1, w2]
    out_specs, out_shape = [row], [jax.ShapeDtypeStruct((t, d), F32)]
    if post == "norm_mod":
        in_specs += [pl.BlockSpec((None, 1, d), lambda i: (layer + 1, 0, 0)),
                     pl.BlockSpec((None, 1, d), _mod_index(st, layer + 1, 1, td)),
                     pl.BlockSpec((None, 1, d), _mod_index(st, layer + 1, 0, td))]
        args += [post_args[0], modr, modr]
        out_specs.append(row)
        out_shape.append(jax.ShapeDtypeStruct((t, d), BF16))
    elif post == "final":
        in_specs.append(pl.BlockSpec((1, d), lambda i: (0, 0)))
        args.append(post_args[0])
    return pl.pallas_call(
        functools.partial(_combine_kernel, post=post),
        grid=(nt,),
        in_specs=in_specs,
        out_specs=out_specs,
        out_shape=out_shape,
        scratch_shapes=[pltpu.VMEM((2, td, d), F32), pltpu.VMEM((2, td, d), F32),
                        pltpu.SemaphoreType.DMA((2,))],
        compiler_params=_cparams(1),
        name="moe_combine",
    )(*args)


MOE_ROWS = 1024
MOE_DOWN_SPLIT = 2


def _moe_layer(x, p, modr, st, layer, moe_idx, post, post_args):
    t, d = x.shape
    n_experts = p["router_w"].shape[-1]
    d_ff = p["moe_w_gate"].shape[-1]
    tm = MOE_ROWS
    hp, w1, w2, code, rank, cnt = _route(x, p["norm2_g"].reshape(-1, 1, d), modr,
                                         p["router_w"][moe_idx], st, layer, n_experts)
    counts = cnt[:n_experts, 0]
    padded = (counts + tm - 1) // tm * tm
    ends = jnp.cumsum(padded)
    starts = ends - padded
    n_tiles = pl.cdiv(2 * t, tm) + n_experts
    n_slots = n_tiles * tm
    dest = starts[:, None] + rank[:n_experts]
    slot0 = jnp.sum(jnp.where(code[:n_experts] == 1, dest, 0), axis=0).astype(jnp.int32)
    slot1 = jnp.sum(jnp.where(code[:n_experts] == 2, dest, 0), axis=0).astype(jnp.int32)
    tile_row = jnp.arange(n_tiles, dtype=jnp.int32) * tm
    tile_expert = jnp.minimum(jnp.sum(tile_row[:, None] >= ends[None, :], axis=1), n_experts - 1)
    wid = (moe_idx * n_experts + tile_expert).astype(jnp.int32)
    nvalid = (ends[-1:] // tm).astype(jnp.int32)
    expert_ids = jnp.arange(n_experts, dtype=jnp.int32)[:, None]
    token_ids = jnp.arange(t, dtype=jnp.int32)
    e0 = jnp.sum(jnp.where(code[:n_experts] == 1, expert_ids, 0), axis=0)
    e1 = jnp.sum(jnp.where(code[:n_experts] == 2, expert_ids, 0), axis=0)
    pair_token = jnp.sort(jnp.concatenate([e0 * t + token_ids, e1 * t + token_ids])) % t
    slot_expert = jnp.repeat(tile_expert, tm)
    local = jnp.arange(n_slots, dtype=jnp.int32) - starts[slot_expert]
    compact_start = jnp.cumsum(counts) - counts
    compact = compact_start[slot_expert] + jnp.clip(local, 0, jnp.maximum(counts[slot_expert] - 1, 0))
    slot_token = pair_token[jnp.clip(compact, 0, 2 * t - 1)].astype(jnp.int32)

    xs = _dispatch(hp, slot_token, ends[-1:].astype(jnp.int32))
    wg = p["moe_w_gate"].reshape(-1, d, d_ff)
    wu = p["moe_w_up"].reshape(-1, d, d_ff)
    wd = p["moe_w_down"].reshape(-1, d_ff, d)
    hidden = _gmm(xs, (wg, wu), wid, nvalid, mode="swiglu", out_dtype=BF16, tm=tm,
                  tn=_pick(d_ff, 512, LANES), name="moe_gate_up")
    ys = _gmm(hidden, (wd,), jnp.repeat(wid, MOE_DOWN_SPLIT), nvalid * MOE_DOWN_SPLIT,
              mode="plain", out_dtype=F32, tm=tm // MOE_DOWN_SPLIT, tn=_pick(d, 512, LANES),
              name="moe_down")
    return _combine(ys, slot0, slot1, x, modr, w1, w2, st, layer, post, post_args)


def _dense_layer(x, p, modr, st, layer, ffn_idx):
    t, d = x.shape
    d_ff = p["ffn_w_gate"].shape[-1]
    h = _norm_mod(x, p["norm2_g"].reshape(-1, 1, d), modr, st, layer, 4, 3)
    hidden = _gmm(h, (p["ffn_w_gate"], p["ffn_w_up"]), ffn_idx, None, mode="swiglu",
                  out_dtype=BF16, tm=st.token_tile(1024), tn=_pick(d_ff, 512, LANES),
                  name="ffn_gate_up")
    tm = st.token_tile(512)
    return _gmm(hidden, (p["ffn_w_down"],), ffn_idx, None, mode="resid", out_dtype=F32, tm=tm,
                tn=_pick(d, 512, LANES), resid=(x, modr, _mod_index(st, layer, 5, tm)),
                name="ffn_down")


def _dft_radix(seq_len):
    r = 1
    while seq_len // r > 512 and r < 8:
        r *= 2
    return r


def kernel(x_prompt, x_sample, cache_k, cache_v, c, c_ctx, norm1_g, norm2_g, w_mod, b_mod, w_in,
           q_norm_g, k_norm_g, hy_conv_w, hy_conv_b, filt_w1, filt_b1, filt_w2, filt_b2, filt_w3,
           filt_log_decay, hy_skip, out_norm_g, w_out, ffn_w_gate, ffn_w_up, ffn_w_down, router_w,
           moe_w_gate, moe_w_up, moe_w_down, final_norm_g):
    p = dict(norm2_g=norm2_g, hy_conv_w=hy_conv_w, hy_conv_b=hy_conv_b, filt_w1=filt_w1,
             filt_b1=filt_b1, filt_w2=filt_w2, filt_b2=filt_b2, filt_w3=filt_w3,
             filt_log_decay=filt_log_decay, hy_skip=hy_skip, ffn_w_gate=ffn_w_gate,
             ffn_w_up=ffn_w_up, ffn_w_down=ffn_w_down, router_w=router_w, moe_w_gate=moe_w_gate,
             moe_w_up=moe_w_up, moe_w_down=moe_w_down)
    batch, seq, d = x_prompt.shape
    dec_batch, dec_seq, _ = x_sample.shape
    depth = w_in.shape[0]
    in_w = w_in.shape[-1]
    n_kv = cache_k.shape[3]
    kv_w = n_kv * HEAD_DIM
    a_w = (3 * d + 2 * kv_w - in_w) // 2
    hy_w = d - a_w
    u_col0 = a_w + 2 * kv_w
    past = cache_k.shape[2]
    st = _Stream(dec_batch, dec_seq, batch, seq)
    assert dec_batch + 1 <= MOD_ROWS

    x = jnp.concatenate([x_sample.reshape(st.ts, d), x_prompt.reshape(st.tc, d)], axis=0)
    cvecs = jnp.zeros((MOD_ROWS, d), F32).at[:dec_batch].set(c).at[dec_batch].set(c_ctx)
    modr = _mod_all(cvecs, w_mod, b_mod).reshape(depth * MOD_ROWS * 6, 1, d)
    cos2, sin2 = _rope_tables(st)
    dfts = {n: _Dft(n, _dft_radix(n)) for n in {dec_seq, seq}}
    norm1_g3 = norm1_g.reshape(depth, 1, d)
    qg3 = q_norm_g.reshape(depth, 1, HEAD_DIM)
    kg3 = k_norm_g.reshape(depth, 1, HEAD_DIM)
    tm = st.token_tile(1024)
    cache_k4 = cache_k.reshape(dec_batch, depth, past, kv_w)
    cache_v4 = cache_v.reshape(dec_batch, depth, past, kv_w)

    new_k, new_v = [], []
    h = y = None
    for layer in range(depth):
        if h is None:
            h = _norm_mod(x, norm1_g3, modr, st, layer, 1, 0)
        proj = _gmm(h, (w_in,), layer, None, mode="plain", out_dtype=BF16,
                    tm=st.token_tile(2048), tn=_pick(in_w, 768, LANES), name="w_in")
        qn, kn, kc = _qk_norm_rope(proj, qg3, kg3, cos2, sin2, layer, a_w, kv_w, st)
        new_k.append(kc[st.ts:].reshape(batch, seq, n_kv, HEAD_DIM))
        new_v.append(proj[st.ts:, a_w + kv_w:a_w + 2 * kv_w].astype(F32)
                     .reshape(batch, seq, n_kv, HEAD_DIM))

        attn_s = _attention(qn, kn, proj, a_w + kv_w, n_kv, dec_batch, dec_seq, 0,
                            cache=(cache_k4, cache_v4, layer))
        attn_c = _attention(qn, kn, proj, a_w + kv_w, n_kv, batch, seq, st.ts)
        hy_s, hy_c = _hyena(proj, layer, p, st, u_col0, hy_w, dfts)
        merged = _merge_norm(attn_s, attn_c, hy_s, hy_c, out_norm_g, layer, st)
        x = _gmm(merged, (w_out,), layer, None, mode="resid", out_dtype=F32, tm=tm,
                 tn=_pick(d, 1024, LANES), resid=(x, modr, _mod_index(st, layer, 2, tm)),
                 name="w_out")
        h = None
        if layer % 2 == 0:
            x = _dense_layer(x, p, modr, st, layer, layer // 2)
        elif layer + 1 < depth:
            x, h = _moe_layer(x, p, modr, st, layer, layer // 2, "norm_mod", (norm1_g3,))
        else:
            (y,) = _moe_layer(x, p, modr, st, layer, layer // 2, "final",
                              (final_norm_g.reshape(1, d),))
    if y is None:
        y = _final_norm(x, final_norm_g)
    y_sample = y[:st.ts].reshape(dec_batch, dec_seq, d)
    y_prompt = y[st.ts:].reshape(batch, seq, d)
    return (y_prompt, y_sample, jnp.stack(new_k, axis=1), jnp.stack(new_v, axis=1))
```

```python
import cmath
import functools
import math

import jax
import jax.numpy as jnp
import numpy as np
from jax import lax
from jax.experimental import pallas as pl
from jax.experimental.pallas import tpu as pltpu

F32 = jnp.float32
BF16 = jnp.bfloat16

HEAD_DIM = 128
GRID_W = 64
ROPE_THETA = 10000.0
EPS = 1e-6
POS_FREQS = 16
HY_ORDER = 2
LANES = 128
MOD_ROWS = 8
VMEM_LIMIT_BYTES = 56 * 1024 * 1024


def _cparams(n_axes):
    return pltpu.CompilerParams(dimension_semantics=("arbitrary",) * n_axes,
                                vmem_limit_bytes=VMEM_LIMIT_BYTES)


def _pick(n, pref, mult):
    best = None
    d = mult
    while d <= min(n, pref):
        if n % d == 0:
            best = d
        d += mult
    assert best is not None, (n, pref, mult)
    return best


def _roundup(x, m):
    return (x + m - 1) // m * m


def _mod_kernel(c_ref, w_ref, b_ref, o_ref):
    c = c_ref[...]
    s = (c * jax.nn.sigmoid(c)).astype(BF16)
    o_ref[...] = jnp.dot(s, w_ref[...].astype(BF16), preferred_element_type=F32) + b_ref[...]


def _mod_all(cvecs, w_mod, b_mod):
    depth, d, n = w_mod.shape
    tn = _pick(n, 1024, LANES)
    return pl.pallas_call(
        _mod_kernel,
        grid=(depth, n // tn),
        in_specs=[pl.BlockSpec((MOD_ROWS, d), lambda l, j: (0, 0)),
                  pl.BlockSpec((None, d, tn), lambda l, j: (l, 0, j)),
                  pl.BlockSpec((None, 1, tn), lambda l, j: (l, 0, j))],
        out_specs=pl.BlockSpec((None, MOD_ROWS, tn), lambda l, j: (l, 0, j)),
        out_shape=jax.ShapeDtypeStruct((depth, MOD_ROWS, n), F32),
        compiler_params=_cparams(2),
        name="mod_all",
    )(cvecs, w_mod, b_mod.reshape(depth, 1, n))


def _rms(x, g):
    return x * lax.rsqrt(jnp.mean(x * x, axis=-1, keepdims=True) + EPS) * g


def _norm_mod_kernel(x_ref, g_ref, sc_ref, sh_ref, o_ref):
    y = _rms(x_ref[...], g_ref[...])
    o_ref[...] = (y * (1.0 + sc_ref[...]) + sh_ref[...]).astype(o_ref.dtype)


def _norm_kernel(x_ref, g_ref, o_ref):
    o_ref[...] = _rms(x_ref[...], g_ref[...]).astype(o_ref.dtype)


class _Stream:
    def __init__(self, dec_batch, dec_seq, batch, seq):
        self.dec_batch, self.dec_seq, self.batch, self.seq = dec_batch, dec_seq, batch, seq
        self.ts = dec_batch * dec_seq
        self.tc = batch * seq
        self.t = self.ts + self.tc

    def group(self, row):
        return jnp.minimum(row // self.dec_seq, self.dec_batch)

    def token_tile(self, pref):
        return _pick(math.gcd(self.dec_seq, self.tc), pref, 16)


def _mod_index(st, layer, which, tm):
    def index(i):
        return ((layer * MOD_ROWS + st.group(i * tm)) * 6 + which, 0, 0)
    return index


def _norm_mod(x, g, modr, st, layer, which_sc, which_sh):
    t, d = x.shape
    tm = st.token_tile(512)
    return pl.pallas_call(
        _norm_mod_kernel,
        grid=(t // tm,),
        in_specs=[pl.BlockSpec((tm, d), lambda i: (i, 0)),
                  pl.BlockSpec((None, 1, d), lambda i: (layer, 0, 0)),
                  pl.BlockSpec((None, 1, d), _mod_index(st, layer, which_sc, tm)),
                  pl.BlockSpec((None, 1, d), _mod_index(st, layer, which_sh, tm))],
        out_specs=pl.BlockSpec((tm, d), lambda i: (i, 0)),
        out_shape=jax.ShapeDtypeStruct((t, d), BF16),
        compiler_params=_cparams(1),
        name="norm_mod",
    )(x, g, modr, modr)


def _final_norm(x, g):
    t, d = x.shape
    tm = _pick(t, 512, 8)
    return pl.pallas_call(
        _norm_kernel,
        grid=(t // tm,),
        in_specs=[pl.BlockSpec((tm, d), lambda i: (i, 0)),
                  pl.BlockSpec((1, d), lambda i: (0, 0))],
        out_specs=pl.BlockSpec((tm, d), lambda i: (i, 0)),
        out_shape=jax.ShapeDtypeStruct((t, d), F32),
        compiler_params=_cparams(1),
        name="final_norm",
    )(x, g.reshape(1, d))


def _gmm_kernel(wid_ref, nv_ref, *refs, mode, merge_tiles):
    i = pl.program_id(1)
    n_a = 1 if merge_tiles is None else 6
    a_refs, refs = refs[:n_a], refs[n_a:]
    if mode == "swiglu":
        wg_ref, wu_ref, o_ref, wgb, wub = refs
        wpairs = ((wg_ref, wgb), (wu_ref, wub))
    elif mode == "resid":
        w_ref, x_ref, gate_ref, o_ref, wb = refs
        wpairs = ((w_ref, wb),)
    else:
        w_ref, o_ref, wb = refs
        wpairs = ((w_ref, wb),)

    def load_a():
        if merge_tiles is None:
            return a_refs[0][...]
        as_ref, ac_ref, hs_ref, hc_ref, ga_ref, gh_ref = a_refs
        sample = i < merge_tiles
        attn = jnp.where(sample, as_ref[...], ac_ref[...]).astype(F32)
        hy = jnp.where(sample, hs_ref[...], hc_ref[...]).astype(F32)
        return jnp.concatenate([_rms(attn, ga_ref[...]).astype(BF16),
                                _rms(hy, gh_ref[...]).astype(BF16)], axis=1)

    prev = wid_ref[jnp.maximum(i - 1, 0)]
    new_w = jnp.logical_or(i == 0, wid_ref[i] != prev)

    @pl.when(jnp.logical_and(new_w, i < nv_ref[0]))
    def _cast():
        for src, dst in wpairs:
            dst[...] = src[...].astype(BF16)

    @pl.when(i >= nv_ref[0])
    def _unused_tile():
        o_ref[...] = jnp.zeros_like(o_ref)

    @pl.when(i < nv_ref[0])
    def _compute():
        a = load_a()
        if mode == "swiglu":
            g = jnp.dot(a, wgb[...], preferred_element_type=F32)
            u = jnp.dot(a, wub[...], preferred_element_type=F32)
            o_ref[...] = (g * jax.nn.sigmoid(g) * u).astype(o_ref.dtype)
        elif mode == "resid":
            acc = jnp.dot(a, wb[...], preferred_element_type=F32)
            o_ref[...] = x_ref[...] + gate_ref[...] * acc
        else:
            o_ref[...] = jnp.dot(a, wb[...], preferred_element_type=F32).astype(o_ref.dtype)


def _gmm(a, ws, wid, nvalid, *, mode, out_dtype, tm, tn, resid=None, merge=None, name="gmm"):
    _, k, n = ws[0].shape
    if merge is None:
        m = a.shape[0]
    else:
        n_sample_rows, gains, layer = merge
        m = a[0].shape[0] + a[1].shape[0]
    nt = m // tm
    if isinstance(wid, int):
        wid, nvalid = jnp.full((nt,), wid, jnp.int32), jnp.full((1,), nt, jnp.int32)
    assert m % tm == 0 and n % tn == 0 and wid.shape == (nt,)

    def a_index(j, i, wid_ref, nv_ref):
        return (jnp.minimum(i, nv_ref[0] - 1), 0)

    def w_index(j, i, wid_ref, nv_ref):
        return (wid_ref[i], 0, j)

    def o_index(j, i, wid_ref, nv_ref):
        return (i, j)

    if merge is None:
        in_specs = [pl.BlockSpec((tm, a.shape[1]), a_index)]
        args = [a]
        merge_tiles = None
    else:
        assert n_sample_rows % tm == 0
        merge_tiles = n_sample_rows // tm
        half = k // 2
        s_idx = lambda j, i, *_: (jnp.minimum(i, merge_tiles - 1), 0)
        c_idx = lambda j, i, *_: (jnp.maximum(i - merge_tiles, 0), 0)
        in_specs = [pl.BlockSpec((tm, half), s_idx), pl.BlockSpec((tm, half), c_idx),
                    pl.BlockSpec((tm, half), s_idx), pl.BlockSpec((tm, half), c_idx),
                    pl.BlockSpec((None, 1, half), lambda j, i, *_: (layer, 0, 0)),
                    pl.BlockSpec((None, 1, half), lambda j, i, *_: (layer, 0, 1))]
        args = [a[0], a[1], a[2], a[3], gains, gains]
    in_specs += [pl.BlockSpec((None, k, tn), w_index) for _ in ws]
    args += list(ws)
    if mode == "resid":
        x, modr, mod_index = resid
        in_specs += [pl.BlockSpec((tm, tn), o_index),
                     pl.BlockSpec((None, 1, tn),
                                  lambda j, i, wid_ref, nv_ref: mod_index(i)[:2] + (j,))]
        args += [x, modr]
    grid_spec = pltpu.PrefetchScalarGridSpec(
        num_scalar_prefetch=2,
        grid=(n // tn, nt),
        in_specs=in_specs,
        out_specs=pl.BlockSpec((tm, tn), o_index),
        scratch_shapes=[pltpu.VMEM((k, tn), BF16) for _ in ws],
    )
    return pl.pallas_call(
        functools.partial(_gmm_kernel, mode=mode, merge_tiles=merge_tiles),
        grid_spec=grid_spec,
        out_shape=jax.ShapeDtypeStruct((m, n), out_dtype),
        compiler_params=_cparams(2),
        name=name,
    )(wid, nvalid, *args)


def _qk_kernel(q_ref, k_ref, qg_ref, kg_ref, cos_ref, sin_ref, qo_ref, ko_ref, kc_ref, *, scale):
    cos = cos_ref[...]
    sin = sin_ref[...]

    def head(x, g):
        xn = _rms(x.astype(F32), g)
        return xn, xn * cos + pltpu.roll(xn, HEAD_DIM // 2, 1) * sin

    for h in range(q_ref.shape[1] // HEAD_DIM):
        sl = slice(h * HEAD_DIM, (h + 1) * HEAD_DIM)
        _, roped = head(q_ref[:, sl], qg_ref[...])
        qo_ref[:, sl] = (roped * scale).astype(qo_ref.dtype)
    for h in range(k_ref.shape[1] // HEAD_DIM):
        sl = slice(h * HEAD_DIM, (h + 1) * HEAD_DIM)
        xn, roped = head(k_ref[:, sl], kg_ref[...])
        ko_ref[:, sl] = roped.astype(ko_ref.dtype)
        kc_ref[:, sl] = xn


def _qk_norm_rope(proj, qg, kg, cos2, sin2, layer, a_w, kv_w, st):
    t = proj.shape[0]
    tm = st.token_tile(512)
    assert a_w % kv_w == 0
    return pl.pallas_call(
        functools.partial(_qk_kernel, scale=1.0 / math.sqrt(HEAD_DIM)),
        grid=(t // tm,),
        in_specs=[pl.BlockSpec((tm, a_w), lambda i: (i, 0)),
                  pl.BlockSpec((tm, kv_w), lambda i: (i, a_w // kv_w)),
                  pl.BlockSpec((None, 1, HEAD_DIM), lambda i: (layer, 0, 0)),
                  pl.BlockSpec((None, 1, HEAD_DIM), lambda i: (layer, 0, 0)),
                  pl.BlockSpec((tm, HEAD_DIM), lambda i: (i, 0)),
                  pl.BlockSpec((tm, HEAD_DIM), lambda i: (i, 0))],
        out_specs=[pl.BlockSpec((tm, a_w), lambda i: (i, 0)),
                   pl.BlockSpec((tm, kv_w), lambda i: (i, 0)),
                   pl.BlockSpec((tm, kv_w), lambda i: (i, 0))],
        out_shape=[jax.ShapeDtypeStruct((t, a_w), BF16),
                   jax.ShapeDtypeStruct((t, kv_w), BF16),
                   jax.ShapeDtypeStruct((t, kv_w), F32)],
        compiler_params=_cparams(1),
        name="qk_norm_rope",
    )(proj, proj, qg, kg, cos2, sin2)


def _rope_tables(st):
    pos = np.arange(st.dec_seq)
    inv = ROPE_THETA ** (-np.arange(0, HEAD_DIM // 2, 2, dtype=np.float64) / (HEAD_DIM // 2))
    ang = np.concatenate([(pos // GRID_W)[:, None] * inv, (pos % GRID_W)[:, None] * inv], axis=-1)
    cos2 = np.concatenate([np.cos(ang), np.cos(ang)], axis=-1)
    sin2 = np.concatenate([-np.sin(ang), np.sin(ang)], axis=-1)
    cos_all = np.concatenate([np.tile(cos2, (st.dec_batch, 1)), np.ones((st.tc, HEAD_DIM))])
    sin_all = np.concatenate([np.tile(sin2, (st.dec_batch, 1)), np.zeros((st.tc, HEAD_DIM))])
    return jnp.asarray(cos_all, F32), jnp.asarray(sin_all, F32)


KEY_CHUNK = 512


def _attn_kernel(*refs, g, has_cache):
    if has_cache:
        q_ref, ck_ref, cv_ref, k_ref, v_ref, o_ref = refs
    else:
        q_ref, k_ref, v_ref, o_ref = refs
    tq = q_ref.shape[0]
    q = jnp.concatenate([q_ref[:, h * HEAD_DIM:(h + 1) * HEAD_DIM] for h in range(g)], axis=0)
    lk = k_ref.shape[0]
    parts = []
    if has_cache:
        parts.append((ck_ref, cv_ref, 0, ck_ref.shape[0]))
    c0 = 0
    while c0 < lk:
        n = min(KEY_CHUNK, lk - c0)
        parts.append((k_ref, v_ref, c0, n))
        c0 += n
    m = acc = None
    for kr, vr, c0, n in parts:
        kc = kr[c0:c0 + n, :].astype(BF16)
        va = jnp.concatenate([vr[c0:c0 + n, :].astype(BF16), jnp.ones((n, HEAD_DIM), BF16)], axis=1)
        s = lax.dot_general(q, kc, (((1,), (1,)), ((), ())), preferred_element_type=F32)
        mc = jnp.max(s, axis=-1, keepdims=True)
        if m is None:
            m = mc
            acc = jnp.dot(jnp.exp(s - m).astype(BF16), va, preferred_element_type=F32)
        else:
            m_new = jnp.maximum(m, mc)
            acc = (jnp.exp(m - m_new) * acc
                   + jnp.dot(jnp.exp(s - m_new).astype(BF16), va, preferred_element_type=F32))
            m = m_new
    o = acc[:, :HEAD_DIM] / acc[:, HEAD_DIM:]
    for h in range(g):
        o_ref[:, h * HEAD_DIM:(h + 1) * HEAD_DIM] = o[h * tq:(h + 1) * tq].astype(o_ref.dtype)


def _attention(qn, kn, proj, v_col0, n_kv, nseq, seq_len, row0, cache=None):
    aw = qn.shape[1]
    g = aw // HEAD_DIM // n_kv
    tq = _pick(seq_len, 256, 16)
    nq = seq_len // tq
    assert row0 % seq_len == 0 and v_col0 % HEAD_DIM == 0
    in_specs = [pl.BlockSpec((tq, g * HEAD_DIM), lambda b, kv, i: (row0 // tq + b * nq + i, kv))]
    args = [qn]
    if cache is not None:
        ck, cv, layer = cache
        past = ck.shape[2]
        cspec = pl.BlockSpec((None, None, past, HEAD_DIM), lambda b, kv, i: (b, layer, 0, kv))
        in_specs += [cspec, cspec]
        args += [ck, cv]
    in_specs += [pl.BlockSpec((seq_len, HEAD_DIM), lambda b, kv, i: (row0 // seq_len + b, kv)),
                 pl.BlockSpec((seq_len, HEAD_DIM),
                              lambda b, kv, i: (row0 // seq_len + b, v_col0 // HEAD_DIM + kv))]
    args += [kn, proj]
    return pl.pallas_call(
        functools.partial(_attn_kernel, g=g, has_cache=cache is not None),
        grid=(nseq, n_kv, nq),
        in_specs=in_specs,
        out_specs=pl.BlockSpec((tq, g * HEAD_DIM), lambda b, kv, i: (b * nq + i, kv)),
        out_shape=jax.ShapeDtypeStruct((nseq * seq_len, aw), BF16),
        compiler_params=_cparams(3),
        name="attention",
    )(*args)


def _filter_kernel(feat_ref, w1_ref, b1_ref, w2_ref, b2_ref, w3_ref, ld_ref, o_ref, h_ref, *,
                   seq_len, hy_w):
    hp = lax.Precision.HIGHEST

    @pl.when(pl.program_id(1) == 0)
    def _hidden():
        h = jnp.sin(jnp.dot(feat_ref[...], w1_ref[...], precision=hp, preferred_element_type=F32)
                    + b1_ref[...])
        h_ref[...] = jnp.sin(jnp.dot(h, w2_ref[...], precision=hp, preferred_element_type=F32)
                             + b2_ref[...])

    h = jnp.dot(h_ref[...].astype(BF16), w3_ref[...].astype(BF16), preferred_element_type=F32)
    tl, tn = h.shape
    row = lax.broadcasted_iota(jnp.int32, (tl, 1), 0) + pl.program_id(0) * tl
    t = row.astype(F32) / seq_len
    h = h * jnp.exp(-jnp.exp(ld_ref[...]) * t)
    col = lax.broadcasted_iota(jnp.int32, (1, tn), 1) + pl.program_id(1) * tn
    backward = (col // hy_w) % 2 == 1
    o_ref[...] = jnp.where(row == 0, jnp.where(backward, 0.0, h), h).astype(o_ref.dtype)


def _filter_features(seq_len):
    t = np.arange(seq_len, dtype=np.float32) / np.float32(seq_len)
    fr = np.arange(1, POS_FREQS + 1, dtype=np.float32)
    ang = float(np.float32(2.0 * math.pi)) * t[:, None].astype(np.float64) * fr
    feat = np.concatenate([t[:, None], np.sin(ang), np.cos(ang)], axis=-1)
    out = np.zeros((seq_len, LANES), np.float32)
    out[:, :feat.shape[1]] = feat
    return jnp.asarray(out)


def _pad2(x, rows, cols):
    return jnp.pad(x, ((0, rows - x.shape[0]), (0, cols - x.shape[1])))


def _hyena_filters(seq_len, layer, p, hy_w):
    n = p["filt_w3"].shape[-1]
    feat = _filter_features(seq_len)
    w1 = _pad2(p["filt_w1"][layer], LANES, LANES)
    b1 = _pad2(p["filt_b1"][layer][None, :], 1, LANES)
    w2 = _pad2(p["filt_w2"][layer], LANES, LANES)
    b2 = _pad2(p["filt_b2"][layer][None, :], 1, LANES)
    w3 = _pad2(p["filt_w3"][layer], LANES, n)
    ld = p["filt_log_decay"][layer][None, :]
    tl = _pick(seq_len, 1024, 16)
    tn = _pick(n, 2048, LANES)
    return pl.pallas_call(
        functools.partial(_filter_kernel, seq_len=seq_len, hy_w=hy_w),
        grid=(seq_len // tl, n // tn),
        in_specs=[pl.BlockSpec((tl, LANES), lambda i, j: (i, 0)),
                  pl.BlockSpec((LANES, LANES), lambda i, j: (0, 0)),
                  pl.BlockSpec((1, LANES), lambda i, j: (0, 0)),
                  pl.BlockSpec((LANES, LANES), lambda i, j: (0, 0)),
                  pl.BlockSpec((1, LANES), lambda i, j: (0, 0)),
                  pl.BlockSpec((LANES, tn), lambda i, j: (0, j)),
                  pl.BlockSpec((1, tn), lambda i, j: (0, j))],
        out_specs=pl.BlockSpec((tl, tn), lambda i, j: (i, j)),
        out_shape=jax.ShapeDtypeStruct((seq_len, n), BF16),
        scratch_shapes=[pltpu.VMEM((tl, LANES), F32)],
        compiler_params=_cparams(2),
        name="hy_filter",
    )(feat, w1, b1, w2, b2, w3, ld)


class _Dft:
    def __init__(self, seq_len, radix):
        self.seq_len, self.r = seq_len, radix
        n = 2 * seq_len
        self.n1 = n // radix
        self.la = seq_len // radix
        k1 = self.n1 // 2 + 1
        self.k1p = _roundup(k1, 16)
        a = np.arange(self.la)[None, :]
        kk = np.arange(self.k1p)[:, None]
        valid = kk < k1
        th = 2.0 * math.pi * kk * a / self.n1
        fwd = np.concatenate([np.where(valid, np.cos(th), 0.0), np.where(valid, -np.sin(th), 0.0)])
        wgt = np.where((kk == 0) | (kk == self.n1 // 2), 1.0, 2.0) * valid / n
        inv = np.concatenate([wgt * np.cos(th), -wgt * np.sin(th)]).T
        self.fwd = jnp.asarray(fwd, F32)
        self.inv = jnp.asarray(inv, F32)
        ph = 2.0 * math.pi * kk / n
        ones = np.ones((1, LANES))
        self.tw_re = jnp.asarray(np.cos(ph) * ones, F32)
        self.tw_im = jnp.asarray(-np.sin(ph) * ones, F32)


def _cmul_const(x, w):
    wr, wi = w.real, w.imag
    if abs(wi) < 1e-12:
        if abs(wr - 1.0) < 1e-12:
            return x
        if abs(wr + 1.0) < 1e-12:
            return (-x[0], -x[1])
    if abs(wr) < 1e-12:
        if abs(wi - 1.0) < 1e-12:
            return (-x[1], x[0])
        if abs(wi + 1.0) < 1e-12:
            return (x[1], -x[0])
    return (x[0] * wr - x[1] * wi, x[0] * wi + x[1] * wr)


def _fft_list(xs, sign):
    n = len(xs)
    if n == 1:
        return xs
    ev = _fft_list(xs[0::2], sign)
    od = _fft_list(xs[1::2], sign)
    out = [None] * n
    for k in range(n // 2):
        t = _cmul_const(od[k], cmath.exp(sign * 2j * math.pi * k / n))
        out[k] = (ev[k][0] + t[0], ev[k][1] + t[1])
        out[k + n // 2] = (ev[k][0] - t[0], ev[k][1] - t[1])
    return out


ROW_CHUNK = 16
LONGCONV_ROWS = 1024


def _lane_tile(x, width):
    return x if width == LANES else jnp.concatenate([x] * (width // LANES), axis=1)


def _twiddles(tw_re_ref, tw_im_ref, rows, r, tc):
    if r == 1:
        return []
    w1 = (tw_re_ref[rows, :], tw_im_ref[rows, :])
    ws = [w1]
    for _ in range(r - 2):
        pr, pi = ws[-1]
        ws.append((pr * w1[0] - pi * w1[1], pr * w1[1] + pi * w1[0]))
    return [(_lane_tile(wr, tc), _lane_tile(wi, tc)) for wr, wi in ws]


def _spectrum_chunk(y_ref, tws, c0, k1p, r):
    rows = pl.ds(c0, ROW_CHUNK)
    rows_im = pl.ds(k1p + c0, ROW_CHUNK)
    ts = []
    for b in range(r):
        yr = y_ref[b, rows, :]
        yi = y_ref[b, rows_im, :]
        if b == 0:
            ts.append((yr, yi))
        else:
            wr, wi = tws[b - 1]
            ts.append((yr * wr - yi * wi, yr * wi + yi * wr))
    return _fft_list(ts, -1)


def _lanes(q):
    return slice(q * LANES, (q + 1) * LANES)


def _conv3_group(u_ref, w_ref, b_ref, q, seq):
    x = u_ref[seq, _lanes(q)].astype(F32)
    n = x.shape[0]
    row = lax.broadcasted_iota(jnp.int32, x.shape, 0)
    prev = jnp.where(row == 0, 0.0, pltpu.roll(x, 1, 0))
    nxt = jnp.where(row == n - 1, 0.0, pltpu.roll(x, n - 1, 0))
    w = w_ref[:, _lanes(q)]
    return prev * w[0:1] + x * w[1:2] + nxt * w[2:3] + b_ref[:, _lanes(q)]


def _stage1(f_ref, load_group, xf_ref, y_ref, r, la):
    nq = xf_ref.shape[0]
    for q in range(nq):
        xf_ref[q] = load_group(q)
    f = f_ref[...].astype(BF16)
    for b in range(r):
        xb = jnp.concatenate([xf_ref[q, pl.ds(b, la, stride=r), :] for q in range(nq)], axis=1)
        y_ref[b] = jnp.dot(f, xb.astype(BF16), preferred_element_type=F32)


def _spectrum_kernel(*refs, r, k1p, la):
    if r > 1:
        f_ref, tw_re_ref, tw_im_ref, hf_ref, hb_ref, kr_ref, ki_ref, xf_ref, y_ref, acc_ref = refs
    else:
        f_ref, hf_ref, hb_ref, kr_ref, ki_ref, xf_ref, y_ref, acc_ref = refs
        tw_re_ref = tw_im_ref = None
    tc = y_ref.shape[-1]

    def sweep(src_ref, backward):
        _stage1(f_ref, lambda q: src_ref[:, _lanes(q)].astype(F32), xf_ref, y_ref, r, la)

        def chunk(c, carry):
            c0 = pl.multiple_of(c * ROW_CHUNK, ROW_CHUNK)
            rows = pl.ds(c0, ROW_CHUNK)
            rows_im = pl.ds(k1p + c0, ROW_CHUNK)
            x = _spectrum_chunk(y_ref, _twiddles(tw_re_ref, tw_im_ref, rows, r, tc), c0, k1p, r)
            for k2 in range(r):
                if backward:
                    kr_ref[k2, rows, :] = (acc_ref[k2, rows, :] + x[k2][0]).astype(kr_ref.dtype)
                    ki_ref[k2, rows, :] = (acc_ref[k2, rows_im, :] - x[k2][1]).astype(ki_ref.dtype)
                else:
                    acc_ref[k2, rows, :] = x[k2][0]
                    acc_ref[k2, rows_im, :] = x[k2][1]
            return carry

        lax.fori_loop(0, k1p // ROW_CHUNK, chunk, 0)

    sweep(hf_ref, False)
    sweep(hb_ref, True)


def _filter_spectrum(taps, dft, hy_w):
    r, la, k1p, seq_len = dft.r, dft.la, dft.k1p, dft.seq_len
    tc = _pick(hy_w, 256, LANES)
    nct = hy_w // tc

    def tap_spec(direction):
        return pl.BlockSpec((seq_len, tc), lambda o, j: (0, (o * 2 + direction) * nct + j))

    const = [pl.BlockSpec(dft.fwd.shape, lambda o, j: (0, 0))]
    cargs = [dft.fwd]
    if r > 1:
        const += [pl.BlockSpec(dft.tw_re.shape, lambda o, j: (0, 0))] * 2
        cargs += [dft.tw_re, dft.tw_im]
    out_spec = pl.BlockSpec((None, r, k1p, tc), lambda o, j: (o, 0, 0, j))
    shape = jax.ShapeDtypeStruct((HY_ORDER, r, k1p, hy_w), BF16)
    return pl.pallas_call(
        functools.partial(_spectrum_kernel, r=r, k1p=k1p, la=la),
        grid=(HY_ORDER, nct),
        in_specs=const + [tap_spec(0), tap_spec(1)],
        out_specs=[out_spec, out_spec],
        out_shape=[shape, shape],
        scratch_shapes=[pltpu.VMEM((tc // LANES, seq_len, LANES), F32),
                        pltpu.VMEM((r, 2 * k1p, tc), F32),
                        pltpu.VMEM((r, 2 * k1p, tc), F32)],
        compiler_params=_cparams(2),
        name="hy_filter_spectrum",
    )(*cargs, taps, taps)


def _longconv_kernel(*refs, n_sub, **kw):
    seq_len = refs[-3].shape[1]
    for sub in range(n_sub):
        _longconv_one(refs, slice(sub * seq_len, (sub + 1) * seq_len), **kw)


def _longconv_one(refs, seq, *, r, k1p, la, z_conv):
    refs = list(refs)
    f_ref, fi_ref = refs[:2]
    del refs[:2]
    tw_re_ref = tw_im_ref = None
    if r > 1:
        tw_re_ref, tw_im_ref = refs[:2]
        del refs[:2]
    z_ref = refs.pop(0)
    if z_conv:
        zw_ref, zb_ref = refs[:2]
        del refs[:2]
    gate_ref, gw_ref, gb_ref, skip_ref, kr_ref, ki_ref, o_ref, xf_ref, y_ref, g_ref = refs
    tc = y_ref.shape[-1]
    if z_conv:
        load_z = lambda q: _conv3_group(z_ref, zw_ref, zb_ref, q, seq)
    else:
        load_z = lambda q: z_ref[seq, _lanes(q)].astype(F32)
    _stage1(f_ref, load_z, xf_ref, y_ref, r, la)

    def chunk(c, carry):
        c0 = pl.multiple_of(c * ROW_CHUNK, ROW_CHUNK)
        rows = pl.ds(c0, ROW_CHUNK)
        rows_im = pl.ds(k1p + c0, ROW_CHUNK)
        tws = _twiddles(tw_re_ref, tw_im_ref, rows, r, tc)
        x = _spectrum_chunk(y_ref, tws, c0, k1p, r)
        prod = []
        for k2 in range(r):
            kr = kr_ref[k2, rows, :].astype(F32)
            ki = ki_ref[k2, rows, :].astype(F32)
            prod.append((x[k2][0] * kr - x[k2][1] * ki, x[k2][0] * ki + x[k2][1] * kr))
        s = _fft_list(prod, +1)
        for b in range(r):
            if b == 0:
                gr, gi = s[0]
            else:
                wr, wi = tws[b - 1]
                gr = s[b][0] * wr + s[b][1] * wi
                gi = s[b][1] * wr - s[b][0] * wi
            g_ref[b, rows, :] = gr.astype(g_ref.dtype)
            g_ref[b, rows_im, :] = gi.astype(g_ref.dtype)
        return carry

    lax.fori_loop(0, k1p // ROW_CHUNK, chunk, 0)
    fi = fi_ref[...].astype(BF16)
    for b in range(r):
        yb = jnp.dot(fi, g_ref[b], preferred_element_type=F32)
        rows = pl.ds(b, la, stride=r)
        for q in range(xf_ref.shape[0]):
            xf_ref[q, rows, :] = yb[:, _lanes(q)] + skip_ref[:, _lanes(q)] * xf_ref[q, rows, :]
    for q in range(xf_ref.shape[0]):
        gate = _conv3_group(gate_ref, gw_ref, gb_ref, q, seq)
        o_ref[seq, _lanes(q)] = (gate * xf_ref[q]).astype(o_ref.dtype)


def _longconv(proj, u_col0, row0, layer, conv_w, conv_b, z_prev, order, skip, kf_re, kf_im, dft,
              nseq, hy_w):
    r, la, k1p, seq_len = dft.r, dft.la, dft.k1p, dft.seq_len
    tc = _pick(math.gcd(hy_w, u_col0), 256 if r > 1 else 1024, LANES)
    n_sub = _pick(math.gcd(nseq, row0 // seq_len) or nseq, max(1, LONGCONV_ROWS // seq_len), 1)
    blk = n_sub * seq_len
    assert row0 % blk == 0
    nct = hy_w // tc
    seq0 = row0 // blk
    z_conv = z_prev is None

    def u_specs(part):
        return [pl.BlockSpec((blk, tc), lambda s, j: (seq0 + s, u_col0 // tc + part * nct + j)),
                pl.BlockSpec((None, 3, tc), lambda s, j: (layer, 0, part * nct + j)),
                pl.BlockSpec((None, 1, tc), lambda s, j: (layer, 0, part * nct + j))]

    cdim = lambda s, j: (0, 0)
    in_specs = [pl.BlockSpec(dft.fwd.shape, cdim), pl.BlockSpec(dft.inv.shape, cdim)]
    args = [dft.fwd, dft.inv]
    if r > 1:
        in_specs += [pl.BlockSpec(dft.tw_re.shape, cdim)] * 2
        args += [dft.tw_re, dft.tw_im]
    if z_conv:
        in_specs += u_specs(0)
        args += [proj, conv_w, conv_b]
    else:
        in_specs.append(pl.BlockSpec((blk, tc), lambda s, j: (s, j)))
        args.append(z_prev)
    in_specs += u_specs(1 + order)
    args += [proj, conv_w, conv_b]
    in_specs += [pl.BlockSpec((None, 1, tc), lambda s, j: (layer * HY_ORDER + order, 0, j)),
                 pl.BlockSpec((None, r, k1p, tc), lambda s, j: (order, 0, 0, j)),
                 pl.BlockSpec((None, r, k1p, tc), lambda s, j: (order, 0, 0, j))]
    args += [skip, kf_re, kf_im]
    return pl.pallas_call(
        functools.partial(_longconv_kernel, n_sub=n_sub, r=r, k1p=k1p, la=la, z_conv=z_conv),
        grid=(nseq // n_sub, nct),
        in_specs=in_specs,
        out_specs=pl.BlockSpec((blk, tc), lambda s, j: (s, j)),
        out_shape=jax.ShapeDtypeStruct((nseq * seq_len, hy_w), BF16),
        scratch_shapes=[pltpu.VMEM((tc // LANES, seq_len, LANES), F32),
                        pltpu.VMEM((r, 2 * k1p, tc), F32),
                        pltpu.VMEM((r, 2 * k1p, tc), BF16)],
        compiler_params=_cparams(2),
        name="hy_longconv",
    )(*args)


def _hyena(proj, layer, p, st, u_col0, hy_w, dfts):
    skip = p["hy_skip"].reshape(-1, 1, hy_w)
    conv_w = p["hy_conv_w"]
    conv_b = p["hy_conv_b"].reshape(-1, 1, 3 * hy_w)
    outs = []
    for nseq, seq_len, row0 in ((st.dec_batch, st.dec_seq, 0), (st.batch, st.seq, st.ts)):
        dft = dfts[seq_len]
        taps = _hyena_filters(seq_len, layer, p, hy_w)
        kf_re, kf_im = _filter_spectrum(taps, dft, hy_w)
        z = None
        for order in range(HY_ORDER):
            z = _longconv(proj, u_col0, row0, layer, conv_w, conv_b, z, order, skip, kf_re, kf_im,
                          dft, nseq, hy_w)
        outs.append(z)
    return outs


def _route_kernel(x_ref, g_ref, sc_ref, sh_ref, rw_ref, hp_ref, w1_ref, w2_ref, code_ref,
                  rank_ref, cnt_ref, carry_ref, *, n_experts):
    i = pl.program_id(0)

    @pl.when(i == 0)
    def _init():
        carry_ref[...] = jnp.zeros_like(carry_ref)

    h = _rms(x_ref[...], g_ref[...]) * (1.0 + sc_ref[...]) + sh_ref[...]
    hp_ref[...] = h

    logits = jnp.dot(h, rw_ref[...], precision=lax.Precision.HIGHEST, preferred_element_type=F32)
    lane = lax.broadcasted_iota(jnp.int32, logits.shape, 1)
    neg = jnp.float32(-jnp.inf)
    logits = jnp.where(lane < n_experts, logits, neg)
    m1 = jnp.max(logits, axis=-1, keepdims=True)
    i1 = jnp.min(jnp.where(logits == m1, lane, LANES), axis=-1, keepdims=True)
    rest = jnp.where(lane == i1, neg, logits)
    m2 = jnp.max(rest, axis=-1, keepdims=True)
    i2 = jnp.min(jnp.where(rest == m2, lane, LANES), axis=-1, keepdims=True)
    e2 = jnp.exp(m2 - m1)
    w1 = 1.0 / (1.0 + e2)
    w2 = e2 / (1.0 + e2)
    w1_ref[...] = jnp.broadcast_to(w1, w1_ref.shape)
    w2_ref[...] = jnp.broadcast_to(w2, w2_ref.shape)

    code = jnp.where(lane == i1, 1.0, jnp.where(lane == i2, 2.0, 0.0))
    code_t = code.T[:code_ref.shape[0], :]
    code_ref[...] = code_t.astype(jnp.int32)
    sel = (code_t > 0.0).astype(BF16)
    tm = sel.shape[1]
    upper = (lax.broadcasted_iota(jnp.int32, (tm, tm), 0)
             < lax.broadcasted_iota(jnp.int32, (tm, tm), 1)).astype(BF16)
    excl = jnp.dot(sel, upper, preferred_element_type=F32)
    carry = carry_ref[...]
    rank_ref[...] = (excl + carry[:, :1]).astype(jnp.int32)
    carry = carry + jnp.sum(sel.astype(F32), axis=-1, keepdims=True)
    carry_ref[...] = carry
    cnt_ref[...] = carry.astype(jnp.int32)


def _route(x, g, modr, rw, st, layer, n_experts):
    t, d = x.shape
    tm = st.token_tile(512)
    assert tm % LANES == 0 and n_experts <= 8
    rwp = _pad2(rw, d, LANES)
    row_spec = pl.BlockSpec((tm, LANES), lambda i: (i, 0))
    col_spec = pl.BlockSpec((8, tm), lambda i: (0, i))
    return pl.pallas_call(
        functools.partial(_route_kernel, n_experts=n_experts),
        grid=(t // tm,),
        in_specs=[pl.BlockSpec((tm, d), lambda i: (i, 0)),
                  pl.BlockSpec((None, 1, d), lambda i: (layer, 0, 0)),
                  pl.BlockSpec((None, 1, d), _mod_index(st, layer, 4, tm)),
                  pl.BlockSpec((None, 1, d), _mod_index(st, layer, 3, tm)),
                  pl.BlockSpec((d, LANES), lambda i: (0, 0))],
        out_specs=[pl.BlockSpec((tm, d), lambda i: (i, 0)), row_spec, row_spec,
                   col_spec, col_spec, pl.BlockSpec((8, LANES), lambda i: (0, 0))],
        out_shape=[jax.ShapeDtypeStruct((t, d), F32),
                   jax.ShapeDtypeStruct((t, LANES), F32),
                   jax.ShapeDtypeStruct((t, LANES), F32),
                   jax.ShapeDtypeStruct((8, t), jnp.int32),
                   jax.ShapeDtypeStruct((8, t), jnp.int32),
                   jax.ShapeDtypeStruct((8, LANES), jnp.int32)],
        scratch_shapes=[pltpu.VMEM((8, LANES), F32)],
        compiler_params=_cparams(1),
        name="moe_route",
    )(x, g, modr, modr, rwp)


DISPATCH_ROWS = 512


GATHER_UNROLL = 8


def _issue_row_gather(idx_ref, src_ref, dst_ref, sem, priorities):
    def trip(g, carry):
        for u in range(GATHER_UNROLL):
            rr = g * GATHER_UNROLL + u
            pltpu.make_async_copy(src_ref.at[pl.ds(idx_ref[0, rr], 1)], dst_ref.at[pl.ds(rr, 1)],
                                  sem).start(priority=priorities[u % len(priorities)])
        return carry

    lax.fori_loop(0, dst_ref.shape[0] // GATHER_UNROLL, trip, 0)


def _wait_row_gather(dst_ref, sem):
    pltpu.make_async_copy(dst_ref, dst_ref, sem).wait()


def _pipelined_gather(idx_refs, next_idx_refs, src_ref, bufs, sems, n_steps, priorities):
    i = pl.program_id(0)
    cur = i % 2

    @pl.when(jnp.logical_and(i == 0, n_steps > 0))
    def _first():
        for idx_ref, buf in zip(idx_refs, bufs):
            _issue_row_gather(idx_ref, src_ref, buf.at[0], sems.at[0], priorities)

    @pl.when(i + 1 < n_steps)
    def _prefetch():
        for idx_ref, buf in zip(next_idx_refs, bufs):
            _issue_row_gather(idx_ref, src_ref, buf.at[1 - cur], sems.at[1 - cur], priorities)

    @pl.when(i < n_steps)
    def _wait():
        for buf in bufs:
            _wait_row_gather(buf.at[cur], sems.at[cur])

    return cur


def _dispatch_kernel(nv_ref, tok_ref, tok_next_ref, h_ref, xs_ref, rows_ref, sems):
    n_steps = nv_ref[0]
    cur = _pipelined_gather([tok_ref], [tok_next_ref], h_ref, [rows_ref], sems, n_steps, (1,))

    @pl.when(pl.program_id(0) < n_steps)
    def _used():
        xs_ref[...] = rows_ref[cur].astype(xs_ref.dtype)

    @pl.when(pl.program_id(0) >= n_steps)
    def _unused():
        xs_ref[...] = jnp.zeros_like(xs_ref)


def _index_specs(nt, td):
    return [pl.BlockSpec((None, 1, td), lambda i, *_: (i, 0, 0), memory_space=pltpu.SMEM),
            pl.BlockSpec((None, 1, td), lambda i, *_: (jnp.minimum(i + 1, nt - 1), 0, 0),
                         memory_space=pltpu.SMEM)]


def _dispatch(h, slot_token, n_used_slots):
    n_slots = slot_token.shape[0]
    d = h.shape[1]
    td = _pick(n_slots, DISPATCH_ROWS, LANES)
    nt = n_slots // td
    tok = slot_token.reshape(nt, 1, td)
    grid_spec = pltpu.PrefetchScalarGridSpec(
        num_scalar_prefetch=1,
        grid=(nt,),
        in_specs=_index_specs(nt, td) + [pl.BlockSpec(memory_space=pl.ANY)],
        out_specs=pl.BlockSpec((td, d), lambda i, nv: (i, 0)),
        scratch_shapes=[pltpu.VMEM((2, td, d), F32), pltpu.SemaphoreType.DMA((2,))],
    )
    return pl.pallas_call(
        _dispatch_kernel,
        grid_spec=grid_spec,
        out_shape=jax.ShapeDtypeStruct((n_slots, d), BF16),
        compiler_params=_cparams(1),
        name="moe_dispatch",
    )((n_used_slots + td - 1) // td, tok, tok, h)


def _combine_kernel(*refs, post):
    (s0_ref, s0n_ref, s1_ref, s1n_ref, ys_ref, x_ref, gate_ref, w1_ref, w2_ref) = refs[:9]
    rest = refs[9:]
    n_post = {"none": 0, "norm_mod": 3, "final": 1}[post]
    post_refs, rest = rest[:n_post], rest[n_post:]
    n_out = 1 if post in ("none", "final") else 2
    outs, (ya, yb, sems) = rest[:n_out], rest[n_out:]
    cur = _pipelined_gather([s0_ref, s1_ref], [s0n_ref, s1n_ref], ys_ref, [ya, yb], sems,
                            pl.num_programs(0), (0, 1))
    w1 = _lane_tile(w1_ref[...], ya.shape[-1])
    w2 = _lane_tile(w2_ref[...], ya.shape[-1])
    xn = x_ref[...] + gate_ref[...] * (w1 * ya[cur] + w2 * yb[cur])
    if post == "final":
        outs[0][...] = _rms(xn, post_refs[0][...])
    else:
        outs[0][...] = xn
    if post == "norm_mod":
        g_ref, sc_ref, sh_ref = post_refs
        outs[1][...] = (_rms(xn, g_ref[...]) * (1.0 + sc_ref[...]) + sh_ref[...]).astype(outs[1].dtype)


def _combine(ys, slot0, slot1, x, modr, w1, w2, st, layer, post, post_args=()):
    t, d = x.shape
    td = _pick(math.gcd(st.dec_seq, st.tc), DISPATCH_ROWS, LANES)
    nt = t // td
    row = pl.BlockSpec((td, d), lambda i: (i, 0))
    in_specs = (_index_specs(nt, td) + _index_specs(nt, td)
                + [pl.BlockSpec(memory_space=pl.ANY), row,
                   pl.BlockSpec((None, 1, d), _mod_index(st, layer, 5, td)),
                   pl.BlockSpec((td, LANES), lambda i: (i, 0)),
                   pl.BlockSpec((td, LANES), lambda i: (i, 0))])
    s0 = slot0.reshape(nt, 1, td)
    s1 = slot1.reshape(nt, 1, td)
    args = [s0, s0, s1, s1, ys, x, modr, w1, w2]
    out_specs, out_shape = [row], [jax.ShapeDtypeStruct((t, d), F32)]
    if post == "norm_mod":
        in_specs += [pl.BlockSpec((None, 1, d), lambda i: (layer + 1, 0, 0)),
                     pl.BlockSpec((None, 1, d), _mod_index(st, layer + 1, 1, td)),
                     pl.BlockSpec((None, 1, d), _mod_index(st, layer + 1, 0, td))]
        args += [post_args[0], modr, modr]
        out_specs.append(row)
        out_shape.append(jax.ShapeDtypeStruct((t, d), BF16))
    elif post == "final":
        in_specs.append(pl.BlockSpec((1, d), lambda i: (0, 0)))
        args.append(post_args[0])
    return pl.pallas_call(
        functools.partial(_combine_kernel, post=post),
        grid=(nt,),
        in_specs=in_specs,
        out_specs=out_specs,
        out_shape=out_shape,
        scratch_shapes=[pltpu.VMEM((2, td, d), F32), pltpu.VMEM((2, td, d), F32),
                        pltpu.SemaphoreType.DMA((2,))],
        compiler_params=_cparams(1),
        name="moe_combine",
    )(*args)


MOE_ROWS = 1024
MOE_DOWN_SPLIT = 2


def _moe_layer(x, p, modr, st, layer, moe_idx, post, post_args):
    t, d = x.shape
    n_experts = p["router_w"].shape[-1]
    d_ff = p["moe_w_gate"].shape[-1]
    tm = MOE_ROWS
    hp, w1, w2, code, rank, cnt = _route(x, p["norm2_g"].reshape(-1, 1, d), modr,
                                         p["router_w"][moe_idx], st, layer, n_experts)
    counts = cnt[:n_experts, 0]
    padded = (counts + tm - 1) // tm * tm
    ends = jnp.cumsum(padded)
    starts = ends - padded
    n_tiles = pl.cdiv(2 * t, tm) + n_experts
    n_slots = n_tiles * tm
    dest = starts[:, None] + rank[:n_experts]
    slot0 = jnp.sum(jnp.where(code[:n_experts] == 1, dest, 0), axis=0).astype(jnp.int32)
    slot1 = jnp.sum(jnp.where(code[:n_experts] == 2, dest, 0), axis=0).astype(jnp.int32)
    tile_row = jnp.arange(n_tiles, dtype=jnp.int32) * tm
    tile_expert = jnp.minimum(jnp.sum(tile_row[:, None] >= ends[None, :], axis=1), n_experts - 1)
    wid = (moe_idx * n_experts + tile_expert).astype(jnp.int32)
    nvalid = (ends[-1:] // tm).astype(jnp.int32)
    expert_ids = jnp.arange(n_experts, dtype=jnp.int32)[:, None]
    token_ids = jnp.arange(t, dtype=jnp.int32)
    e0 = jnp.sum(jnp.where(code[:n_experts] == 1, expert_ids, 0), axis=0)
    e1 = jnp.sum(jnp.where(code[:n_experts] == 2, expert_ids, 0), axis=0)
    pair_token = jnp.sort(jnp.concatenate([e0 * t + token_ids, e1 * t + token_ids])) % t
    slot_expert = jnp.repeat(tile_expert, tm)
    local = jnp.arange(n_slots, dtype=jnp.int32) - starts[slot_expert]
    compact_start = jnp.cumsum(counts) - counts
    compact = compact_start[slot_expert] + jnp.clip(local, 0, jnp.maximum(counts[slot_expert] - 1, 0))
    slot_token = pair_token[jnp.clip(compact, 0, 2 * t - 1)].astype(jnp.int32)

    xs = _dispatch(hp, slot_token, ends[-1:].astype(jnp.int32))
    wg = p["moe_w_gate"].reshape(-1, d, d_ff)
    wu = p["moe_w_up"].reshape(-1, d, d_ff)
    wd = p["moe_w_down"].reshape(-1, d_ff, d)
    hidden = _gmm(xs, (wg, wu), wid, nvalid, mode="swiglu", out_dtype=BF16, tm=tm,
                  tn=_pick(d_ff, 512, LANES), name="moe_gate_up")
    ys = _gmm(hidden, (wd,), jnp.repeat(wid, MOE_DOWN_SPLIT), nvalid * MOE_DOWN_SPLIT,
              mode="plain", out_dtype=F32, tm=tm // MOE_DOWN_SPLIT, tn=_pick(d, 512, LANES),
              name="moe_down")
    return _combine(ys, slot0, slot1, x, modr, w1, w2, st, layer, post, post_args)


def _dense_layer(x, p, modr, st, layer, ffn_idx):
    t, d = x.shape
    d_ff = p["ffn_w_gate"].shape[-1]
    h = _norm_mod(x, p["norm2_g"].reshape(-1, 1, d), modr, st, layer, 4, 3)
    hidden = _gmm(h, (p["ffn_w_gate"], p["ffn_w_up"]), ffn_idx, None, mode="swiglu",
                  out_dtype=BF16, tm=st.token_tile(1024), tn=_pick(d_ff, 512, LANES),
                  name="ffn_gate_up")
    tm = st.token_tile(512)
    return _gmm(hidden, (p["ffn_w_down"],), ffn_idx, None, mode="resid", out_dtype=F32, tm=tm,
                tn=_pick(d, 512, LANES), resid=(x, modr, _mod_index(st, layer, 5, tm)),
                name="ffn_down")


def _dft_radix(seq_len):
    r = 1
    while seq_len // r > 512 and r < 8:
        r *= 2
    return r


def kernel(x_prompt, x_sample, cache_k, cache_v, c, c_ctx, norm1_g, norm2_g, w_mod, b_mod, w_in,
           q_norm_g, k_norm_g, hy_conv_w, hy_conv_b, filt_w1, filt_b1, filt_w2, filt_b2, filt_w3,
           filt_log_decay, hy_skip, out_norm_g, w_out, ffn_w_gate, ffn_w_up, ffn_w_down, router_w,
           moe_w_gate, moe_w_up, moe_w_down, final_norm_g):
    p = dict(norm2_g=norm2_g, hy_conv_w=hy_conv_w, hy_conv_b=hy_conv_b, filt_w1=filt_w1,
             filt_b1=filt_b1, filt_w2=filt_w2, filt_b2=filt_b2, filt_w3=filt_w3,
             filt_log_decay=filt_log_decay, hy_skip=hy_skip, ffn_w_gate=ffn_w_gate,
             ffn_w_up=ffn_w_up, ffn_w_down=ffn_w_down, router_w=router_w, moe_w_gate=moe_w_gate,
             moe_w_up=moe_w_up, moe_w_down=moe_w_down)
    batch, seq, d = x_prompt.shape
    dec_batch, dec_seq, _ = x_sample.shape
    depth = w_in.shape[0]
    in_w = w_in.shape[-1]
    n_kv = cache_k.shape[3]
    kv_w = n_kv * HEAD_DIM
    a_w = (3 * d + 2 * kv_w - in_w) // 2
    hy_w = d - a_w
    u_col0 = a_w + 2 * kv_w
    past = cache_k.shape[2]
    st = _Stream(dec_batch, dec_seq, batch, seq)
    assert dec_batch + 1 <= MOD_ROWS

    x = jnp.concatenate([x_sample.reshape(st.ts, d), x_prompt.reshape(st.tc, d)], axis=0)
    cvecs = jnp.zeros((MOD_ROWS, d), F32).at[:dec_batch].set(c).at[dec_batch].set(c_ctx)
    modr = _mod_all(cvecs, w_mod, b_mod).reshape(depth * MOD_ROWS * 6, 1, d)
    cos2, sin2 = _rope_tables(st)
    dfts = {n: _Dft(n, _dft_radix(n)) for n in {dec_seq, seq}}
    norm1_g3 = norm1_g.reshape(depth, 1, d)
    qg3 = q_norm_g.reshape(depth, 1, HEAD_DIM)
    kg3 = k_norm_g.reshape(depth, 1, HEAD_DIM)
    tm = st.token_tile(1024)
    out_norm_g3 = out_norm_g.reshape(depth, 1, d)
    cache_k4 = cache_k.reshape(dec_batch, depth, past, kv_w)
    cache_v4 = cache_v.reshape(dec_batch, depth, past, kv_w)

    new_k, new_v = [], []
    h = y = None
    for layer in range(depth):
        if h is None:
            h = _norm_mod(x, norm1_g3, modr, st, layer, 1, 0)
        proj = _gmm(h, (w_in,), layer, None, mode="plain", out_dtype=BF16,
                    tm=st.token_tile(2048), tn=_pick(in_w, 768, LANES), name="w_in")
        qn, kn, kc = _qk_norm_rope(proj, qg3, kg3, cos2, sin2, layer, a_w, kv_w, st)
        new_k.append(kc[st.ts:].reshape(batch, seq, n_kv, HEAD_DIM))
        new_v.append(proj[st.ts:, a_w + kv_w:a_w + 2 * kv_w].astype(F32)
                     .reshape(batch, seq, n_kv, HEAD_DIM))

        attn_s = _attention(qn, kn, proj, a_w + kv_w, n_kv, dec_batch, dec_seq, 0,
                            cache=(cache_k4, cache_v4, layer))
        attn_c = _attention(qn, kn, proj, a_w + kv_w, n_kv, batch, seq, st.ts)
        hy_s, hy_c = _hyena(proj, layer, p, st, u_col0, hy_w, dfts)
        tm = st.token_tile(512)
        x = _gmm((attn_s, attn_c, hy_s, hy_c), (w_out,), layer, None, mode="resid", out_dtype=F32,
                 tm=tm, tn=_pick(d, 1024, LANES), merge=(st.ts, out_norm_g3, layer),
                 resid=(x, modr, _mod_index(st, layer, 2, tm)), name="w_out")
        h = None
        if layer % 2 == 0:
            x = _dense_layer(x, p, modr, st, layer, layer // 2)
        elif layer + 1 < depth:
            x, h = _moe_layer(x, p, modr, st, layer, layer // 2, "norm_mod", (norm1_g3,))
        else:
            (y,) = _moe_layer(x, p, modr, st, layer, layer // 2, "final",
                              (final_norm_g.reshape(1, d),))
    if y is None:
        y = _final_norm(x, final_norm_g)
    y_sample = y[:st.ts].reshape(dec_batch, dec_seq, d)
    y_prompt = y[st.ts:].reshape(batch, seq, d)
    return (y_prompt, y_sample, jnp.stack(new_k, axis=1), jnp.stack(new_v, axis=1))
```

```python
import cmath
import functools
import math

import jax
import jax.numpy as jnp
import numpy as np
from jax import lax
from jax.experimental import pallas as pl
from jax.experimental.pallas import tpu as pltpu

F32 = jnp.float32
BF16 = jnp.bfloat16

HEAD_DIM = 128
GRID_W = 64
ROPE_THETA = 10000.0
EPS = 1e-6
POS_FREQS = 16
HY_ORDER = 2
LANES = 128
MOD_ROWS = 8
VMEM_LIMIT_BYTES = 56 * 1024 * 1024


def _cparams(n_axes):
    return pltpu.CompilerParams(dimension_semantics=("arbitrary",) * n_axes,
                                vmem_limit_bytes=VMEM_LIMIT_BYTES)


def _pick(n, pref, mult):
    best = None
    d = mult
    while d <= min(n, pref):
        if n % d == 0:
            best = d
        d += mult
    assert best is not None, (n, pref, mult)
    return best


def _roundup(x, m):
    return (x + m - 1) // m * m


def _mod_kernel(c_ref, w_ref, b_ref, o_ref):
    c = c_ref[...]
    s = (c * jax.nn.sigmoid(c)).astype(BF16)
    o_ref[...] = jnp.dot(s, w_ref[...].astype(BF16), preferred_element_type=F32) + b_ref[...]


def _mod_all(cvecs, w_mod, b_mod):
    depth, d, n = w_mod.shape
    tn = _pick(n, 1024, LANES)
    return pl.pallas_call(
        _mod_kernel,
        grid=(depth, n // tn),
        in_specs=[pl.BlockSpec((MOD_ROWS, d), lambda l, j: (0, 0)),
                  pl.BlockSpec((None, d, tn), lambda l, j: (l, 0, j)),
                  pl.BlockSpec((None, 1, tn), lambda l, j: (l, 0, j))],
        out_specs=pl.BlockSpec((None, MOD_ROWS, tn), lambda l, j: (l, 0, j)),
        out_shape=jax.ShapeDtypeStruct((depth, MOD_ROWS, n), F32),
        compiler_params=_cparams(2),
        name="mod_all",
    )(cvecs, w_mod, b_mod.reshape(depth, 1, n))


def _rms(x, g):
    return x * lax.rsqrt(jnp.mean(x * x, axis=-1, keepdims=True) + EPS) * g


def _norm_mod_kernel(x_ref, g_ref, sc_ref, sh_ref, o_ref):
    y = _rms(x_ref[...], g_ref[...])
    o_ref[...] = (y * (1.0 + sc_ref[...]) + sh_ref[...]).astype(o_ref.dtype)


def _norm_kernel(x_ref, g_ref, o_ref):
    o_ref[...] = _rms(x_ref[...], g_ref[...]).astype(o_ref.dtype)


class _Stream:
    def __init__(self, dec_batch, dec_seq, batch, seq):
        self.dec_batch, self.dec_seq, self.batch, self.seq = dec_batch, dec_seq, batch, seq
        self.ts = dec_batch * dec_seq
        self.tc = batch * seq
        self.t = self.ts + self.tc

    def group(self, row):
        return jnp.minimum(row // self.dec_seq, self.dec_batch)

    def token_tile(self, pref):
        return _pick(math.gcd(self.dec_seq, self.tc), pref, 16)


def _mod_index(st, layer, which, tm):
    def index(i):
        return ((layer * MOD_ROWS + st.group(i * tm)) * 6 + which, 0, 0)
    return index


def _norm_mod(x, g, modr, st, layer, which_sc, which_sh):
    t, d = x.shape
    tm = st.token_tile(512)
    return pl.pallas_call(
        _norm_mod_kernel,
        grid=(t // tm,),
        in_specs=[pl.BlockSpec((tm, d), lambda i: (i, 0)),
                  pl.BlockSpec((None, 1, d), lambda i: (layer, 0, 0)),
                  pl.BlockSpec((None, 1, d), _mod_index(st, layer, which_sc, tm)),
                  pl.BlockSpec((None, 1, d), _mod_index(st, layer, which_sh, tm))],
        out_specs=pl.BlockSpec((tm, d), lambda i: (i, 0)),
        out_shape=jax.ShapeDtypeStruct((t, d), BF16),
        compiler_params=_cparams(1),
        name="norm_mod",
    )(x, g, modr, modr)


def _final_norm(x, g):
    t, d = x.shape
    tm = _pick(t, 512, 8)
    return pl.pallas_call(
        _norm_kernel,
        grid=(t // tm,),
        in_specs=[pl.BlockSpec((tm, d), lambda i: (i, 0)),
                  pl.BlockSpec((1, d), lambda i: (0, 0))],
        out_specs=pl.BlockSpec((tm, d), lambda i: (i, 0)),
        out_shape=jax.ShapeDtypeStruct((t, d), F32),
        compiler_params=_cparams(1),
        name="final_norm",
    )(x, g.reshape(1, d))


def _gmm_kernel(wid_ref, src_ref, *refs, mode, merge_tiles):
    i = pl.program_id(1)
    active = src_ref[i] == i
    n_a = 1 if merge_tiles is None else 6
    a_refs, refs = refs[:n_a], refs[n_a:]
    if mode == "swiglu":
        wg_ref, wu_ref, o_ref, wgb, wub = refs
        wpairs = ((wg_ref, wgb), (wu_ref, wub))
    elif mode == "resid":
        w_ref, x_ref, gate_ref, o_ref, wb = refs
        wpairs = ((w_ref, wb),)
    else:
        w_ref, o_ref, wb = refs
        wpairs = ((w_ref, wb),)

    def load_a():
        if merge_tiles is None:
            return a_refs[0][...]
        as_ref, ac_ref, hs_ref, hc_ref, ga_ref, gh_ref = a_refs
        sample = i < merge_tiles
        attn = jnp.where(sample, as_ref[...], ac_ref[...]).astype(F32)
        hy = jnp.where(sample, hs_ref[...], hc_ref[...]).astype(F32)
        return jnp.concatenate([_rms(attn, ga_ref[...]).astype(BF16),
                                _rms(hy, gh_ref[...]).astype(BF16)], axis=1)

    prev = wid_ref[jnp.maximum(i - 1, 0)]
    new_w = jnp.logical_or(i == 0, wid_ref[i] != prev)

    @pl.when(jnp.logical_and(new_w, active))
    def _cast():
        for src, dst in wpairs:
            dst[...] = src[...].astype(BF16)

    @pl.when(jnp.logical_not(active))
    def _unused_tile():
        o_ref[...] = jnp.zeros_like(o_ref)

    @pl.when(active)
    def _compute():
        a = load_a()
        if mode == "swiglu":
            g = jnp.dot(a, wgb[...], preferred_element_type=F32)
            u = jnp.dot(a, wub[...], preferred_element_type=F32)
            o_ref[...] = (g * jax.nn.sigmoid(g) * u).astype(o_ref.dtype)
        elif mode == "resid":
            acc = jnp.dot(a, wb[...], preferred_element_type=F32)
            o_ref[...] = x_ref[...] + gate_ref[...] * acc
        else:
            o_ref[...] = jnp.dot(a, wb[...], preferred_element_type=F32).astype(o_ref.dtype)


def _gmm(a, ws, wid, src, *, mode, out_dtype, tm, tn, resid=None, merge=None, name="gmm"):
    _, k, n = ws[0].shape
    if merge is None:
        m = a.shape[0]
    else:
        n_sample_rows, gains, layer = merge
        m = a[0].shape[0] + a[1].shape[0]
    nt = m // tm
    if isinstance(wid, int):
        wid, src = jnp.full((nt,), wid, jnp.int32), jnp.arange(nt, dtype=jnp.int32)
    assert m % tm == 0 and n % tn == 0 and wid.shape == (nt,) and src.shape == (nt,)

    def a_index(j, i, wid_ref, src_ref):
        return (src_ref[i], 0)

    def w_index(j, i, wid_ref, src_ref):
        return (wid_ref[i], 0, j)

    def o_index(j, i, wid_ref, src_ref):
        return (i, j)

    if merge is None:
        in_specs = [pl.BlockSpec((tm, a.shape[1]), a_index)]
        args = [a]
        merge_tiles = None
    else:
        assert n_sample_rows % tm == 0
        merge_tiles = n_sample_rows // tm
        half = k // 2
        s_idx = lambda j, i, *_: (jnp.minimum(i, merge_tiles - 1), 0)
        c_idx = lambda j, i, *_: (jnp.maximum(i - merge_tiles, 0), 0)
        in_specs = [pl.BlockSpec((tm, half), s_idx), pl.BlockSpec((tm, half), c_idx),
                    pl.BlockSpec((tm, half), s_idx), pl.BlockSpec((tm, half), c_idx),
                    pl.BlockSpec((None, 1, half), lambda j, i, *_: (layer, 0, 0)),
                    pl.BlockSpec((None, 1, half), lambda j, i, *_: (layer, 0, 1))]
        args = [a[0], a[1], a[2], a[3], gains, gains]
    in_specs += [pl.BlockSpec((None, k, tn), w_index) for _ in ws]
    args += list(ws)
    if mode == "resid":
        x, modr, mod_index = resid
        in_specs += [pl.BlockSpec((tm, tn), o_index),
                     pl.BlockSpec((None, 1, tn),
                                  lambda j, i, wid_ref, nv_ref: mod_index(i)[:2] + (j,))]
        args += [x, modr]
    grid_spec = pltpu.PrefetchScalarGridSpec(
        num_scalar_prefetch=2,
        grid=(n // tn, nt),
        in_specs=in_specs,
        out_specs=pl.BlockSpec((tm, tn), o_index),
        scratch_shapes=[pltpu.VMEM((k, tn), BF16) for _ in ws],
    )
    return pl.pallas_call(
        functools.partial(_gmm_kernel, mode=mode, merge_tiles=merge_tiles),
        grid_spec=grid_spec,
        out_shape=jax.ShapeDtypeStruct((m, n), out_dtype),
        compiler_params=_cparams(2),
        name=name,
    )(wid, src, *args)


def _qk_kernel(q_ref, k_ref, qg_ref, kg_ref, cos_ref, sin_ref, qo_ref, ko_ref, kc_ref, *, scale):
    cos = cos_ref[...]
    sin = sin_ref[...]

    def head(x, g):
        xn = _rms(x.astype(F32), g)
        return xn, xn * cos + pltpu.roll(xn, HEAD_DIM // 2, 1) * sin

    for h in range(q_ref.shape[1] // HEAD_DIM):
        sl = slice(h * HEAD_DIM, (h + 1) * HEAD_DIM)
        _, roped = head(q_ref[:, sl], qg_ref[...])
        qo_ref[:, sl] = (roped * scale).astype(qo_ref.dtype)
    for h in range(k_ref.shape[1] // HEAD_DIM):
        sl = slice(h * HEAD_DIM, (h + 1) * HEAD_DIM)
        xn, roped = head(k_ref[:, sl], kg_ref[...])
        ko_ref[:, sl] = roped.astype(ko_ref.dtype)
        kc_ref[:, sl] = xn


def _qk_norm_rope(proj, qg, kg, cos2, sin2, layer, a_w, kv_w, st):
    t = proj.shape[0]
    tm = st.token_tile(512)
    assert a_w % kv_w == 0
    return pl.pallas_call(
        functools.partial(_qk_kernel, scale=1.0 / math.sqrt(HEAD_DIM)),
        grid=(t // tm,),
        in_specs=[pl.BlockSpec((tm, a_w), lambda i: (i, 0)),
                  pl.BlockSpec((tm, kv_w), lambda i: (i, a_w // kv_w)),
                  pl.BlockSpec((None, 1, HEAD_DIM), lambda i: (layer, 0, 0)),
                  pl.BlockSpec((None, 1, HEAD_DIM), lambda i: (layer, 0, 0)),
                  pl.BlockSpec((tm, HEAD_DIM), lambda i: (i, 0)),
                  pl.BlockSpec((tm, HEAD_DIM), lambda i: (i, 0))],
        out_specs=[pl.BlockSpec((tm, a_w), lambda i: (i, 0)),
                   pl.BlockSpec((tm, kv_w), lambda i: (i, 0)),
                   pl.BlockSpec((tm, kv_w), lambda i: (i, 0))],
        out_shape=[jax.ShapeDtypeStruct((t, a_w), BF16),
                   jax.ShapeDtypeStruct((t, kv_w), BF16),
                   jax.ShapeDtypeStruct((t, kv_w), F32)],
        compiler_params=_cparams(1),
        name="qk_norm_rope",
    )(proj, proj, qg, kg, cos2, sin2)


def _rope_tables(st):
    pos = np.arange(st.dec_seq)
    inv = ROPE_THETA ** (-np.arange(0, HEAD_DIM // 2, 2, dtype=np.float64) / (HEAD_DIM // 2))
    ang = np.concatenate([(pos // GRID_W)[:, None] * inv, (pos % GRID_W)[:, None] * inv], axis=-1)
    cos2 = np.concatenate([np.cos(ang), np.cos(ang)], axis=-1)
    sin2 = np.concatenate([-np.sin(ang), np.sin(ang)], axis=-1)
    cos_all = np.concatenate([np.tile(cos2, (st.dec_batch, 1)), np.ones((st.tc, HEAD_DIM))])
    sin_all = np.concatenate([np.tile(sin2, (st.dec_batch, 1)), np.zeros((st.tc, HEAD_DIM))])
    return jnp.asarray(cos_all, F32), jnp.asarray(sin_all, F32)


KEY_CHUNK = 512


def _attn_kernel(*refs, g, has_cache):
    if has_cache:
        q_ref, ck_ref, cv_ref, k_ref, v_ref, o_ref = refs
    else:
        q_ref, k_ref, v_ref, o_ref = refs
    tq = q_ref.shape[0]
    q = jnp.concatenate([q_ref[:, h * HEAD_DIM:(h + 1) * HEAD_DIM] for h in range(g)], axis=0)
    lk = k_ref.shape[0]
    parts = []
    if has_cache:
        parts.append((ck_ref, cv_ref, 0, ck_ref.shape[0]))
    c0 = 0
    while c0 < lk:
        n = min(KEY_CHUNK, lk - c0)
        parts.append((k_ref, v_ref, c0, n))
        c0 += n
    m = acc = None
    for kr, vr, c0, n in parts:
        kc = kr[c0:c0 + n, :].astype(BF16)
        va = jnp.concatenate([vr[c0:c0 + n, :].astype(BF16), jnp.ones((n, HEAD_DIM), BF16)], axis=1)
        s = lax.dot_general(q, kc, (((1,), (1,)), ((), ())), preferred_element_type=F32)
        mc = jnp.max(s, axis=-1, keepdims=True)
        if m is None:
            m = mc
            acc = jnp.dot(jnp.exp(s - m).astype(BF16), va, preferred_element_type=F32)
        else:
            m_new = jnp.maximum(m, mc)
            acc = (jnp.exp(m - m_new) * acc
                   + jnp.dot(jnp.exp(s - m_new).astype(BF16), va, preferred_element_type=F32))
            m = m_new
    o = acc[:, :HEAD_DIM] / acc[:, HEAD_DIM:]
    for h in range(g):
        o_ref[:, h * HEAD_DIM:(h + 1) * HEAD_DIM] = o[h * tq:(h + 1) * tq].astype(o_ref.dtype)


def _attention(qn, kn, proj, v_col0, n_kv, nseq, seq_len, row0, cache=None):
    aw = qn.shape[1]
    g = aw // HEAD_DIM // n_kv
    tq = _pick(seq_len, 256, 16)
    nq = seq_len // tq
    assert row0 % seq_len == 0 and v_col0 % HEAD_DIM == 0
    in_specs = [pl.BlockSpec((tq, g * HEAD_DIM), lambda b, kv, i: (row0 // tq + b * nq + i, kv))]
    args = [qn]
    if cache is not None:
        ck, cv, layer = cache
        past = ck.shape[2]
        cspec = pl.BlockSpec((None, None, past, HEAD_DIM), lambda b, kv, i: (b, layer, 0, kv))
        in_specs += [cspec, cspec]
        args += [ck, cv]
    in_specs += [pl.BlockSpec((seq_len, HEAD_DIM), lambda b, kv, i: (row0 // seq_len + b, kv)),
                 pl.BlockSpec((seq_len, HEAD_DIM),
                              lambda b, kv, i: (row0 // seq_len + b, v_col0 // HEAD_DIM + kv))]
    args += [kn, proj]
    return pl.pallas_call(
        functools.partial(_attn_kernel, g=g, has_cache=cache is not None),
        grid=(nseq, n_kv, nq),
        in_specs=in_specs,
        out_specs=pl.BlockSpec((tq, g * HEAD_DIM), lambda b, kv, i: (b * nq + i, kv)),
        out_shape=jax.ShapeDtypeStruct((nseq * seq_len, aw), BF16),
        compiler_params=_cparams(3),
        name="attention",
    )(*args)


def _filter_kernel(feat_ref, w1_ref, b1_ref, w2_ref, b2_ref, w3_ref, ld_ref, o_ref, h_ref, *,
                   seq_len, hy_w):
    hp = lax.Precision.HIGHEST

    @pl.when(pl.program_id(1) == 0)
    def _hidden():
        h = jnp.sin(jnp.dot(feat_ref[...], w1_ref[...], precision=hp, preferred_element_type=F32)
                    + b1_ref[...])
        h_ref[...] = jnp.sin(jnp.dot(h, w2_ref[...], precision=hp, preferred_element_type=F32)
                             + b2_ref[...])

    h = jnp.dot(h_ref[...].astype(BF16), w3_ref[...].astype(BF16), preferred_element_type=F32)
    tl, tn = h.shape
    row = lax.broadcasted_iota(jnp.int32, (tl, 1), 0) + pl.program_id(0) * tl
    t = row.astype(F32) / seq_len
    h = h * jnp.exp(-jnp.exp(ld_ref[...]) * t)
    col = lax.broadcasted_iota(jnp.int32, (1, tn), 1) + pl.program_id(1) * tn
    backward = (col // hy_w) % 2 == 1
    o_ref[...] = jnp.where(row == 0, jnp.where(backward, 0.0, h), h).astype(o_ref.dtype)


def _filter_features(seq_len):
    t = np.arange(seq_len, dtype=np.float32) / np.float32(seq_len)
    fr = np.arange(1, POS_FREQS + 1, dtype=np.float32)
    ang = float(np.float32(2.0 * math.pi)) * t[:, None].astype(np.float64) * fr
    feat = np.concatenate([t[:, None], np.sin(ang), np.cos(ang)], axis=-1)
    out = np.zeros((seq_len, LANES), np.float32)
    out[:, :feat.shape[1]] = feat
    return jnp.asarray(out)


def _pad2(x, rows, cols):
    return jnp.pad(x, ((0, rows - x.shape[0]), (0, cols - x.shape[1])))


def _hyena_filters(seq_len, layer, p, hy_w):
    n = p["filt_w3"].shape[-1]
    feat = _filter_features(seq_len)
    w1 = _pad2(p["filt_w1"][layer], LANES, LANES)
    b1 = _pad2(p["filt_b1"][layer][None, :], 1, LANES)
    w2 = _pad2(p["filt_w2"][layer], LANES, LANES)
    b2 = _pad2(p["filt_b2"][layer][None, :], 1, LANES)
    w3 = _pad2(p["filt_w3"][layer], LANES, n)
    ld = p["filt_log_decay"][layer][None, :]
    tl = _pick(seq_len, 1024, 16)
    tn = _pick(n, 2048, LANES)
    return pl.pallas_call(
        functools.partial(_filter_kernel, seq_len=seq_len, hy_w=hy_w),
        grid=(seq_len // tl, n // tn),
        in_specs=[pl.BlockSpec((tl, LANES), lambda i, j: (i, 0)),
                  pl.BlockSpec((LANES, LANES), lambda i, j: (0, 0)),
                  pl.BlockSpec((1, LANES), lambda i, j: (0, 0)),
                  pl.BlockSpec((LANES, LANES), lambda i, j: (0, 0)),
                  pl.BlockSpec((1, LANES), lambda i, j: (0, 0)),
                  pl.BlockSpec((LANES, tn), lambda i, j: (0, j)),
                  pl.BlockSpec((1, tn), lambda i, j: (0, j))],
        out_specs=pl.BlockSpec((tl, tn), lambda i, j: (i, j)),
        out_shape=jax.ShapeDtypeStruct((seq_len, n), BF16),
        scratch_shapes=[pltpu.VMEM((tl, LANES), F32)],
        compiler_params=_cparams(2),
        name="hy_filter",
    )(feat, w1, b1, w2, b2, w3, ld)


class _Dft:
    def __init__(self, seq_len, radix):
        self.seq_len, self.r = seq_len, radix
        n = 2 * seq_len
        self.n1 = n // radix
        self.la = seq_len // radix
        k1 = self.n1 // 2 + 1
        self.k1p = _roundup(k1, 16)
        a = np.arange(self.la)[None, :]
        kk = np.arange(self.k1p)[:, None]
        valid = kk < k1
        th = 2.0 * math.pi * kk * a / self.n1
        fwd = np.concatenate([np.where(valid, np.cos(th), 0.0), np.where(valid, -np.sin(th), 0.0)])
        wgt = np.where((kk == 0) | (kk == self.n1 // 2), 1.0, 2.0) * valid / n
        inv = np.concatenate([wgt * np.cos(th), -wgt * np.sin(th)]).T
        self.fwd = jnp.asarray(fwd, F32)
        self.inv = jnp.asarray(inv, F32)
        ph = 2.0 * math.pi * kk / n
        ones = np.ones((1, LANES))
        self.tw_re = jnp.asarray(np.cos(ph) * ones, F32)
        self.tw_im = jnp.asarray(-np.sin(ph) * ones, F32)


def _cmul_const(x, w):
    wr, wi = w.real, w.imag
    if abs(wi) < 1e-12:
        if abs(wr - 1.0) < 1e-12:
            return x
        if abs(wr + 1.0) < 1e-12:
            return (-x[0], -x[1])
    if abs(wr) < 1e-12:
        if abs(wi - 1.0) < 1e-12:
            return (-x[1], x[0])
        if abs(wi + 1.0) < 1e-12:
            return (x[1], -x[0])
    return (x[0] * wr - x[1] * wi, x[0] * wi + x[1] * wr)


def _fft_list(xs, sign):
    n = len(xs)
    if n == 1:
        return xs
    ev = _fft_list(xs[0::2], sign)
    od = _fft_list(xs[1::2], sign)
    out = [None] * n
    for k in range(n // 2):
        t = _cmul_const(od[k], cmath.exp(sign * 2j * math.pi * k / n))
        out[k] = (ev[k][0] + t[0], ev[k][1] + t[1])
        out[k + n // 2] = (ev[k][0] - t[0], ev[k][1] - t[1])
    return out


ROW_CHUNK = 16
LONGCONV_ROWS = 1024


def _lane_tile(x, width):
    return x if width == LANES else jnp.concatenate([x] * (width // LANES), axis=1)


def _twiddles(tw_re_ref, tw_im_ref, rows, r, tc):
    if r == 1:
        return []
    w1 = (tw_re_ref[rows, :], tw_im_ref[rows, :])
    ws = [w1]
    for _ in range(r - 2):
        pr, pi = ws[-1]
        ws.append((pr * w1[0] - pi * w1[1], pr * w1[1] + pi * w1[0]))
    return [(_lane_tile(wr, tc), _lane_tile(wi, tc)) for wr, wi in ws]


def _spectrum_chunk(y_ref, tws, c0, k1p, r):
    rows = pl.ds(c0, ROW_CHUNK)
    rows_im = pl.ds(k1p + c0, ROW_CHUNK)
    ts = []
    for b in range(r):
        yr = y_ref[b, rows, :]
        yi = y_ref[b, rows_im, :]
        if b == 0:
            ts.append((yr, yi))
        else:
            wr, wi = tws[b - 1]
            ts.append((yr * wr - yi * wi, yr * wi + yi * wr))
    return _fft_list(ts, -1)


def _lanes(q):
    return slice(q * LANES, (q + 1) * LANES)


def _conv3_group(u_ref, w_ref, b_ref, q, seq):
    x = u_ref[seq, _lanes(q)].astype(F32)
    n = x.shape[0]
    row = lax.broadcasted_iota(jnp.int32, x.shape, 0)
    prev = jnp.where(row == 0, 0.0, pltpu.roll(x, 1, 0))
    nxt = jnp.where(row == n - 1, 0.0, pltpu.roll(x, n - 1, 0))
    w = w_ref[:, _lanes(q)]
    return prev * w[0:1] + x * w[1:2] + nxt * w[2:3] + b_ref[:, _lanes(q)]


def _stage1(f_ref, load_group, xf_ref, y_ref, r, la):
    nq = xf_ref.shape[0]
    for q in range(nq):
        xf_ref[q] = load_group(q)
    f = f_ref[...].astype(BF16)
    for b in range(r):
        xb = jnp.concatenate([xf_ref[q, pl.ds(b, la, stride=r), :] for q in range(nq)], axis=1)
        y_ref[b] = jnp.dot(f, xb.astype(BF16), preferred_element_type=F32)


def _spectrum_kernel(*refs, r, k1p, la):
    if r > 1:
        f_ref, tw_re_ref, tw_im_ref, hf_ref, hb_ref, kr_ref, ki_ref, xf_ref, y_ref, acc_ref = refs
    else:
        f_ref, hf_ref, hb_ref, kr_ref, ki_ref, xf_ref, y_ref, acc_ref = refs
        tw_re_ref = tw_im_ref = None
    tc = y_ref.shape[-1]

    def sweep(src_ref, backward):
        _stage1(f_ref, lambda q: src_ref[:, _lanes(q)].astype(F32), xf_ref, y_ref, r, la)

        def chunk(c, carry):
            c0 = pl.multiple_of(c * ROW_CHUNK, ROW_CHUNK)
            rows = pl.ds(c0, ROW_CHUNK)
            rows_im = pl.ds(k1p + c0, ROW_CHUNK)
            x = _spectrum_chunk(y_ref, _twiddles(tw_re_ref, tw_im_ref, rows, r, tc), c0, k1p, r)
            for k2 in range(r):
                if backward:
                    kr_ref[k2, rows, :] = (acc_ref[k2, rows, :] + x[k2][0]).astype(kr_ref.dtype)
                    ki_ref[k2, rows, :] = (acc_ref[k2, rows_im, :] - x[k2][1]).astype(ki_ref.dtype)
                else:
                    acc_ref[k2, rows, :] = x[k2][0]
                    acc_ref[k2, rows_im, :] = x[k2][1]
            return carry

        lax.fori_loop(0, k1p // ROW_CHUNK, chunk, 0)

    sweep(hf_ref, False)
    sweep(hb_ref, True)


def _filter_spectrum(taps, dft, hy_w):
    r, la, k1p, seq_len = dft.r, dft.la, dft.k1p, dft.seq_len
    tc = _pick(hy_w, 256, LANES)
    nct = hy_w // tc

    def tap_spec(direction):
        return pl.BlockSpec((seq_len, tc), lambda o, j: (0, (o * 2 + direction) * nct + j))

    const = [pl.BlockSpec(dft.fwd.shape, lambda o, j: (0, 0))]
    cargs = [dft.fwd]
    if r > 1:
        const += [pl.BlockSpec(dft.tw_re.shape, lambda o, j: (0, 0))] * 2
        cargs += [dft.tw_re, dft.tw_im]
    out_spec = pl.BlockSpec((None, r, k1p, tc), lambda o, j: (o, 0, 0, j))
    shape = jax.ShapeDtypeStruct((HY_ORDER, r, k1p, hy_w), BF16)
    return pl.pallas_call(
        functools.partial(_spectrum_kernel, r=r, k1p=k1p, la=la),
        grid=(HY_ORDER, nct),
        in_specs=const + [tap_spec(0), tap_spec(1)],
        out_specs=[out_spec, out_spec],
        out_shape=[shape, shape],
        scratch_shapes=[pltpu.VMEM((tc // LANES, seq_len, LANES), F32),
                        pltpu.VMEM((r, 2 * k1p, tc), F32),
                        pltpu.VMEM((r, 2 * k1p, tc), F32)],
        compiler_params=_cparams(2),
        name="hy_filter_spectrum",
    )(*cargs, taps, taps)


def _longconv_kernel(*refs, n_sub, **kw):
    seq_len = refs[-3].shape[1]
    for sub in range(n_sub):
        _longconv_one(refs, slice(sub * seq_len, (sub + 1) * seq_len), **kw)


def _longconv_one(refs, seq, *, r, k1p, la, z_conv):
    refs = list(refs)
    f_ref, fi_ref = refs[:2]
    del refs[:2]
    tw_re_ref = tw_im_ref = None
    if r > 1:
        tw_re_ref, tw_im_ref = refs[:2]
        del refs[:2]
    z_ref = refs.pop(0)
    if z_conv:
        zw_ref, zb_ref = refs[:2]
        del refs[:2]
    gate_ref, gw_ref, gb_ref, skip_ref, kr_ref, ki_ref, o_ref, xf_ref, y_ref, g_ref = refs
    tc = y_ref.shape[-1]
    if z_conv:
        load_z = lambda q: _conv3_group(z_ref, zw_ref, zb_ref, q, seq)
    else:
        load_z = lambda q: z_ref[seq, _lanes(q)].astype(F32)
    _stage1(f_ref, load_z, xf_ref, y_ref, r, la)

    def chunk(c, carry):
        c0 = pl.multiple_of(c * ROW_CHUNK, ROW_CHUNK)
        rows = pl.ds(c0, ROW_CHUNK)
        rows_im = pl.ds(k1p + c0, ROW_CHUNK)
        tws = _twiddles(tw_re_ref, tw_im_ref, rows, r, tc)
        x = _spectrum_chunk(y_ref, tws, c0, k1p, r)
        prod = []
        for k2 in range(r):
            kr = kr_ref[k2, rows, :].astype(F32)
            ki = ki_ref[k2, rows, :].astype(F32)
            prod.append((x[k2][0] * kr - x[k2][1] * ki, x[k2][0] * ki + x[k2][1] * kr))
        s = _fft_list(prod, +1)
        for b in range(r):
            if b == 0:
                gr, gi = s[0]
            else:
                wr, wi = tws[b - 1]
                gr = s[b][0] * wr + s[b][1] * wi
                gi = s[b][1] * wr - s[b][0] * wi
            g_ref[b, rows, :] = gr.astype(g_ref.dtype)
            g_ref[b, rows_im, :] = gi.astype(g_ref.dtype)
        return carry

    lax.fori_loop(0, k1p // ROW_CHUNK, chunk, 0)
    fi = fi_ref[...].astype(BF16)
    for b in range(r):
        yb = jnp.dot(fi, g_ref[b], preferred_element_type=F32)
        rows = pl.ds(b, la, stride=r)
        for q in range(xf_ref.shape[0]):
            xf_ref[q, rows, :] = yb[:, _lanes(q)] + skip_ref[:, _lanes(q)] * xf_ref[q, rows, :]
    for q in range(xf_ref.shape[0]):
        gate = _conv3_group(gate_ref, gw_ref, gb_ref, q, seq)
        o_ref[seq, _lanes(q)] = (gate * xf_ref[q]).astype(o_ref.dtype)


def _longconv(proj, u_col0, row0, layer, conv_w, conv_b, z_prev, order, skip, kf_re, kf_im, dft,
              nseq, hy_w):
    r, la, k1p, seq_len = dft.r, dft.la, dft.k1p, dft.seq_len
    tc = _pick(math.gcd(hy_w, u_col0), 256 if r > 1 else 1024, LANES)
    n_sub = _pick(math.gcd(nseq, row0 // seq_len) or nseq, max(1, LONGCONV_ROWS // seq_len), 1)
    blk = n_sub * seq_len
    assert row0 % blk == 0
    nct = hy_w // tc
    seq0 = row0 // blk
    z_conv = z_prev is None

    def u_specs(part):
        return [pl.BlockSpec((blk, tc), lambda s, j: (seq0 + s, u_col0 // tc + part * nct + j)),
                pl.BlockSpec((None, 3, tc), lambda s, j: (layer, 0, part * nct + j)),
                pl.BlockSpec((None, 1, tc), lambda s, j: (layer, 0, part * nct + j))]

    cdim = lambda s, j: (0, 0)
    in_specs = [pl.BlockSpec(dft.fwd.shape, cdim), pl.BlockSpec(dft.inv.shape, cdim)]
    args = [dft.fwd, dft.inv]
    if r > 1:
        in_specs += [pl.BlockSpec(dft.tw_re.shape, cdim)] * 2
        args += [dft.tw_re, dft.tw_im]
    if z_conv:
        in_specs += u_specs(0)
        args += [proj, conv_w, conv_b]
    else:
        in_specs.append(pl.BlockSpec((blk, tc), lambda s, j: (s, j)))
        args.append(z_prev)
    in_specs += u_specs(1 + order)
    args += [proj, conv_w, conv_b]
    in_specs += [pl.BlockSpec((None, 1, tc), lambda s, j: (layer * HY_ORDER + order, 0, j)),
                 pl.BlockSpec((None, r, k1p, tc), lambda s, j: (order, 0, 0, j)),
                 pl.BlockSpec((None, r, k1p, tc), lambda s, j: (order, 0, 0, j))]
    args += [skip, kf_re, kf_im]
    return pl.pallas_call(
        functools.partial(_longconv_kernel, n_sub=n_sub, r=r, k1p=k1p, la=la, z_conv=z_conv),
        grid=(nseq // n_sub, nct),
        in_specs=in_specs,
        out_specs=pl.BlockSpec((blk, tc), lambda s, j: (s, j)),
        out_shape=jax.ShapeDtypeStruct((nseq * seq_len, hy_w), BF16),
        scratch_shapes=[pltpu.VMEM((tc // LANES, seq_len, LANES), F32),
                        pltpu.VMEM((r, 2 * k1p, tc), F32),
                        pltpu.VMEM((r, 2 * k1p, tc), BF16)],
        compiler_params=_cparams(2),
        name="hy_longconv",
    )(*args)


def _hyena(proj, layer, p, st, u_col0, hy_w, dfts):
    skip = p["hy_skip"].reshape(-1, 1, hy_w)
    conv_w = p["hy_conv_w"]
    conv_b = p["hy_conv_b"].reshape(-1, 1, 3 * hy_w)
    outs = []
    for nseq, seq_len, row0 in ((st.dec_batch, st.dec_seq, 0), (st.batch, st.seq, st.ts)):
        dft = dfts[seq_len]
        taps = _hyena_filters(seq_len, layer, p, hy_w)
        kf_re, kf_im = _filter_spectrum(taps, dft, hy_w)
        z = None
        for order in range(HY_ORDER):
            z = _longconv(proj, u_col0, row0, layer, conv_w, conv_b, z, order, skip, kf_re, kf_im,
                          dft, nseq, hy_w)
        outs.append(z)
    return outs


def _route_kernel(x_ref, g_ref, sc_ref, sh_ref, rw_ref, hp_ref, w1_ref, w2_ref, code_ref,
                  rank_ref, cnt_ref, carry_ref, *, n_experts):
    i = pl.program_id(0)

    @pl.when(i == 0)
    def _init():
        carry_ref[...] = jnp.zeros_like(carry_ref)

    h = _rms(x_ref[...], g_ref[...]) * (1.0 + sc_ref[...]) + sh_ref[...]
    hp_ref[...] = h

    rw = rw_ref[...]
    h_hi, rw_hi = h.astype(BF16), rw.astype(BF16)
    h_lo = (h - h_hi.astype(F32)).astype(BF16)
    rw_lo = (rw - rw_hi.astype(F32)).astype(BF16)
    logits = (jnp.dot(h_hi, rw_hi, preferred_element_type=F32)
              + (jnp.dot(h_hi, rw_lo, preferred_element_type=F32)
                 + jnp.dot(h_lo, rw_hi, preferred_element_type=F32)))
    lane = lax.broadcasted_iota(jnp.int32, logits.shape, 1)
    neg = jnp.float32(-jnp.inf)
    logits = jnp.where(lane < n_experts, logits, neg)
    m1 = jnp.max(logits, axis=-1, keepdims=True)
    i1 = jnp.min(jnp.where(logits == m1, lane, LANES), axis=-1, keepdims=True)
    rest = jnp.where(lane == i1, neg, logits)
    m2 = jnp.max(rest, axis=-1, keepdims=True)
    i2 = jnp.min(jnp.where(rest == m2, lane, LANES), axis=-1, keepdims=True)
    e2 = jnp.exp(m2 - m1)
    w1 = 1.0 / (1.0 + e2)
    w2 = e2 / (1.0 + e2)
    w1_ref[...] = jnp.broadcast_to(w1, w1_ref.shape)
    w2_ref[...] = jnp.broadcast_to(w2, w2_ref.shape)

    code = jnp.where(lane == i1, 1.0, jnp.where(lane == i2, 2.0, 0.0))
    code_t = code.T[:code_ref.shape[0], :]
    code_ref[...] = code_t.astype(jnp.int32)
    sel = (code_t > 0.0).astype(BF16)
    tm = sel.shape[1]
    upper = (lax.broadcasted_iota(jnp.int32, (tm, tm), 0)
             < lax.broadcasted_iota(jnp.int32, (tm, tm), 1)).astype(BF16)
    excl = jnp.dot(sel, upper, preferred_element_type=F32)
    carry = carry_ref[...]
    rank_ref[...] = (excl + carry[:, :1]).astype(jnp.int32)
    carry = carry + jnp.sum(sel.astype(F32), axis=-1, keepdims=True)
    carry_ref[...] = carry
    cnt_ref[...] = carry.astype(jnp.int32)


def _route(x, g, modr, rw, st, layer, n_experts):
    t, d = x.shape
    tm = st.token_tile(512)
    assert tm % LANES == 0 and n_experts <= 8
    rwp = _pad2(rw, d, LANES)
    row_spec = pl.BlockSpec((tm, LANES), lambda i: (i, 0))
    col_spec = pl.BlockSpec((8, tm), lambda i: (0, i))
    return pl.pallas_call(
        functools.partial(_route_kernel, n_experts=n_experts),
        grid=(t // tm,),
        in_specs=[pl.BlockSpec((tm, d), lambda i: (i, 0)),
                  pl.BlockSpec((None, 1, d), lambda i: (layer, 0, 0)),
                  pl.BlockSpec((None, 1, d), _mod_index(st, layer, 4, tm)),
                  pl.BlockSpec((None, 1, d), _mod_index(st, layer, 3, tm)),
                  pl.BlockSpec((d, LANES), lambda i: (0, 0))],
        out_specs=[pl.BlockSpec((tm, d), lambda i: (i, 0)), row_spec, row_spec,
                   col_spec, col_spec, pl.BlockSpec((8, LANES), lambda i: (0, 0))],
        out_shape=[jax.ShapeDtypeStruct((t, d), F32),
                   jax.ShapeDtypeStruct((t, LANES), F32),
                   jax.ShapeDtypeStruct((t, LANES), F32),
                   jax.ShapeDtypeStruct((8, t), jnp.int32),
                   jax.ShapeDtypeStruct((8, t), jnp.int32),
                   jax.ShapeDtypeStruct((8, LANES), jnp.int32)],
        scratch_shapes=[pltpu.VMEM((8, LANES), F32)],
        compiler_params=_cparams(1),
        name="moe_route",
    )(x, g, modr, modr, rwp)


DISPATCH_ROWS = 512


GATHER_UNROLL = 8


def _issue_row_gather(idx_ref, src_ref, dst_ref, sem, priorities):
    def trip(g, carry):
        for u in range(GATHER_UNROLL):
            rr = g * GATHER_UNROLL + u
            pltpu.make_async_copy(src_ref.at[pl.ds(idx_ref[0, rr], 1)], dst_ref.at[pl.ds(rr, 1)],
                                  sem).start(priority=priorities[u % len(priorities)])
        return carry

    lax.fori_loop(0, dst_ref.shape[0] // GATHER_UNROLL, trip, 0)


def _wait_row_gather(dst_ref, sem):
    pltpu.make_async_copy(dst_ref, dst_ref, sem).wait()


def _pipelined_gather(idx_refs, next_idx_refs, src_ref, bufs, sems, n_steps, priorities):
    i = pl.program_id(0)
    cur = i % 2

    @pl.when(jnp.logical_and(i == 0, n_steps > 0))
    def _first():
        for idx_ref, buf in zip(idx_refs, bufs):
            _issue_row_gather(idx_ref, src_ref, buf.at[0], sems.at[0], priorities)

    @pl.when(i + 1 < n_steps)
    def _prefetch():
        for idx_ref, buf in zip(next_idx_refs, bufs):
            _issue_row_gather(idx_ref, src_ref, buf.at[1 - cur], sems.at[1 - cur], priorities)

    @pl.when(i < n_steps)
    def _wait():
        for buf in bufs:
            _wait_row_gather(buf.at[cur], sems.at[cur])

    return cur


def _dispatch_kernel(nv_ref, tok_ref, tok_next_ref, h_ref, xs_ref, rows_ref, sems):
    n_steps = nv_ref[0]
    cur = _pipelined_gather([tok_ref], [tok_next_ref], h_ref, [rows_ref], sems, n_steps, (1,))

    @pl.when(pl.program_id(0) < n_steps)
    def _used():
        xs_ref[...] = rows_ref[cur].astype(xs_ref.dtype)

    @pl.when(pl.program_id(0) >= n_steps)
    def _unused():
        xs_ref[...] = jnp.zeros_like(xs_ref)


def _index_specs(nt, td):
    return [pl.BlockSpec((None, 1, td), lambda i, *_: (i, 0, 0), memory_space=pltpu.SMEM),
            pl.BlockSpec((None, 1, td), lambda i, *_: (jnp.minimum(i + 1, nt - 1), 0, 0),
                         memory_space=pltpu.SMEM)]


def _dispatch(h, slot_token, n_used_slots):
    n_slots = slot_token.shape[0]
    d = h.shape[1]
    td = _pick(n_slots, DISPATCH_ROWS, LANES)
    nt = n_slots // td
    tok = slot_token.reshape(nt, 1, td)
    grid_spec = pltpu.PrefetchScalarGridSpec(
        num_scalar_prefetch=1,
        grid=(nt,),
        in_specs=_index_specs(nt, td) + [pl.BlockSpec(memory_space=pl.ANY)],
        out_specs=pl.BlockSpec((td, d), lambda i, nv: (i, 0)),
        scratch_shapes=[pltpu.VMEM((2, td, d), F32), pltpu.SemaphoreType.DMA((2,))],
    )
    return pl.pallas_call(
        _dispatch_kernel,
        grid_spec=grid_spec,
        out_shape=jax.ShapeDtypeStruct((n_slots, d), BF16),
        compiler_params=_cparams(1),
        name="moe_dispatch",
    )((n_used_slots + td - 1) // td, tok, tok, h)


def _combine_kernel(*refs, post):
    (s0_ref, s0n_ref, s1_ref, s1n_ref, ys_ref, x_ref, gate_ref, w1_ref, w2_ref) = refs[:9]
    rest = refs[9:]
    n_post = {"none": 0, "norm_mod": 3, "final": 1}[post]
    post_refs, rest = rest[:n_post], rest[n_post:]
    n_out = 1 if post in ("none", "final") else 2
    outs, (ya, yb, sems) = rest[:n_out], rest[n_out:]
    cur = _pipelined_gather([s0_ref, s1_ref], [s0n_ref, s1n_ref], ys_ref, [ya, yb], sems,
                            pl.num_programs(0), (0, 1))
    w1 = _lane_tile(w1_ref[...], ya.shape[-1])
    w2 = _lane_tile(w2_ref[...], ya.shape[-1])
    xn = x_ref[...] + gate_ref[...] * (w1 * ya[cur] + w2 * yb[cur])
    if post == "final":
        outs[0][...] = _rms(xn, post_refs[0][...])
    else:
        outs[0][...] = xn
    if post == "norm_mod":
        g_ref, sc_ref, sh_ref = post_refs
        outs[1][...] = (_rms(xn, g_ref[...]) * (1.0 + sc_ref[...]) + sh_ref[...]).astype(outs[1].dtype)


def _combine(ys, slot0, slot1, x, modr, w1, w2, st, layer, post, post_args=()):
    t, d = x.shape
    td = _pick(math.gcd(st.dec_seq, st.tc), DISPATCH_ROWS, LANES)
    nt = t // td
    row = pl.BlockSpec((td, d), lambda i: (i, 0))
    in_specs = (_index_specs(nt, td) + _index_specs(nt, td)
                + [pl.BlockSpec(memory_space=pl.ANY), row,
                   pl.BlockSpec((None, 1, d), _mod_index(st, layer, 5, td)),
                   pl.BlockSpec((td, LANES), lambda i: (i, 0)),
                   pl.BlockSpec((td, LANES), lambda i: (i, 0))])
    s0 = slot0.reshape(nt, 1, td)
    s1 = slot1.reshape(nt, 1, td)
    args = [s0, s0, s1, s1, ys, x, modr, w1, w2]
    out_specs, out_shape = [row], [jax.ShapeDtypeStruct((t, d), F32)]
    if post == "norm_mod":
        in_specs += [pl.BlockSpec((None, 1, d), lambda i: (layer + 1, 0, 0)),
                     pl.BlockSpec((None, 1, d), _mod_index(st, layer + 1, 1, td)),
                     pl.BlockSpec((None, 1, d), _mod_index(st, layer + 1, 0, td))]
        args += [post_args[0], modr, modr]
        out_specs.append(row)
        out_shape.append(jax.ShapeDtypeStruct((t, d), BF16))
    elif post == "final":
        in_specs.append(pl.BlockSpec((1, d), lambda i: (0, 0)))
        args.append(post_args[0])
    return pl.pallas_call(
        functools.partial(_combine_kernel, post=post),
        grid=(nt,),
        in_specs=in_specs,
        out_specs=out_specs,
        out_shape=out_shape,
        scratch_shapes=[pltpu.VMEM((2, td, d), F32), pltpu.VMEM((2, td, d), F32),
                        pltpu.SemaphoreType.DMA((2,))],
        compiler_params=_cparams(1),
        name="moe_combine",
    )(*args)


MOE_ROWS = 1024
MOE_DOWN_SPLIT = 2


def _tile_sources(tile_row, tile_expert, starts, counts):
    active = tile_row - starts[tile_expert] < counts[tile_expert]
    idx = jnp.arange(tile_row.shape[0], dtype=jnp.int32)
    return lax.cummax(jnp.where(active, idx, 0)).astype(jnp.int32)


def _moe_layer(x, p, modr, st, layer, moe_idx, post, post_args):
    t, d = x.shape
    n_experts = p["router_w"].shape[-1]
    d_ff = p["moe_w_gate"].shape[-1]
    tm = MOE_ROWS
    hp, w1, w2, code, rank, cnt = _route(x, p["norm2_g"].reshape(-1, 1, d), modr,
                                         p["router_w"][moe_idx], st, layer, n_experts)
    counts = cnt[:n_experts, 0]
    padded = (counts + tm - 1) // tm * tm
    ends = jnp.cumsum(padded)
    starts = ends - padded
    n_tiles = pl.cdiv(2 * t, tm) + n_experts
    n_slots = n_tiles * tm
    dest = starts[:, None] + rank[:n_experts]
    slot0 = jnp.sum(jnp.where(code[:n_experts] == 1, dest, 0), axis=0).astype(jnp.int32)
    slot1 = jnp.sum(jnp.where(code[:n_experts] == 2, dest, 0), axis=0).astype(jnp.int32)
    tile_row = jnp.arange(n_tiles, dtype=jnp.int32) * tm
    tile_expert = jnp.minimum(jnp.sum(tile_row[:, None] >= ends[None, :], axis=1), n_experts - 1)
    wid = (moe_idx * n_experts + tile_expert).astype(jnp.int32)
    expert_ids = jnp.arange(n_experts, dtype=jnp.int32)[:, None]
    token_ids = jnp.arange(t, dtype=jnp.int32)
    e0 = jnp.sum(jnp.where(code[:n_experts] == 1, expert_ids, 0), axis=0)
    e1 = jnp.sum(jnp.where(code[:n_experts] == 2, expert_ids, 0), axis=0)
    pair_token = jnp.sort(jnp.concatenate([e0 * t + token_ids, e1 * t + token_ids])) % t
    slot_expert = jnp.repeat(tile_expert, tm)
    local = jnp.arange(n_slots, dtype=jnp.int32) - starts[slot_expert]
    compact_start = jnp.cumsum(counts) - counts
    compact = compact_start[slot_expert] + jnp.clip(local, 0, jnp.maximum(counts[slot_expert] - 1, 0))
    slot_token = pair_token[jnp.clip(compact, 0, 2 * t - 1)].astype(jnp.int32)

    xs = _dispatch(hp, slot_token, ends[-1:].astype(jnp.int32))
    wg = p["moe_w_gate"].reshape(-1, d, d_ff)
    wu = p["moe_w_up"].reshape(-1, d, d_ff)
    wd = p["moe_w_down"].reshape(-1, d_ff, d)
    hidden = _gmm(xs, (wg, wu), wid, _tile_sources(tile_row, tile_expert, starts, counts),
                  mode="swiglu", out_dtype=BF16, tm=tm, tn=_pick(d_ff, 512, LANES),
                  name="moe_gate_up")
    tm_d = tm // MOE_DOWN_SPLIT
    sub_expert = jnp.repeat(tile_expert, MOE_DOWN_SPLIT)
    sub_row = jnp.arange(n_tiles * MOE_DOWN_SPLIT, dtype=jnp.int32) * tm_d
    ys = _gmm(hidden, (wd,), jnp.repeat(wid, MOE_DOWN_SPLIT),
              _tile_sources(sub_row, sub_expert, starts, counts), mode="plain", out_dtype=F32,
              tm=tm_d, tn=_pick(d, 512, LANES), name="moe_down")
    return _combine(ys, slot0, slot1, x, modr, w1, w2, st, layer, post, post_args)


def _dense_layer(x, p, modr, st, layer, ffn_idx):
    t, d = x.shape
    d_ff = p["ffn_w_gate"].shape[-1]
    h = _norm_mod(x, p["norm2_g"].reshape(-1, 1, d), modr, st, layer, 4, 3)
    hidden = _gmm(h, (p["ffn_w_gate"], p["ffn_w_up"]), ffn_idx, None, mode="swiglu",
                  out_dtype=BF16, tm=st.token_tile(1024), tn=_pick(d_ff, 512, LANES),
                  name="ffn_gate_up")
    tm = st.token_tile(512)
    return _gmm(hidden, (p["ffn_w_down"],), ffn_idx, None, mode="resid", out_dtype=F32, tm=tm,
                tn=_pick(d, 512, LANES), resid=(x, modr, _mod_index(st, layer, 5, tm)),
                name="ffn_down")


def _dft_radix(seq_len):
    r = 1
    while seq_len // r > 512 and r < 8:
        r *= 2
    return r


def kernel(x_prompt, x_sample, cache_k, cache_v, c, c_ctx, norm1_g, norm2_g, w_mod, b_mod, w_in,
           q_norm_g, k_norm_g, hy_conv_w, hy_conv_b, filt_w1, filt_b1, filt_w2, filt_b2, filt_w3,
           filt_log_decay, hy_skip, out_norm_g, w_out, ffn_w_gate, ffn_w_up, ffn_w_down, router_w,
           moe_w_gate, moe_w_up, moe_w_down, final_norm_g):
    p = dict(norm2_g=norm2_g, hy_conv_w=hy_conv_w, hy_conv_b=hy_conv_b, filt_w1=filt_w1,
             filt_b1=filt_b1, filt_w2=filt_w2, filt_b2=filt_b2, filt_w3=filt_w3,
             filt_log_decay=filt_log_decay, hy_skip=hy_skip, ffn_w_gate=ffn_w_gate,
             ffn_w_up=ffn_w_up, ffn_w_down=ffn_w_down, router_w=router_w, moe_w_gate=moe_w_gate,
             moe_w_up=moe_w_up, moe_w_down=moe_w_down)
    batch, seq, d = x_prompt.shape
    dec_batch, dec_seq, _ = x_sample.shape
    depth = w_in.shape[0]
    in_w = w_in.shape[-1]
    n_kv = cache_k.shape[3]
    kv_w = n_kv * HEAD_DIM
    a_w = (3 * d + 2 * kv_w - in_w) // 2
    hy_w = d - a_w
    u_col0 = a_w + 2 * kv_w
    past = cache_k.shape[2]
    st = _Stream(dec_batch, dec_seq, batch, seq)
    assert dec_batch + 1 <= MOD_ROWS

    x = jnp.concatenate([x_sample.reshape(st.ts, d), x_prompt.reshape(st.tc, d)], axis=0)
    cvecs = jnp.zeros((MOD_ROWS, d), F32).at[:dec_batch].set(c).at[dec_batch].set(c_ctx)
    modr = _mod_all(cvecs, w_mod, b_mod).reshape(depth * MOD_ROWS * 6, 1, d)
    cos2, sin2 = _rope_tables(st)
    dfts = {n: _Dft(n, _dft_radix(n)) for n in {dec_seq, seq}}
    norm1_g3 = norm1_g.reshape(depth, 1, d)
    qg3 = q_norm_g.reshape(depth, 1, HEAD_DIM)
    kg3 = k_norm_g.reshape(depth, 1, HEAD_DIM)
    out_norm_g3 = out_norm_g.reshape(depth, 1, d)
    cache_k4 = cache_k.reshape(dec_batch, depth, past, kv_w)
    cache_v4 = cache_v.reshape(dec_batch, depth, past, kv_w)

    new_k, new_v = [], []
    h = y = None
    for layer in range(depth):
        if h is None:
            h = _norm_mod(x, norm1_g3, modr, st, layer, 1, 0)
        proj = _gmm(h, (w_in,), layer, None, mode="plain", out_dtype=BF16,
                    tm=st.token_tile(2048), tn=_pick(in_w, 768, LANES), name="w_in")
        qn, kn, kc = _qk_norm_rope(proj, qg3, kg3, cos2, sin2, layer, a_w, kv_w, st)
        new_k.append(kc[st.ts:].reshape(batch, seq, n_kv, HEAD_DIM))
        new_v.append(proj[st.ts:, a_w + kv_w:a_w + 2 * kv_w].astype(F32)
                     .reshape(batch, seq, n_kv, HEAD_DIM))

        attn_s = _attention(qn, kn, proj, a_w + kv_w, n_kv, dec_batch, dec_seq, 0,
                            cache=(cache_k4, cache_v4, layer))
        attn_c = _attention(qn, kn, proj, a_w + kv_w, n_kv, batch, seq, st.ts)
        hy_s, hy_c = _hyena(proj, layer, p, st, u_col0, hy_w, dfts)
        tm = st.token_tile(512)
        x = _gmm((attn_s, attn_c, hy_s, hy_c), (w_out,), layer, None, mode="resid", out_dtype=F32,
                 tm=tm, tn=_pick(d, 1024, LANES), merge=(st.ts, out_norm_g3, layer),
                 resid=(x, modr, _mod_index(st, layer, 2, tm)), name="w_out")
        h = None
        if layer % 2 == 0:
            x = _dense_layer(x, p, modr, st, layer, layer // 2)
        elif layer + 1 < depth:
            x, h = _moe_layer(x, p, modr, st, layer, layer // 2, "norm_mod", (norm1_g3,))
        else:
            (y,) = _moe_layer(x, p, modr, st, layer, layer // 2, "final",
                              (final_norm_g.reshape(1, d),))
    if y is None:
        y = _final_norm(x, final_norm_g)
    y_sample = y[:st.ts].reshape(dec_batch, dec_seq, d)
    y_prompt = y[st.ts:].reshape(batch, seq, d)
    return (y_prompt, y_sample, jnp.stack(new_k, axis=1), jnp.stack(new_v, axis=1))
```

```python
import cmath
import functools
import math

import jax
import jax.numpy as jnp
import numpy as np
from jax import lax
from jax.experimental import pallas as pl
from jax.experimental.pallas import tpu as pltpu

F32 = jnp.float32
BF16 = jnp.bfloat16

HEAD_DIM = 128
GRID_W = 64
ROPE_THETA = 10000.0
EPS = 1e-6
POS_FREQS = 16
HY_ORDER = 2
LANES = 128
MOD_ROWS = 8
VMEM_LIMIT_BYTES = 56 * 1024 * 1024


def _cparams(n_axes):
    return pltpu.CompilerParams(dimension_semantics=("arbitrary",) * n_axes,
                                vmem_limit_bytes=VMEM_LIMIT_BYTES)


def _pick(n, pref, mult):
    best = None
    d = mult
    while d <= min(n, pref):
        if n % d == 0:
            best = d
        d += mult
    assert best is not None, (n, pref, mult)
    return best


def _roundup(x, m):
    return (x + m - 1) // m * m


def _mod_kernel(c_ref, w_ref, b_ref, o_ref):
    c = c_ref[...]
    s = (c * jax.nn.sigmoid(c)).astype(BF16)
    o_ref[...] = jnp.dot(s, w_ref[...].astype(BF16), preferred_element_type=F32) + b_ref[...]


def _mod_all(cvecs, w_mod, b_mod):
    depth, d, n = w_mod.shape
    tn = _pick(n, 1024, LANES)
    return pl.pallas_call(
        _mod_kernel,
        grid=(depth, n // tn),
        in_specs=[pl.BlockSpec((MOD_ROWS, d), lambda l, j: (0, 0)),
                  pl.BlockSpec((None, d, tn), lambda l, j: (l, 0, j)),
                  pl.BlockSpec((None, 1, tn), lambda l, j: (l, 0, j))],
        out_specs=pl.BlockSpec((None, MOD_ROWS, tn), lambda l, j: (l, 0, j)),
        out_shape=jax.ShapeDtypeStruct((depth, MOD_ROWS, n), F32),
        compiler_params=_cparams(2),
        name="mod_all",
    )(cvecs, w_mod, b_mod.reshape(depth, 1, n))


def _rms(x, g):
    return x * lax.rsqrt(jnp.mean(x * x, axis=-1, keepdims=True) + EPS) * g


def _norm_mod_kernel(x_ref, g_ref, sc_ref, sh_ref, o_ref):
    y = _rms(x_ref[...], g_ref[...])
    o_ref[...] = (y * (1.0 + sc_ref[...]) + sh_ref[...]).astype(o_ref.dtype)


def _norm_kernel(x_ref, g_ref, o_ref):
    o_ref[...] = _rms(x_ref[...], g_ref[...]).astype(o_ref.dtype)


class _Stream:
    def __init__(self, dec_batch, dec_seq, batch, seq):
        self.dec_batch, self.dec_seq, self.batch, self.seq = dec_batch, dec_seq, batch, seq
        self.ts = dec_batch * dec_seq
        self.tc = batch * seq
        self.t = self.ts + self.tc

    def group(self, row):
        return jnp.minimum(row // self.dec_seq, self.dec_batch)

    def token_tile(self, pref):
        return _pick(math.gcd(self.dec_seq, self.tc), pref, 16)


def _mod_index(st, layer, which, tm):
    def index(i):
        return ((layer * MOD_ROWS + st.group(i * tm)) * 6 + which, 0, 0)
    return index


def _norm_mod(x, g, modr, st, layer, which_sc, which_sh):
    t, d = x.shape
    tm = st.token_tile(512)
    return pl.pallas_call(
        _norm_mod_kernel,
        grid=(t // tm,),
        in_specs=[pl.BlockSpec((tm, d), lambda i: (i, 0)),
                  pl.BlockSpec((None, 1, d), lambda i: (layer, 0, 0)),
                  pl.BlockSpec((None, 1, d), _mod_index(st, layer, which_sc, tm)),
                  pl.BlockSpec((None, 1, d), _mod_index(st, layer, which_sh, tm))],
        out_specs=pl.BlockSpec((tm, d), lambda i: (i, 0)),
        out_shape=jax.ShapeDtypeStruct((t, d), BF16),
        compiler_params=_cparams(1),
        name="norm_mod",
    )(x, g, modr, modr)


def _join_norm_mod_kernel(xs_ref, xc_ref, g_ref, sc_ref, sh_ref, x_ref, o_ref, *, n_sample_tiles):
    x = jnp.where(pl.program_id(0) < n_sample_tiles, xs_ref[...], xc_ref[...])
    x_ref[...] = x
    o_ref[...] = (_rms(x, g_ref[...]) * (1.0 + sc_ref[...]) + sh_ref[...]).astype(o_ref.dtype)


def _join_norm_mod(x_sample, x_context, g, modr, st, layer, which_sc, which_sh):
    d = x_sample.shape[1]
    tm = st.token_tile(512)
    ns = st.ts // tm
    row = pl.BlockSpec((tm, d), lambda i: (i, 0))
    return pl.pallas_call(
        functools.partial(_join_norm_mod_kernel, n_sample_tiles=ns),
        grid=(st.t // tm,),
        in_specs=[pl.BlockSpec((tm, d), lambda i: (jnp.minimum(i, ns - 1), 0)),
                  pl.BlockSpec((tm, d), lambda i: (jnp.maximum(i - ns, 0), 0)),
                  pl.BlockSpec((None, 1, d), lambda i: (layer, 0, 0)),
                  pl.BlockSpec((None, 1, d), _mod_index(st, layer, which_sc, tm)),
                  pl.BlockSpec((None, 1, d), _mod_index(st, layer, which_sh, tm))],
        out_specs=[row, row],
        out_shape=[jax.ShapeDtypeStruct((st.t, d), F32), jax.ShapeDtypeStruct((st.t, d), BF16)],
        compiler_params=_cparams(1),
        name="join_norm_mod",
    )(x_sample, x_context, g, modr, modr)


def _final_norm(x, g):
    t, d = x.shape
    tm = _pick(t, 512, 8)
    return pl.pallas_call(
        _norm_kernel,
        grid=(t // tm,),
        in_specs=[pl.BlockSpec((tm, d), lambda i: (i, 0)),
                  pl.BlockSpec((1, d), lambda i: (0, 0))],
        out_specs=pl.BlockSpec((tm, d), lambda i: (i, 0)),
        out_shape=jax.ShapeDtypeStruct((t, d), F32),
        compiler_params=_cparams(1),
        name="final_norm",
    )(x, g.reshape(1, d))


def _gmm_kernel(wid_ref, src_ref, *refs, mode, merge_tiles):
    i = pl.program_id(1)
    active = src_ref[i] == i
    n_a = 1 if merge_tiles is None else 6
    a_refs, refs = refs[:n_a], refs[n_a:]
    if mode == "swiglu":
        wg_ref, wu_ref, o_ref, wgb, wub = refs
        wpairs = ((wg_ref, wgb), (wu_ref, wub))
    elif mode == "resid":
        w_ref, x_ref, gate_ref, o_ref, wb = refs
        wpairs = ((w_ref, wb),)
    else:
        w_ref, o_ref, wb = refs
        wpairs = ((w_ref, wb),)

    def load_a():
        if merge_tiles is None:
            return a_refs[0][...]
        as_ref, ac_ref, hs_ref, hc_ref, ga_ref, gh_ref = a_refs
        sample = i < merge_tiles
        attn = jnp.where(sample, as_ref[...], ac_ref[...]).astype(F32)
        hy = jnp.where(sample, hs_ref[...], hc_ref[...]).astype(F32)
        return jnp.concatenate([_rms(attn, ga_ref[...]).astype(BF16),
                                _rms(hy, gh_ref[...]).astype(BF16)], axis=1)

    prev = wid_ref[jnp.maximum(i - 1, 0)]
    new_w = jnp.logical_or(i == 0, wid_ref[i] != prev)

    @pl.when(jnp.logical_and(new_w, active))
    def _cast():
        for src, dst in wpairs:
            dst[...] = src[...].astype(BF16)

    @pl.when(jnp.logical_not(active))
    def _unused_tile():
        o_ref[...] = jnp.zeros_like(o_ref)

    @pl.when(active)
    def _compute():
        a = load_a()
        if mode == "swiglu":
            g = jnp.dot(a, wgb[...], preferred_element_type=F32)
            u = jnp.dot(a, wub[...], preferred_element_type=F32)
            o_ref[...] = (g * jax.nn.sigmoid(g) * u).astype(o_ref.dtype)
        elif mode == "resid":
            acc = jnp.dot(a, wb[...], preferred_element_type=F32)
            o_ref[...] = x_ref[...] + gate_ref[...] * acc
        else:
            o_ref[...] = jnp.dot(a, wb[...], preferred_element_type=F32).astype(o_ref.dtype)


def _gmm(a, ws, wid, src, *, mode, out_dtype, tm, tn, resid=None, merge=None, name="gmm"):
    _, k, n = ws[0].shape
    if merge is None:
        m = a.shape[0]
    else:
        n_sample_rows, gains, layer = merge
        m = a[0].shape[0] + a[1].shape[0]
    nt = m // tm
    if isinstance(wid, int):
        wid, src = jnp.full((nt,), wid, jnp.int32), jnp.arange(nt, dtype=jnp.int32)
    assert m % tm == 0 and n % tn == 0 and wid.shape == (nt,) and src.shape == (nt,)

    def a_index(j, i, wid_ref, src_ref):
        return (src_ref[i], 0)

    def w_index(j, i, wid_ref, src_ref):
        return (wid_ref[i], 0, j)

    def o_index(j, i, wid_ref, src_ref):
        return (i, j)

    if merge is None:
        in_specs = [pl.BlockSpec((tm, a.shape[1]), a_index)]
        args = [a]
        merge_tiles = None
    else:
        assert n_sample_rows % tm == 0
        merge_tiles = n_sample_rows // tm
        half = k // 2
        s_idx = lambda j, i, *_: (jnp.minimum(i, merge_tiles - 1), 0)
        c_idx = lambda j, i, *_: (jnp.maximum(i - merge_tiles, 0), 0)
        in_specs = [pl.BlockSpec((tm, half), s_idx), pl.BlockSpec((tm, half), c_idx),
                    pl.BlockSpec((tm, half), s_idx), pl.BlockSpec((tm, half), c_idx),
                    pl.BlockSpec((None, 1, half), lambda j, i, *_: (layer, 0, 0)),
                    pl.BlockSpec((None, 1, half), lambda j, i, *_: (layer, 0, 1))]
        args = [a[0], a[1], a[2], a[3], gains, gains]
    in_specs += [pl.BlockSpec((None, k, tn), w_index) for _ in ws]
    args += list(ws)
    if mode == "resid":
        x, modr, mod_index = resid
        in_specs += [pl.BlockSpec((tm, tn), o_index),
                     pl.BlockSpec((None, 1, tn),
                                  lambda j, i, wid_ref, nv_ref: mod_index(i)[:2] + (j,))]
        args += [x, modr]
    grid_spec = pltpu.PrefetchScalarGridSpec(
        num_scalar_prefetch=2,
        grid=(n // tn, nt),
        in_specs=in_specs,
        out_specs=pl.BlockSpec((tm, tn), o_index),
        scratch_shapes=[pltpu.VMEM((k, tn), BF16) for _ in ws],
    )
    return pl.pallas_call(
        functools.partial(_gmm_kernel, mode=mode, merge_tiles=merge_tiles),
        grid_spec=grid_spec,
        out_shape=jax.ShapeDtypeStruct((m, n), out_dtype),
        compiler_params=_cparams(2),
        name=name,
    )(wid, src, *args)


def _qk_kernel(q_ref, k_ref, qg_ref, kg_ref, cos_ref, sin_ref, qo_ref, ko_ref, kc_ref, *, scale):
    cos = cos_ref[...]
    sin = sin_ref[...]

    def head(x, g):
        xn = _rms(x.astype(F32), g)
        return xn, xn * cos + pltpu.roll(xn, HEAD_DIM // 2, 1) * sin

    for h in range(q_ref.shape[1] // HEAD_DIM):
        sl = slice(h * HEAD_DIM, (h + 1) * HEAD_DIM)
        _, roped = head(q_ref[:, sl], qg_ref[...])
        qo_ref[:, sl] = (roped * scale).astype(qo_ref.dtype)
    for h in range(k_ref.shape[1] // HEAD_DIM):
        sl = slice(h * HEAD_DIM, (h + 1) * HEAD_DIM)
        xn, roped = head(k_ref[:, sl], kg_ref[...])
        ko_ref[:, sl] = roped.astype(ko_ref.dtype)
        kc_ref[:, sl] = xn


def _qk_norm_rope(proj, qg, kg, cos2, sin2, layer, a_w, kv_w, st):
    t = proj.shape[0]
    tm = st.token_tile(512)
    assert a_w % kv_w == 0
    return pl.pallas_call(
        functools.partial(_qk_kernel, scale=1.0 / math.sqrt(HEAD_DIM)),
        grid=(t // tm,),
        in_specs=[pl.BlockSpec((tm, a_w), lambda i: (i, 0)),
                  pl.BlockSpec((tm, kv_w), lambda i: (i, a_w // kv_w)),
                  pl.BlockSpec((None, 1, HEAD_DIM), lambda i: (layer, 0, 0)),
                  pl.BlockSpec((None, 1, HEAD_DIM), lambda i: (layer, 0, 0)),
                  pl.BlockSpec((tm, HEAD_DIM), lambda i: (i, 0)),
                  pl.BlockSpec((tm, HEAD_DIM), lambda i: (i, 0))],
        out_specs=[pl.BlockSpec((tm, a_w), lambda i: (i, 0)),
                   pl.BlockSpec((tm, kv_w), lambda i: (i, 0)),
                   pl.BlockSpec((tm, kv_w), lambda i: (i, 0))],
        out_shape=[jax.ShapeDtypeStruct((t, a_w), BF16),
                   jax.ShapeDtypeStruct((t, kv_w), BF16),
                   jax.ShapeDtypeStruct((t, kv_w), F32)],
        compiler_params=_cparams(1),
        name="qk_norm_rope",
    )(proj, proj, qg, kg, cos2, sin2)


def _rope_tables(st):
    pos = np.arange(st.dec_seq)
    inv = ROPE_THETA ** (-np.arange(0, HEAD_DIM // 2, 2, dtype=np.float64) / (HEAD_DIM // 2))
    ang = np.concatenate([(pos // GRID_W)[:, None] * inv, (pos % GRID_W)[:, None] * inv], axis=-1)
    cos2 = np.concatenate([np.cos(ang), np.cos(ang)], axis=-1)
    sin2 = np.concatenate([-np.sin(ang), np.sin(ang)], axis=-1)
    cos_all = np.concatenate([np.tile(cos2, (st.dec_batch, 1)), np.ones((st.tc, HEAD_DIM))])
    sin_all = np.concatenate([np.tile(sin2, (st.dec_batch, 1)), np.zeros((st.tc, HEAD_DIM))])
    return jnp.asarray(cos_all, F32), jnp.asarray(sin_all, F32)


KEY_CHUNK = 512


def _attn_kernel(*refs, g, has_cache):
    if has_cache:
        q_ref, ck_ref, cv_ref, k_ref, v_ref, o_ref = refs
    else:
        q_ref, k_ref, v_ref, o_ref = refs
    tq = q_ref.shape[0]
    q = jnp.concatenate([q_ref[:, h * HEAD_DIM:(h + 1) * HEAD_DIM] for h in range(g)], axis=0)
    lk = k_ref.shape[0]
    parts = []
    if has_cache:
        parts.append((ck_ref, cv_ref, 0, ck_ref.shape[0]))
    c0 = 0
    while c0 < lk:
        n = min(KEY_CHUNK, lk - c0)
        parts.append((k_ref, v_ref, c0, n))
        c0 += n
    m = acc = None
    for kr, vr, c0, n in parts:
        kc = kr[c0:c0 + n, :].astype(BF16)
        va = jnp.concatenate([vr[c0:c0 + n, :].astype(BF16), jnp.ones((n, HEAD_DIM), BF16)], axis=1)
        s = lax.dot_general(q, kc, (((1,), (1,)), ((), ())), preferred_element_type=F32)
        mc = jnp.max(s, axis=-1, keepdims=True)
        if m is None:
            m = mc
            acc = jnp.dot(jnp.exp(s - m).astype(BF16), va, preferred_element_type=F32)
        else:
            m_new = jnp.maximum(m, mc)
            acc = (jnp.exp(m - m_new) * acc
                   + jnp.dot(jnp.exp(s - m_new).astype(BF16), va, preferred_element_type=F32))
            m = m_new
    o = acc[:, :HEAD_DIM] / acc[:, HEAD_DIM:]
    for h in range(g):
        o_ref[:, h * HEAD_DIM:(h + 1) * HEAD_DIM] = o[h * tq:(h + 1) * tq].astype(o_ref.dtype)


def _attention(qn, kn, proj, v_col0, n_kv, nseq, seq_len, row0, cache=None):
    aw = qn.shape[1]
    g = aw // HEAD_DIM // n_kv
    tq = _pick(seq_len, 512, 16)
    nq = seq_len // tq
    assert row0 % seq_len == 0 and v_col0 % HEAD_DIM == 0
    in_specs = [pl.BlockSpec((tq, g * HEAD_DIM), lambda b, kv, i: (row0 // tq + b * nq + i, kv))]
    args = [qn]
    if cache is not None:
        ck, cv, layer = cache
        past = ck.shape[2]
        cspec = pl.BlockSpec((None, None, past, HEAD_DIM), lambda b, kv, i: (b, layer, 0, kv))
        in_specs += [cspec, cspec]
        args += [ck, cv]
    in_specs += [pl.BlockSpec((seq_len, HEAD_DIM), lambda b, kv, i: (row0 // seq_len + b, kv)),
                 pl.BlockSpec((seq_len, HEAD_DIM),
                              lambda b, kv, i: (row0 // seq_len + b, v_col0 // HEAD_DIM + kv))]
    args += [kn, proj]
    return pl.pallas_call(
        functools.partial(_attn_kernel, g=g, has_cache=cache is not None),
        grid=(nseq, n_kv, nq),
        in_specs=in_specs,
        out_specs=pl.BlockSpec((tq, g * HEAD_DIM), lambda b, kv, i: (b * nq + i, kv)),
        out_shape=jax.ShapeDtypeStruct((nseq * seq_len, aw), BF16),
        compiler_params=_cparams(3),
        name="attention",
    )(*args)


def _filter_kernel(feat_ref, w1_ref, b1_ref, w2_ref, b2_ref, w3_ref, ld_ref, o_ref, h_ref, *,
                   seq_len, hy_w):
    hp = lax.Precision.HIGHEST

    @pl.when(pl.program_id(1) == 0)
    def _hidden():
        h = jnp.sin(jnp.dot(feat_ref[...], w1_ref[...], precision=hp, preferred_element_type=F32)
                    + b1_ref[...])
        h_ref[...] = jnp.sin(jnp.dot(h, w2_ref[...], precision=hp, preferred_element_type=F32)
                             + b2_ref[...])

    h = jnp.dot(h_ref[...].astype(BF16), w3_ref[...].astype(BF16), preferred_element_type=F32)
    tl, tn = h.shape
    row = lax.broadcasted_iota(jnp.int32, (tl, 1), 0) + pl.program_id(0) * tl
    t = row.astype(F32) / seq_len
    h = h * jnp.exp(-jnp.exp(ld_ref[...]) * t)
    col = lax.broadcasted_iota(jnp.int32, (1, tn), 1) + pl.program_id(1) * tn
    backward = (col // hy_w) % 2 == 1
    o_ref[...] = jnp.where(row == 0, jnp.where(backward, 0.0, h), h).astype(o_ref.dtype)


def _filter_features(seq_len):
    t = np.arange(seq_len, dtype=np.float32) / np.float32(seq_len)
    fr = np.arange(1, POS_FREQS + 1, dtype=np.float32)
    ang = float(np.float32(2.0 * math.pi)) * t[:, None].astype(np.float64) * fr
    feat = np.concatenate([t[:, None], np.sin(ang), np.cos(ang)], axis=-1)
    out = np.zeros((seq_len, LANES), np.float32)
    out[:, :feat.shape[1]] = feat
    return jnp.asarray(out)


def _pad2(x, rows, cols):
    return jnp.pad(x, ((0, rows - x.shape[0]), (0, cols - x.shape[1])))


def _hyena_filters(seq_len, layer, p, hy_w):
    n = p["filt_w3"].shape[-1]
    feat = _filter_features(seq_len)
    w1 = _pad2(p["filt_w1"][layer], LANES, LANES)
    b1 = _pad2(p["filt_b1"][layer][None, :], 1, LANES)
    w2 = _pad2(p["filt_w2"][layer], LANES, LANES)
    b2 = _pad2(p["filt_b2"][layer][None, :], 1, LANES)
    w3 = _pad2(p["filt_w3"][layer], LANES, n)
    ld = p["filt_log_decay"][layer][None, :]
    tl = _pick(seq_len, 1024, 16)
    tn = _pick(n, 2048, LANES)
    return pl.pallas_call(
        functools.partial(_filter_kernel, seq_len=seq_len, hy_w=hy_w),
        grid=(seq_len // tl, n // tn),
        in_specs=[pl.BlockSpec((tl, LANES), lambda i, j: (i, 0)),
                  pl.BlockSpec((LANES, LANES), lambda i, j: (0, 0)),
                  pl.BlockSpec((1, LANES), lambda i, j: (0, 0)),
                  pl.BlockSpec((LANES, LANES), lambda i, j: (0, 0)),
                  pl.BlockSpec((1, LANES), lambda i, j: (0, 0)),
                  pl.BlockSpec((LANES, tn), lambda i, j: (0, j)),
                  pl.BlockSpec((1, tn), lambda i, j: (0, j))],
        out_specs=pl.BlockSpec((tl, tn), lambda i, j: (i, j)),
        out_shape=jax.ShapeDtypeStruct((seq_len, n), BF16),
        scratch_shapes=[pltpu.VMEM((tl, LANES), F32)],
        compiler_params=_cparams(2),
        name="hy_filter",
    )(feat, w1, b1, w2, b2, w3, ld)


class _Dft:
    def __init__(self, seq_len, radix):
        self.seq_len, self.r = seq_len, radix
        n = 2 * seq_len
        self.n1 = n // radix
        self.la = seq_len // radix
        k1 = self.n1 // 2 + 1
        self.k1p = _roundup(k1, 16)
        a = np.arange(self.la)[None, :]
        kk = np.arange(self.k1p)[:, None]
        valid = kk < k1
        th = 2.0 * math.pi * kk * a / self.n1
        fwd = np.concatenate([np.where(valid, np.cos(th), 0.0), np.where(valid, -np.sin(th), 0.0)])
        wgt = np.where((kk == 0) | (kk == self.n1 // 2), 1.0, 2.0) * valid / n
        inv = np.concatenate([wgt * np.cos(th), -wgt * np.sin(th)]).T
        self.fwd = jnp.asarray(fwd, F32)
        self.inv = jnp.asarray(inv, F32)
        ph = 2.0 * math.pi * kk / n
        ones = np.ones((1, LANES))
        self.tw_re = jnp.asarray(np.cos(ph) * ones, F32)
        self.tw_im = jnp.asarray(-np.sin(ph) * ones, F32)


def _cmul_const(x, w):
    wr, wi = w.real, w.imag
    if abs(wi) < 1e-12:
        if abs(wr - 1.0) < 1e-12:
            return x
        if abs(wr + 1.0) < 1e-12:
            return (-x[0], -x[1])
    if abs(wr) < 1e-12:
        if abs(wi - 1.0) < 1e-12:
            return (-x[1], x[0])
        if abs(wi + 1.0) < 1e-12:
            return (x[1], -x[0])
    return (x[0] * wr - x[1] * wi, x[0] * wi + x[1] * wr)


def _fft_list(xs, sign):
    n = len(xs)
    if n == 1:
        return xs
    ev = _fft_list(xs[0::2], sign)
    od = _fft_list(xs[1::2], sign)
    out = [None] * n
    for k in range(n // 2):
        t = _cmul_const(od[k], cmath.exp(sign * 2j * math.pi * k / n))
        out[k] = (ev[k][0] + t[0], ev[k][1] + t[1])
        out[k + n // 2] = (ev[k][0] - t[0], ev[k][1] - t[1])
    return out


ROW_CHUNK = 16
LONGCONV_ROWS = 1024


def _lane_tile(x, width):
    return x if width == LANES else jnp.concatenate([x] * (width // LANES), axis=1)


def _twiddles(tw_re_ref, tw_im_ref, rows, r, tc):
    if r == 1:
        return []
    w1 = (tw_re_ref[rows, :], tw_im_ref[rows, :])
    ws = [w1]
    for _ in range(r - 2):
        pr, pi = ws[-1]
        ws.append((pr * w1[0] - pi * w1[1], pr * w1[1] + pi * w1[0]))
    return [(_lane_tile(wr, tc), _lane_tile(wi, tc)) for wr, wi in ws]


def _spectrum_chunk(y_ref, tws, c0, k1p, r):
    rows = pl.ds(c0, ROW_CHUNK)
    rows_im = pl.ds(k1p + c0, ROW_CHUNK)
    ts = []
    for b in range(r):
        yr = y_ref[b, rows, :]
        yi = y_ref[b, rows_im, :]
        if b == 0:
            ts.append((yr, yi))
        else:
            wr, wi = tws[b - 1]
            ts.append((yr * wr - yi * wi, yr * wi + yi * wr))
    return _fft_list(ts, -1)


def _lanes(q):
    return slice(q * LANES, (q + 1) * LANES)


def _conv3_group(u_ref, w_ref, b_ref, q, seq):
    x = u_ref[seq, _lanes(q)].astype(F32)
    n = x.shape[0]
    row = lax.broadcasted_iota(jnp.int32, x.shape, 0)
    prev = jnp.where(row == 0, 0.0, pltpu.roll(x, 1, 0))
    nxt = jnp.where(row == n - 1, 0.0, pltpu.roll(x, n - 1, 0))
    w = w_ref[:, _lanes(q)]
    return prev * w[0:1] + x * w[1:2] + nxt * w[2:3] + b_ref[:, _lanes(q)]


def _stage1(f_ref, load_group, xf_ref, y_ref, r, la):
    nq = xf_ref.shape[0]
    for q in range(nq):
        xf_ref[q] = load_group(q)
    f = f_ref[...].astype(BF16)
    for b in range(r):
        xb = jnp.concatenate([xf_ref[q, pl.ds(b, la, stride=r), :] for q in range(nq)], axis=1)
        y_ref[b] = jnp.dot(f, xb.astype(BF16), preferred_element_type=F32)


def _spectrum_kernel(*refs, r, k1p, la):
    if r > 1:
        f_ref, tw_re_ref, tw_im_ref, hf_ref, hb_ref, kr_ref, ki_ref, xf_ref, y_ref, acc_ref = refs
    else:
        f_ref, hf_ref, hb_ref, kr_ref, ki_ref, xf_ref, y_ref, acc_ref = refs
        tw_re_ref = tw_im_ref = None
    tc = y_ref.shape[-1]

    def sweep(src_ref, backward):
        _stage1(f_ref, lambda q: src_ref[:, _lanes(q)].astype(F32), xf_ref, y_ref, r, la)

        def chunk(c, carry):
            c0 = pl.multiple_of(c * ROW_CHUNK, ROW_CHUNK)
            rows = pl.ds(c0, ROW_CHUNK)
            rows_im = pl.ds(k1p + c0, ROW_CHUNK)
            x = _spectrum_chunk(y_ref, _twiddles(tw_re_ref, tw_im_ref, rows, r, tc), c0, k1p, r)
            for k2 in range(r):
                if backward:
                    kr_ref[k2, rows, :] = (acc_ref[k2, rows, :] + x[k2][0]).astype(kr_ref.dtype)
                    ki_ref[k2, rows, :] = (acc_ref[k2, rows_im, :] - x[k2][1]).astype(ki_ref.dtype)
                else:
                    acc_ref[k2, rows, :] = x[k2][0]
                    acc_ref[k2, rows_im, :] = x[k2][1]
            return carry

        lax.fori_loop(0, k1p // ROW_CHUNK, chunk, 0)

    sweep(hf_ref, False)
    sweep(hb_ref, True)


def _filter_spectrum(taps, dft, hy_w):
    r, la, k1p, seq_len = dft.r, dft.la, dft.k1p, dft.seq_len
    tc = _pick(hy_w, 256, LANES)
    nct = hy_w // tc

    def tap_spec(direction):
        return pl.BlockSpec((seq_len, tc), lambda o, j: (0, (o * 2 + direction) * nct + j))

    const = [pl.BlockSpec(dft.fwd.shape, lambda o, j: (0, 0))]
    cargs = [dft.fwd]
    if r > 1:
        const += [pl.BlockSpec(dft.tw_re.shape, lambda o, j: (0, 0))] * 2
        cargs += [dft.tw_re, dft.tw_im]
    out_spec = pl.BlockSpec((None, r, k1p, tc), lambda o, j: (o, 0, 0, j))
    shape = jax.ShapeDtypeStruct((HY_ORDER, r, k1p, hy_w), BF16)
    return pl.pallas_call(
        functools.partial(_spectrum_kernel, r=r, k1p=k1p, la=la),
        grid=(HY_ORDER, nct),
        in_specs=const + [tap_spec(0), tap_spec(1)],
        out_specs=[out_spec, out_spec],
        out_shape=[shape, shape],
        scratch_shapes=[pltpu.VMEM((tc // LANES, seq_len, LANES), F32),
                        pltpu.VMEM((r, 2 * k1p, tc), F32),
                        pltpu.VMEM((r, 2 * k1p, tc), F32)],
        compiler_params=_cparams(2),
        name="hy_filter_spectrum",
    )(*cargs, taps, taps)


def _longconv_kernel(*refs, n_sub, **kw):
    seq_len = refs[-3].shape[1]
    for sub in range(n_sub):
        _longconv_one(refs, slice(sub * seq_len, (sub + 1) * seq_len), **kw)


def _longconv_one(refs, seq, *, r, k1p, la, z_conv):
    refs = list(refs)
    f_ref, fi_ref = refs[:2]
    del refs[:2]
    tw_re_ref = tw_im_ref = None
    if r > 1:
        tw_re_ref, tw_im_ref = refs[:2]
        del refs[:2]
    z_ref = refs.pop(0)
    if z_conv:
        zw_ref, zb_ref = refs[:2]
        del refs[:2]
    gate_ref, gw_ref, gb_ref, skip_ref, kr_ref, ki_ref, o_ref, xf_ref, y_ref, g_ref = refs
    tc = y_ref.shape[-1]
    if z_conv:
        load_z = lambda q: _conv3_group(z_ref, zw_ref, zb_ref, q, seq)
    else:
        load_z = lambda q: z_ref[seq, _lanes(q)].astype(F32)
    _stage1(f_ref, load_z, xf_ref, y_ref, r, la)

    def chunk(c, carry):
        c0 = pl.multiple_of(c * ROW_CHUNK, ROW_CHUNK)
        rows = pl.ds(c0, ROW_CHUNK)
        rows_im = pl.ds(k1p + c0, ROW_CHUNK)
        tws = _twiddles(tw_re_ref, tw_im_ref, rows, r, tc)
        x = _spectrum_chunk(y_ref, tws, c0, k1p, r)
        prod = []
        for k2 in range(r):
            kr = kr_ref[k2, rows, :].astype(F32)
            ki = ki_ref[k2, rows, :].astype(F32)
            prod.append((x[k2][0] * kr - x[k2][1] * ki, x[k2][0] * ki + x[k2][1] * kr))
        s = _fft_list(prod, +1)
        for b in range(r):
            if b == 0:
                gr, gi = s[0]
            else:
                wr, wi = tws[b - 1]
                gr = s[b][0] * wr + s[b][1] * wi
                gi = s[b][1] * wr - s[b][0] * wi
            g_ref[b, rows, :] = gr.astype(g_ref.dtype)
            g_ref[b, rows_im, :] = gi.astype(g_ref.dtype)
        return carry

    lax.fori_loop(0, k1p // ROW_CHUNK, chunk, 0)
    fi = fi_ref[...].astype(BF16)
    for b in range(r):
        yb = jnp.dot(fi, g_ref[b], preferred_element_type=F32)
        rows = pl.ds(b, la, stride=r)
        for q in range(xf_ref.shape[0]):
            xf_ref[q, rows, :] = yb[:, _lanes(q)] + skip_ref[:, _lanes(q)] * xf_ref[q, rows, :]
    for q in range(xf_ref.shape[0]):
        gate = _conv3_group(gate_ref, gw_ref, gb_ref, q, seq)
        o_ref[seq, _lanes(q)] = (gate * xf_ref[q]).astype(o_ref.dtype)


def _longconv(proj, u_col0, row0, layer, conv_w, conv_b, z_prev, order, skip, kf_re, kf_im, dft,
              nseq, hy_w):
    r, la, k1p, seq_len = dft.r, dft.la, dft.k1p, dft.seq_len
    tc = _pick(math.gcd(hy_w, u_col0), 256 if r > 1 else 1024, LANES)
    n_sub = _pick(math.gcd(nseq, row0 // seq_len) or nseq, max(1, LONGCONV_ROWS // seq_len), 1)
    blk = n_sub * seq_len
    assert row0 % blk == 0
    nct = hy_w // tc
    seq0 = row0 // blk
    z_conv = z_prev is None

    def u_specs(part):
        return [pl.BlockSpec((blk, tc), lambda s, j: (seq0 + s, u_col0 // tc + part * nct + j)),
                pl.BlockSpec((None, 3, tc), lambda s, j: (layer, 0, part * nct + j)),
                pl.BlockSpec((None, 1, tc), lambda s, j: (layer, 0, part * nct + j))]

    cdim = lambda s, j: (0, 0)
    in_specs = [pl.BlockSpec(dft.fwd.shape, cdim), pl.BlockSpec(dft.inv.shape, cdim)]
    args = [dft.fwd, dft.inv]
    if r > 1:
        in_specs += [pl.BlockSpec(dft.tw_re.shape, cdim)] * 2
        args += [dft.tw_re, dft.tw_im]
    if z_conv:
        in_specs += u_specs(0)
        args += [proj, conv_w, conv_b]
    else:
        in_specs.append(pl.BlockSpec((blk, tc), lambda s, j: (s, j)))
        args.append(z_prev)
    in_specs += u_specs(1 + order)
    args += [proj, conv_w, conv_b]
    in_specs += [pl.BlockSpec((None, 1, tc), lambda s, j: (layer * HY_ORDER + order, 0, j)),
                 pl.BlockSpec((None, r, k1p, tc), lambda s, j: (order, 0, 0, j)),
                 pl.BlockSpec((None, r, k1p, tc), lambda s, j: (order, 0, 0, j))]
    args += [skip, kf_re, kf_im]
    return pl.pallas_call(
        functools.partial(_longconv_kernel, n_sub=n_sub, r=r, k1p=k1p, la=la, z_conv=z_conv),
        grid=(nseq // n_sub, nct),
        in_specs=in_specs,
        out_specs=pl.BlockSpec((blk, tc), lambda s, j: (s, j)),
        out_shape=jax.ShapeDtypeStruct((nseq * seq_len, hy_w), BF16),
        scratch_shapes=[pltpu.VMEM((tc // LANES, seq_len, LANES), F32),
                        pltpu.VMEM((r, 2 * k1p, tc), F32),
                        pltpu.VMEM((r, 2 * k1p, tc), BF16)],
        compiler_params=_cparams(2),
        name="hy_longconv",
    )(*args)


def _hyena(proj, layer, p, st, u_col0, hy_w, dfts):
    skip = p["hy_skip"].reshape(-1, 1, hy_w)
    conv_w = p["hy_conv_w"]
    conv_b = p["hy_conv_b"].reshape(-1, 1, 3 * hy_w)
    outs = []
    for nseq, seq_len, row0 in ((st.dec_batch, st.dec_seq, 0), (st.batch, st.seq, st.ts)):
        dft = dfts[seq_len]
        taps = _hyena_filters(seq_len, layer, p, hy_w)
        kf_re, kf_im = _filter_spectrum(taps, dft, hy_w)
        z = None
        for order in range(HY_ORDER):
            z = _longconv(proj, u_col0, row0, layer, conv_w, conv_b, z, order, skip, kf_re, kf_im,
                          dft, nseq, hy_w)
        outs.append(z)
    return outs


def _route_kernel(x_ref, g_ref, sc_ref, sh_ref, rw_ref, hp_ref, w1_ref, w2_ref, code_ref,
                  rank_ref, cnt_ref, carry_ref, *, n_experts):
    i = pl.program_id(0)

    @pl.when(i == 0)
    def _init():
        carry_ref[...] = jnp.zeros_like(carry_ref)

    h = _rms(x_ref[...], g_ref[...]) * (1.0 + sc_ref[...]) + sh_ref[...]
    hp_ref[...] = h

    rw = rw_ref[...]
    h_hi, rw_hi = h.astype(BF16), rw.astype(BF16)
    h_lo = (h - h_hi.astype(F32)).astype(BF16)
    rw_lo = (rw - rw_hi.astype(F32)).astype(BF16)
    logits = (jnp.dot(h_hi, rw_hi, preferred_element_type=F32)
              + (jnp.dot(h_hi, rw_lo, preferred_element_type=F32)
                 + jnp.dot(h_lo, rw_hi, preferred_element_type=F32)))
    lane = lax.broadcasted_iota(jnp.int32, logits.shape, 1)
    neg = jnp.float32(-jnp.inf)
    logits = jnp.where(lane < n_experts, logits, neg)
    m1 = jnp.max(logits, axis=-1, keepdims=True)
    i1 = jnp.min(jnp.where(logits == m1, lane, LANES), axis=-1, keepdims=True)
    rest = jnp.where(lane == i1, neg, logits)
    m2 = jnp.max(rest, axis=-1, keepdims=True)
    i2 = jnp.min(jnp.where(rest == m2, lane, LANES), axis=-1, keepdims=True)
    e2 = jnp.exp(m2 - m1)
    w1 = 1.0 / (1.0 + e2)
    w2 = e2 / (1.0 + e2)
    w1_ref[...] = jnp.broadcast_to(w1, w1_ref.shape)
    w2_ref[...] = jnp.broadcast_to(w2, w2_ref.shape)

    code = jnp.where(lane == i1, 1.0, jnp.where(lane == i2, 2.0, 0.0))
    code_t = code.T[:code_ref.shape[0], :]
    code_ref[...] = code_t.astype(jnp.int32)
    sel = (code_t > 0.0).astype(BF16)
    tm = sel.shape[1]
    upper = (lax.broadcasted_iota(jnp.int32, (tm, tm), 0)
             < lax.broadcasted_iota(jnp.int32, (tm, tm), 1)).astype(BF16)
    excl = jnp.dot(sel, upper, preferred_element_type=F32)
    carry = carry_ref[...]
    rank_ref[...] = (excl + carry[:, :1]).astype(jnp.int32)
    carry = carry + jnp.sum(sel.astype(F32), axis=-1, keepdims=True)
    carry_ref[...] = carry
    cnt_ref[...] = carry.astype(jnp.int32)


def _route(x, g, modr, rw, st, layer, n_experts):
    t, d = x.shape
    tm = st.token_tile(512)
    assert tm % LANES == 0 and n_experts <= 8
    rwp = _pad2(rw, d, LANES)
    row_spec = pl.BlockSpec((tm, LANES), lambda i: (i, 0))
    col_spec = pl.BlockSpec((8, tm), lambda i: (0, i))
    return pl.pallas_call(
        functools.partial(_route_kernel, n_experts=n_experts),
        grid=(t // tm,),
        in_specs=[pl.BlockSpec((tm, d), lambda i: (i, 0)),
                  pl.BlockSpec((None, 1, d), lambda i: (layer, 0, 0)),
                  pl.BlockSpec((None, 1, d), _mod_index(st, layer, 4, tm)),
                  pl.BlockSpec((None, 1, d), _mod_index(st, layer, 3, tm)),
                  pl.BlockSpec((d, LANES), lambda i: (0, 0))],
        out_specs=[pl.BlockSpec((tm, d), lambda i: (i, 0)), row_spec, row_spec,
                   col_spec, col_spec, pl.BlockSpec((8, LANES), lambda i: (0, 0))],
        out_shape=[jax.ShapeDtypeStruct((t, d), F32),
                   jax.ShapeDtypeStruct((t, LANES), F32),
                   jax.ShapeDtypeStruct((t, LANES), F32),
                   jax.ShapeDtypeStruct((8, t), jnp.int32),
                   jax.ShapeDtypeStruct((8, t), jnp.int32),
                   jax.ShapeDtypeStruct((8, LANES), jnp.int32)],
        scratch_shapes=[pltpu.VMEM((8, LANES), F32)],
        compiler_params=_cparams(1),
        name="moe_route",
    )(x, g, modr, modr, rwp)


DISPATCH_ROWS = 512


GATHER_UNROLL = 8


def _issue_row_gather(idx_ref, src_ref, dst_ref, sem, priorities):
    def trip(g, carry):
        for u in range(GATHER_UNROLL):
            rr = g * GATHER_UNROLL + u
            pltpu.make_async_copy(src_ref.at[pl.ds(idx_ref[0, rr], 1)], dst_ref.at[pl.ds(rr, 1)],
                                  sem).start(priority=priorities[u % len(priorities)])
        return carry

    lax.fori_loop(0, dst_ref.shape[0] // GATHER_UNROLL, trip, 0)


def _wait_row_gather(dst_ref, sem):
    pltpu.make_async_copy(dst_ref, dst_ref, sem).wait()


def _pipelined_gather(idx_refs, next_idx_refs, src_ref, bufs, sems, n_steps, priorities):
    i = pl.program_id(0)
    cur = i % 2

    @pl.when(jnp.logical_and(i == 0, n_steps > 0))
    def _first():
        for idx_ref, buf in zip(idx_refs, bufs):
            _issue_row_gather(idx_ref, src_ref, buf.at[0], sems.at[0], priorities)

    @pl.when(i + 1 < n_steps)
    def _prefetch():
        for idx_ref, buf in zip(next_idx_refs, bufs):
            _issue_row_gather(idx_ref, src_ref, buf.at[1 - cur], sems.at[1 - cur], priorities)

    @pl.when(i < n_steps)
    def _wait():
        for buf in bufs:
            _wait_row_gather(buf.at[cur], sems.at[cur])

    return cur


def _dispatch_kernel(nv_ref, tok_ref, tok_next_ref, h_ref, xs_ref, rows_ref, sems):
    n_steps = nv_ref[0]
    cur = _pipelined_gather([tok_ref], [tok_next_ref], h_ref, [rows_ref], sems, n_steps, (1,))

    @pl.when(pl.program_id(0) < n_steps)
    def _used():
        xs_ref[...] = rows_ref[cur].astype(xs_ref.dtype)

    @pl.when(pl.program_id(0) >= n_steps)
    def _unused():
        xs_ref[...] = jnp.zeros_like(xs_ref)


def _index_specs(nt, td):
    return [pl.BlockSpec((None, 1, td), lambda i, *_: (i, 0, 0), memory_space=pltpu.SMEM),
            pl.BlockSpec((None, 1, td), lambda i, *_: (jnp.minimum(i + 1, nt - 1), 0, 0),
                         memory_space=pltpu.SMEM)]


def _dispatch(h, slot_token, n_used_slots):
    n_slots = slot_token.shape[0]
    d = h.shape[1]
    td = _pick(n_slots, DISPATCH_ROWS, LANES)
    nt = n_slots // td
    tok = slot_token.reshape(nt, 1, td)
    grid_spec = pltpu.PrefetchScalarGridSpec(
        num_scalar_prefetch=1,
        grid=(nt,),
        in_specs=_index_specs(nt, td) + [pl.BlockSpec(memory_space=pl.ANY)],
        out_specs=pl.BlockSpec((td, d), lambda i, nv: (i, 0)),
        scratch_shapes=[pltpu.VMEM((2, td, d), F32), pltpu.SemaphoreType.DMA((2,))],
    )
    return pl.pallas_call(
        _dispatch_kernel,
        grid_spec=grid_spec,
        out_shape=jax.ShapeDtypeStruct((n_slots, d), BF16),
        compiler_params=_cparams(1),
        name="moe_dispatch",
    )((n_used_slots + td - 1) // td, tok, tok, h)


def _combine_kernel(*refs, post, n_sample_tiles):
    (s0_ref, s0n_ref, s1_ref, s1n_ref, ys_ref, x_ref, gate_ref, w1_ref, w2_ref) = refs[:9]
    rest = refs[9:]
    n_post = {"none": 0, "norm_mod": 3, "final": 1}[post]
    post_refs, rest = rest[:n_post], rest[n_post:]
    n_out = 1 if post == "none" else 2
    outs, (ya, yb, sems) = rest[:n_out], rest[n_out:]
    cur = _pipelined_gather([s0_ref, s1_ref], [s0n_ref, s1n_ref], ys_ref, [ya, yb], sems,
                            pl.num_programs(0), (0, 1))
    w1 = _lane_tile(w1_ref[...], ya.shape[-1])
    w2 = _lane_tile(w2_ref[...], ya.shape[-1])
    xn = x_ref[...] + gate_ref[...] * (w1 * ya[cur] + w2 * yb[cur])
    if post == "final":
        y = _rms(xn, post_refs[0][...])

        @pl.when(pl.program_id(0) < n_sample_tiles)
        def _sample():
            outs[0][...] = y

        @pl.when(pl.program_id(0) >= n_sample_tiles)
        def _context():
            outs[1][...] = y
    else:
        outs[0][...] = xn
    if post == "norm_mod":
        g_ref, sc_ref, sh_ref = post_refs
        outs[1][...] = (_rms(xn, g_ref[...]) * (1.0 + sc_ref[...]) + sh_ref[...]).astype(outs[1].dtype)


def _combine(ys, slot0, slot1, x, modr, w1, w2, st, layer, post, post_args=()):
    t, d = x.shape
    td = _pick(math.gcd(st.dec_seq, st.tc), DISPATCH_ROWS, LANES)
    nt = t // td
    row = pl.BlockSpec((td, d), lambda i: (i, 0))
    in_specs = (_index_specs(nt, td) + _index_specs(nt, td)
                + [pl.BlockSpec(memory_space=pl.ANY), row,
                   pl.BlockSpec((None, 1, d), _mod_index(st, layer, 5, td)),
                   pl.BlockSpec((td, LANES), lambda i: (i, 0)),
                   pl.BlockSpec((td, LANES), lambda i: (i, 0))])
    ns = st.ts // td
    s0 = slot0.reshape(nt, 1, td)
    s1 = slot1.reshape(nt, 1, td)
    args = [s0, s0, s1, s1, ys, x, modr, w1, w2]
    out_specs, out_shape = [row], [jax.ShapeDtypeStruct((t, d), F32)]
    if post == "norm_mod":
        in_specs += [pl.BlockSpec((None, 1, d), lambda i: (layer + 1, 0, 0)),
                     pl.BlockSpec((None, 1, d), _mod_index(st, layer + 1, 1, td)),
                     pl.BlockSpec((None, 1, d), _mod_index(st, layer + 1, 0, td))]
        args += [post_args[0], modr, modr]
        out_specs.append(row)
        out_shape.append(jax.ShapeDtypeStruct((t, d), BF16))
    elif post == "final":
        in_specs.append(pl.BlockSpec((1, d), lambda i: (0, 0)))
        args.append(post_args[0])
        out_specs = [pl.BlockSpec((td, d), lambda i: (jnp.minimum(i, ns - 1), 0)),
                     pl.BlockSpec((td, d), lambda i: (jnp.maximum(i - ns, 0), 0))]
        out_shape = [jax.ShapeDtypeStruct((st.ts, d), F32), jax.ShapeDtypeStruct((st.tc, d), F32)]
    return pl.pallas_call(
        functools.partial(_combine_kernel, post=post, n_sample_tiles=ns),
        grid=(nt,),
        in_specs=in_specs,
        out_specs=out_specs,
        out_shape=out_shape,
        scratch_shapes=[pltpu.VMEM((2, td, d), F32), pltpu.VMEM((2, td, d), F32),
                        pltpu.SemaphoreType.DMA((2,))],
        compiler_params=_cparams(1),
        name="moe_combine",
    )(*args)


MOE_ROWS = 1024
MOE_DOWN_SPLIT = 2


def _tile_sources(tile_row, tile_expert, starts, counts):
    active = tile_row - starts[tile_expert] < counts[tile_expert]
    idx = jnp.arange(tile_row.shape[0], dtype=jnp.int32)
    return lax.cummax(jnp.where(active, idx, 0)).astype(jnp.int32)


def _moe_layer(x, p, modr, st, layer, moe_idx, post, post_args):
    t, d = x.shape
    n_experts = p["router_w"].shape[-1]
    d_ff = p["moe_w_gate"].shape[-1]
    tm = MOE_ROWS
    hp, w1, w2, code, rank, cnt = _route(x, p["norm2_g"].reshape(-1, 1, d), modr,
                                         p["router_w"][moe_idx], st, layer, n_experts)
    counts = cnt[:n_experts, 0]
    padded = (counts + tm - 1) // tm * tm
    ends = jnp.cumsum(padded)
    starts = ends - padded
    n_tiles = pl.cdiv(2 * t, tm) + n_experts
    n_slots = n_tiles * tm
    dest = starts[:, None] + rank[:n_experts]
    slot0 = jnp.sum(jnp.where(code[:n_experts] == 1, dest, 0), axis=0).astype(jnp.int32)
    slot1 = jnp.sum(jnp.where(code[:n_experts] == 2, dest, 0), axis=0).astype(jnp.int32)
    tile_row = jnp.arange(n_tiles, dtype=jnp.int32) * tm
    tile_expert = jnp.minimum(jnp.sum(tile_row[:, None] >= ends[None, :], axis=1), n_experts - 1)
    wid = (moe_idx * n_experts + tile_expert).astype(jnp.int32)
    expert_ids = jnp.arange(n_experts, dtype=jnp.int32)[:, None]
    token_ids = jnp.arange(t, dtype=jnp.int32)
    e0 = jnp.sum(jnp.where(code[:n_experts] == 1, expert_ids, 0), axis=0)
    e1 = jnp.sum(jnp.where(code[:n_experts] == 2, expert_ids, 0), axis=0)
    pair_token = jnp.sort(jnp.concatenate([e0 * t + token_ids, e1 * t + token_ids])) % t
    slot_expert = jnp.repeat(tile_expert, tm)
    local = jnp.arange(n_slots, dtype=jnp.int32) - starts[slot_expert]
    compact_start = jnp.cumsum(counts) - counts
    compact = compact_start[slot_expert] + jnp.clip(local, 0, jnp.maximum(counts[slot_expert] - 1, 0))
    slot_token = pair_token[jnp.clip(compact, 0, 2 * t - 1)].astype(jnp.int32)

    xs = _dispatch(hp, slot_token, ends[-1:].astype(jnp.int32))
    wg = p["moe_w_gate"].reshape(-1, d, d_ff)
    wu = p["moe_w_up"].reshape(-1, d, d_ff)
    wd = p["moe_w_down"].reshape(-1, d_ff, d)
    hidden = _gmm(xs, (wg, wu), wid, _tile_sources(tile_row, tile_expert, starts, counts),
                  mode="swiglu", out_dtype=BF16, tm=tm, tn=_pick(d_ff, 512, LANES),
                  name="moe_gate_up")
    tm_d = tm // MOE_DOWN_SPLIT
    sub_expert = jnp.repeat(tile_expert, MOE_DOWN_SPLIT)
    sub_row = jnp.arange(n_tiles * MOE_DOWN_SPLIT, dtype=jnp.int32) * tm_d
    ys = _gmm(hidden, (wd,), jnp.repeat(wid, MOE_DOWN_SPLIT),
              _tile_sources(sub_row, sub_expert, starts, counts), mode="plain", out_dtype=F32,
              tm=tm_d, tn=_pick(d, 512, LANES), name="moe_down")
    return _combine(ys, slot0, slot1, x, modr, w1, w2, st, layer, post, post_args)


def _dense_layer(x, p, modr, st, layer, ffn_idx):
    t, d = x.shape
    d_ff = p["ffn_w_gate"].shape[-1]
    h = _norm_mod(x, p["norm2_g"].reshape(-1, 1, d), modr, st, layer, 4, 3)
    hidden = _gmm(h, (p["ffn_w_gate"], p["ffn_w_up"]), ffn_idx, None, mode="swiglu",
                  out_dtype=BF16, tm=st.token_tile(1024), tn=_pick(d_ff, 512, LANES),
                  name="ffn_gate_up")
    tm = st.token_tile(512)
    return _gmm(hidden, (p["ffn_w_down"],), ffn_idx, None, mode="resid", out_dtype=F32, tm=tm,
                tn=_pick(d, 512, LANES), resid=(x, modr, _mod_index(st, layer, 5, tm)),
                name="ffn_down")


def _dft_radix(seq_len):
    r = 1
    while seq_len // r > 512 and r < 8:
        r *= 2
    return r


def kernel(x_prompt, x_sample, cache_k, cache_v, c, c_ctx, norm1_g, norm2_g, w_mod, b_mod, w_in,
           q_norm_g, k_norm_g, hy_conv_w, hy_conv_b, filt_w1, filt_b1, filt_w2, filt_b2, filt_w3,
           filt_log_decay, hy_skip, out_norm_g, w_out, ffn_w_gate, ffn_w_up, ffn_w_down, router_w,
           moe_w_gate, moe_w_up, moe_w_down, final_norm_g):
    p = dict(norm2_g=norm2_g, hy_conv_w=hy_conv_w, hy_conv_b=hy_conv_b, filt_w1=filt_w1,
             filt_b1=filt_b1, filt_w2=filt_w2, filt_b2=filt_b2, filt_w3=filt_w3,
             filt_log_decay=filt_log_decay, hy_skip=hy_skip, ffn_w_gate=ffn_w_gate,
             ffn_w_up=ffn_w_up, ffn_w_down=ffn_w_down, router_w=router_w, moe_w_gate=moe_w_gate,
             moe_w_up=moe_w_up, moe_w_down=moe_w_down)
    batch, seq, d = x_prompt.shape
    dec_batch, dec_seq, _ = x_sample.shape
    depth = w_in.shape[0]
    in_w = w_in.shape[-1]
    n_kv = cache_k.shape[3]
    kv_w = n_kv * HEAD_DIM
    a_w = (3 * d + 2 * kv_w - in_w) // 2
    hy_w = d - a_w
    u_col0 = a_w + 2 * kv_w
    past = cache_k.shape[2]
    st = _Stream(dec_batch, dec_seq, batch, seq)
    assert dec_batch + 1 <= MOD_ROWS

    cvecs = jnp.zeros((MOD_ROWS, d), F32).at[:dec_batch].set(c).at[dec_batch].set(c_ctx)
    modr = _mod_all(cvecs, w_mod, b_mod).reshape(depth * MOD_ROWS * 6, 1, d)
    cos2, sin2 = _rope_tables(st)
    dfts = {n: _Dft(n, _dft_radix(n)) for n in {dec_seq, seq}}
    norm1_g3 = norm1_g.reshape(depth, 1, d)
    qg3 = q_norm_g.reshape(depth, 1, HEAD_DIM)
    kg3 = k_norm_g.reshape(depth, 1, HEAD_DIM)
    out_norm_g3 = out_norm_g.reshape(depth, 1, d)
    cache_k4 = cache_k.reshape(dec_batch, depth, past, kv_w)
    cache_v4 = cache_v.reshape(dec_batch, depth, past, kv_w)

    new_k, new_v = [], []
    y = None
    x, h = _join_norm_mod(x_sample.reshape(st.ts, d), x_prompt.reshape(st.tc, d), norm1_g3, modr,
                          st, 0, 1, 0)
    for layer in range(depth):
        if h is None:
            h = _norm_mod(x, norm1_g3, modr, st, layer, 1, 0)
        proj = _gmm(h, (w_in,), layer, None, mode="plain", out_dtype=BF16,
                    tm=st.token_tile(2048), tn=_pick(in_w, 768, LANES), name="w_in")
        qn, kn, kc = _qk_norm_rope(proj, qg3, kg3, cos2, sin2, layer, a_w, kv_w, st)
        new_k.append(kc[st.ts:].reshape(batch, seq, n_kv, HEAD_DIM))
        new_v.append(proj[st.ts:, a_w + kv_w:a_w + 2 * kv_w].astype(F32)
                     .reshape(batch, seq, n_kv, HEAD_DIM))

        attn_s = _attention(qn, kn, proj, a_w + kv_w, n_kv, dec_batch, dec_seq, 0,
                            cache=(cache_k4, cache_v4, layer))
        attn_c = _attention(qn, kn, proj, a_w + kv_w, n_kv, batch, seq, st.ts)
        hy_s, hy_c = _hyena(proj, layer, p, st, u_col0, hy_w, dfts)
        tm = st.token_tile(512)
        x = _gmm((attn_s, attn_c, hy_s, hy_c), (w_out,), layer, None, mode="resid", out_dtype=F32,
                 tm=tm, tn=_pick(d, 1024, LANES), merge=(st.ts, out_norm_g3, layer),
                 resid=(x, modr, _mod_index(st, layer, 2, tm)), name="w_out")
        h = None
        if layer % 2 == 0:
            x = _dense_layer(x, p, modr, st, layer, layer // 2)
        elif layer + 1 < depth:
            x, h = _moe_layer(x, p, modr, st, layer, layer // 2, "norm_mod", (norm1_g3,))
        else:
            y = _moe_layer(x, p, modr, st, layer, layer // 2, "final",
                           (final_norm_g.reshape(1, d),))
    if y is None:
        y = _final_norm(x, final_norm_g)
        y = (y[:st.ts], y[st.ts:])
    y_sample = y[0].reshape(dec_batch, dec_seq, d)
    y_prompt = y[1].reshape(batch, seq, d)
    return (y_prompt, y_sample, jnp.stack(new_k, axis=1), jnp.stack(new_v, axis=1))
```
